```python
import jax, jax.numpy as jnp
from jax import lax
import numpy as np

D_MODEL = 1024
BATCH = 16
SEQ = 2048
DEPTH = 1

CTX_LEN = 256
GRID_W = 64

RW_WIDTH = 512
RW_HEAD = 64
RW_HEADS = RW_WIDTH // RW_HEAD
RW_DECAY_LORA = 32
RW_ICLR_LORA = 32
RW_GATE_LORA = 96
RW_GN_EPS = 64e-5
GLA_HEADS = 4
GLA_DK = 64
GLA_DV = 128
GLA_KEY = GLA_HEADS * GLA_DK
GLA_VAL = GLA_HEADS * GLA_DV
GLA_GATE_LORA = 16
GLA_GATE_NORM = 16.0
GLA_CHUNK = 64
GLA_NORM_EPS = 1e-5
MIX_WIDTH = RW_WIDTH + GLA_VAL

RW_SIZES = (RW_WIDTH, RW_WIDTH, RW_WIDTH, RW_DECAY_LORA, RW_DECAY_LORA,
            RW_ICLR_LORA, RW_ICLR_LORA, RW_GATE_LORA)
RW_IN = 3 * RW_WIDTH + 2 * RW_DECAY_LORA + 2 * RW_ICLR_LORA + RW_GATE_LORA
GLA_SIZES = (GLA_KEY, GLA_KEY, GLA_VAL, GLA_VAL, GLA_GATE_LORA, GLA_GATE_LORA)
GLA_IN = 2 * GLA_KEY + 2 * GLA_VAL + 2 * GLA_GATE_LORA
IN_WIDTH = RW_IN + GLA_IN

N_EXPERTS = 16
EXPERT_FF = 2816
CAPACITY_FACTOR = 2

DN_ALPHA = (2 * DEPTH) ** 0.25
DN_BETA = (8 * DEPTH) ** -0.25
LN_EPS = 1e-5

kernel_name = 'hymba_rwkv7_gla_ecmoe_diffusion_block'


def split_cols(z, sizes):
    offsets = []
    acc = 0
    for s in sizes[:-1]:
        acc += s
        offsets.append(acc)
    return jnp.split(z, offsets, axis=-1)


def ln_plain(x):
    xf = x.astype(jnp.float32)
    mu = jnp.mean(xf, -1, keepdims=True)
    var = jnp.mean(jnp.square(xf - mu), -1, keepdims=True)
    return ((xf - mu) * lax.rsqrt(var + LN_EPS)).astype(x.dtype)


def ln_affine(x, g, b):
    return ln_plain(x) * g + b


def modulate(h, shift, scale):
    return h * (1.0 + scale) + shift


def seq_shift_delta(z):
    zp = jnp.pad(z, ((0, 0), (1, 1), (0, 0)))
    return 0.5 * (zp[:, :-2] + zp[:, 2:]) - z


def grid_shift_delta(z):
    B, T, F = z.shape
    rows = T // GRID_W
    g = z.reshape(B, rows, GRID_W, F)
    gp = jnp.pad(g, ((0, 0), (1, 1), (1, 1), (0, 0)))
    nb = 0.25 * (gp[:, :-2, 1:-1] + gp[:, 2:, 1:-1] + gp[:, 1:-1, :-2] + gp[:, 1:-1, 2:])
    return nb.reshape(B, T, F) - z


def wkv7_scan(r, decay, kk, kka, k, v, s0, reverse):
    def step(s, inp):
        r_t, w_t, kk_t, kka_t, k_t, v_t = inp
        s_kk = jnp.einsum('bhvk,bhk->bhv', s, kk_t)
        s = s * w_t[:, :, None, :] - s_kk[..., None] * kka_t[:, :, None, :] + v_t[..., None] * k_t[:, :, None, :]
        y = jnp.einsum('bhvk,bhk->bhv', s, r_t)
        return s, y
    xs = tuple(jnp.moveaxis(t.astype(jnp.float32), 1, 0) for t in (r, decay, kk, kka, k, v))
    s, ys = lax.scan(step, s0, xs, reverse=reverse)
    return jnp.moveaxis(ys, 0, 1), s


def rwkv_direction(r_h, k, v_h, kk, wd, ad, w0, w2, a0, a2, k_a, r_k, s0, reverse):
    B, T = k.shape[:2]
    heads = lambda t: t.reshape(B, T, RW_HEADS, RW_HEAD)
    w_log = -jax.nn.softplus(-(w0 + jnp.tanh(wd) @ w2)) - 0.5
    decay = jnp.exp(-jnp.exp(w_log.astype(jnp.float32)))
    a = jax.nn.sigmoid(a0 + ad @ a2)
    k_h = heads(k * (1.0 + (a - 1.0) * k_a))
    y, s = wkv7_scan(r_h, heads(decay), kk, kk * heads(a), k_h, v_h, s0, reverse)
    bonus = jnp.sum(r_h * k_h * r_k, -1, keepdims=True) * v_h
    return y, bonus, s


def gla_chunked(q, k, v, log_a, s0):
    B, T, H, K = q.shape
    V = v.shape[-1]
    C = GLA_CHUNK
    n = T // C
    f32 = jnp.float32
    q = q.astype(f32).reshape(B, n, C, H, K)
    k = k.astype(f32).reshape(B, n, C, H, K)
    v = v.astype(f32).reshape(B, n, C, H, V)
    b = jnp.cumsum(log_a.astype(f32).reshape(B, n, C, H, K), axis=2)
    b_last = b[:, :, -1]
    q_in = q * jnp.exp(b)
    k_in = k * jnp.exp(-b)
    k_out = k * jnp.exp(b_last[:, :, None] - b)
    scores = jnp.einsum('bnihk,bnjhk->bnhij', q_in, k_in)
    mask = jnp.tril(jnp.ones((C, C), dtype=bool))
    scores = jnp.where(mask, scores, 0.0)
    o_intra = jnp.einsum('bnhij,bnjhv->bnihv', scores, v)
    d_state = jnp.einsum('bnjhk,bnjhv->bnhkv', k_out, v)

    def step(s, inp):
        ds_c, bl = inp
        return s * jnp.exp(bl)[..., None] + ds_c, s

    s_final, s_starts = lax.scan(step, s0, (jnp.moveaxis(d_state, 1, 0), jnp.moveaxis(b_last, 1, 0)))
    s_starts = jnp.moveaxis(s_starts, 0, 1)
    o_inter = jnp.einsum('bnihk,bnhkv->bnihv', q_in, s_starts)
    return (o_intra + o_inter).reshape(B, T, H, V), s_final


def mix_tokens(h, delta_fn, p, states0):
    B, T, _ = h.shape
    proj = jnp.einsum('btd,df->btf', h, p['w_in'])
    z_rw, z_gla = proj[..., :RW_IN], proj[..., RW_IN:]

    z_rw = z_rw + delta_fn(z_rw) * p['rw_mu']
    r, k, v, wd_f, wd_b, ad_f, ad_b, gd = split_cols(z_rw, RW_SIZES)
    heads = lambda t: t.reshape(B, T, RW_HEADS, RW_HEAD)
    r_h, v_h = heads(r), heads(v)
    kk = heads(k * p['rw_k_k']).astype(jnp.float32)
    kk = kk * lax.rsqrt(jnp.sum(kk * kk, -1, keepdims=True) + 1e-12)
    g_out = jax.nn.sigmoid(gd) @ p['rw_g2']
    y_f, bo_f, s_rw_f = rwkv_direction(r_h, k, v_h, kk, wd_f, ad_f, p['rw_w0'][0], p['rw_w2'][0],
                                       p['rw_a0'][0], p['rw_a2'][0], p['rw_k_a'], p['rw_r_k'],
                                       states0[0], False)
    y_b, bo_b, s_rw_b = rwkv_direction(r_h, k, v_h, kk, wd_b, ad_b, p['rw_w0'][1], p['rw_w2'][1],
                                       p['rw_a0'][1], p['rw_a2'][1], p['rw_k_a'], p['rw_r_k'],
                                       states0[1], True)
    y = y_f + y_b
    mu = jnp.mean(y, -1, keepdims=True)
    var = jnp.mean(jnp.square(y - mu), -1, keepdims=True)
    y_n = ((y - mu) * lax.rsqrt(var + RW_GN_EPS)).reshape(B, T, RW_WIDTH) * p['rw_gn_w'] + p['rw_gn_b']
    rw_out = (y_n + (bo_f + bo_b).reshape(B, T, RW_WIDTH)) * g_out

    q, kg, vg, gg, gad_f, gad_b = split_cols(z_gla, GLA_SIZES)
    q_h = q.reshape(B, T, GLA_HEADS, GLA_DK) * (GLA_DK ** -0.5)
    k_g = kg.reshape(B, T, GLA_HEADS, GLA_DK)
    v_g = vg.reshape(B, T, GLA_HEADS, GLA_DV)
    la_f = (jax.nn.log_sigmoid((gad_f @ p['gla_a2'][0] + p['gla_a_b'][0]).astype(jnp.float32))
            / GLA_GATE_NORM).reshape(B, T, GLA_HEADS, GLA_DK)
    la_b = (jax.nn.log_sigmoid((gad_b @ p['gla_a2'][1] + p['gla_a_b'][1]).astype(jnp.float32))
            / GLA_GATE_NORM).reshape(B, T, GLA_HEADS, GLA_DK)
    o_f, s_gla_f = gla_chunked(q_h, k_g, v_g, la_f, states0[2])
    flip = lambda t: jnp.flip(t, axis=1)
    o_b, s_gla_b = gla_chunked(flip(q_h), flip(k_g), flip(v_g), flip(la_b), states0[3])
    o = o_f + flip(o_b)
    o = o * lax.rsqrt(jnp.mean(jnp.square(o), -1, keepdims=True) + GLA_NORM_EPS)
    gla_out = o.reshape(B, T, GLA_VAL) * p['gla_norm_w'] * jax.nn.silu(gg)

    return jnp.concatenate([rw_out, gla_out], axis=-1), (s_rw_f, s_rw_b, s_gla_f, s_gla_b)


def expert_choice_ffn(h, router_w, w_gate, w_up, w_down):
    B, T, D = h.shape
    cap = CAPACITY_FACTOR * T // N_EXPERTS
    logits = jnp.einsum('btd,de->bte', h, router_w).astype(jnp.float32)
    aff = jax.nn.softmax(logits, axis=-1)
    gates, idx = lax.top_k(jnp.swapaxes(aff, 1, 2), cap)
    xin = jax.vmap(lambda hb, ib: hb[ib])(h, idx)

    def expert(args):
        xe, wg, wu, wd = args
        return (jax.nn.silu(xe @ wg) * (xe @ wu)) @ wd

    ye = lax.map(expert, (jnp.swapaxes(xin, 0, 1), w_gate, w_up, w_down))
    ye = jnp.swapaxes(ye, 0, 1) * gates[..., None]
    out = jnp.zeros(h.shape, ye.dtype)
    return out.at[jnp.arange(B)[:, None, None], idx].add(ye)


def setup_inputs(seed: int = 0) -> dict:
    key = jax.random.key(seed)
    ks = jax.random.split(key, 30)
    f32 = jnp.float32
    nrm = lambda i, shape, s: s * jax.random.normal(ks[i], shape, f32)
    L, D, E, F = DEPTH, D_MODEL, N_EXPERTS, EXPERT_FF
    return {
        'x': nrm(0, (BATCH, SEQ, D), 1.0),
        'c': nrm(1, (BATCH, D), 1.0),
        'ctx': nrm(2, (BATCH, CTX_LEN, D), 1.0),
        'c_ctx': nrm(3, (D,), 1.0),
        'ada_w': nrm(4, (L, D, 6 * D), D ** -0.5),
        'ada_b': nrm(5, (L, 6 * D), 0.02),
        'w_in': nrm(6, (L, D, IN_WIDTH), D ** -0.5),
        'rw_mu': jax.random.uniform(ks[7], (L, RW_IN), f32),
        'rw_w0': jax.random.uniform(ks[8], (L, 2, RW_WIDTH), f32, -6.5, -1.5),
        'rw_w2': nrm(9, (L, 2, RW_DECAY_LORA, RW_WIDTH), 0.5 * RW_DECAY_LORA ** -0.5),
        'rw_a0': nrm(10, (L, 2, RW_WIDTH), 0.3),
        'rw_a2': nrm(11, (L, 2, RW_ICLR_LORA, RW_WIDTH), 0.5 * RW_ICLR_LORA ** -0.5),
        'rw_g2': nrm(12, (L, RW_GATE_LORA, RW_WIDTH), RW_GATE_LORA ** -0.5),
        'rw_k_k': 0.85 + nrm(13, (L, RW_WIDTH), 0.05),
        'rw_k_a': 1.0 + nrm(14, (L, RW_WIDTH), 0.05),
        'rw_r_k': nrm(15, (L, RW_HEADS, RW_HEAD), 0.1),
        'rw_gn_w': 1.0 + nrm(16, (L, RW_WIDTH), 0.05),
        'rw_gn_b': nrm(17, (L, RW_WIDTH), 0.02),
        'gla_a2': nrm(18, (L, 2, GLA_GATE_LORA, GLA_KEY), GLA_GATE_LORA ** -0.5),
        'gla_a_b': 1.0 + nrm(19, (L, 2, GLA_KEY), 0.5),
        'gla_norm_w': 1.0 + nrm(20, (L, GLA_VAL), 0.05),
        'w_out': nrm(21, (L, MIX_WIDTH, D), DN_BETA * MIX_WIDTH ** -0.5),
        'ln1_g': 1.0 + nrm(22, (L, D), 0.05),
        'ln1_b': nrm(23, (L, D), 0.02),
        'router_w': nrm(24, (L, D, E), D ** -0.5),
        'ex_gate': nrm(25, (L, E, D, F), D ** -0.5),
        'ex_up': nrm(26, (L, E, D, F), D ** -0.5),
        'ex_down': nrm(27, (L, E, F, D), DN_BETA * F ** -0.5),
        'ln2_g': 1.0 + nrm(28, (L, D), 0.05),
        'ln2_b': nrm(29, (L, D), 0.02),
    }


def reference(x, c, ctx, c_ctx, ada_w, ada_b, w_in, rw_mu, rw_w0, rw_w2, rw_a0, rw_a2, rw_g2,
              rw_k_k, rw_k_a, rw_r_k, rw_gn_w, rw_gn_b, gla_a2, gla_a_b, gla_norm_w, w_out,
              ln1_g, ln1_b, router_w, ex_gate, ex_up, ex_down, ln2_g, ln2_b):
    B = x.shape[0]
    f32 = jnp.float32
    zero_states = (jnp.zeros((B, RW_HEADS, RW_HEAD, RW_HEAD), f32),
                   jnp.zeros((B, RW_HEADS, RW_HEAD, RW_HEAD), f32),
                   jnp.zeros((B, GLA_HEADS, GLA_DK, GLA_DV), f32),
                   jnp.zeros((B, GLA_HEADS, GLA_DK, GLA_DV), f32))
    silu_c = jax.nn.silu(c)
    silu_cc = jax.nn.silu(c_ctx)
    for layer in range(DEPTH):
        last = layer == DEPTH - 1
        p = dict(w_in=w_in[layer], rw_mu=rw_mu[layer], rw_w0=rw_w0[layer], rw_w2=rw_w2[layer],
                 rw_a0=rw_a0[layer], rw_a2=rw_a2[layer], rw_g2=rw_g2[layer], rw_k_k=rw_k_k[layer],
                 rw_k_a=rw_k_a[layer], rw_r_k=rw_r_k[layer], rw_gn_w=rw_gn_w[layer],
                 rw_gn_b=rw_gn_b[layer], gla_a2=gla_a2[layer], gla_a_b=gla_a_b[layer],
                 gla_norm_w=gla_norm_w[layer])
        mod = (silu_c @ ada_w[layer] + ada_b[layer])[:, None, :]
        mod_c = silu_cc @ ada_w[layer] + ada_b[layer]
        sh1, sc1, g1, sh2, sc2, g2 = jnp.split(mod, 6, axis=-1)
        sh1c, sc1c, g1c, sh2c, sc2c, g2c = jnp.split(mod_c, 6, axis=-1)

        h_ctx = modulate(ln_plain(ctx), sh1c, sc1c)
        h_lat = modulate(ln_plain(x), sh1, sc1)
        o_ctx, ctx_states = mix_tokens(h_ctx, seq_shift_delta, p, zero_states)
        o_lat, _ = mix_tokens(h_lat, grid_shift_delta, p, ctx_states)
        x = ln_affine(DN_ALPHA * x + g1 * (o_lat @ w_out[layer]), ln1_g[layer], ln1_b[layer])

        h2 = modulate(ln_plain(x), sh2, sc2)
        moe = expert_choice_ffn(h2, router_w[layer], ex_gate[layer], ex_up[layer], ex_down[layer])
        x = ln_affine(DN_ALPHA * x + g2 * moe, ln2_g[layer], ln2_b[layer])

        if not last:
            ctx = ln_affine(DN_ALPHA * ctx + g1c * (o_ctx @ w_out[layer]), ln1_g[layer], ln1_b[layer])
            h2c = modulate(ln_plain(ctx), sh2c, sc2c)
            moe_c = expert_choice_ffn(h2c, router_w[layer], ex_gate[layer], ex_up[layer], ex_down[layer])
            ctx = ln_affine(DN_ALPHA * ctx + g2c * moe_c, ln2_g[layer], ln2_b[layer])
    return x
```

```python
import functools

import jax
import jax.numpy as jnp
from jax import lax
from jax.experimental import pallas as pl
from jax.experimental.pallas import tpu as pltpu

F32 = jnp.float32
BF16 = jnp.bfloat16
HIGHEST = lax.Precision.HIGHEST

D_MODEL = 1024
GRID_W = 64
RW_WIDTH = 512
RW_HEAD = 64
RW_GN_EPS = 64e-5
GLA_KEY = 256
GLA_VAL = 512
GLA_DV = 128
GLA_GATE_NORM = 16.0
GLA_NORM_EPS = 1e-5
N_EXPERTS = 16
CAPACITY_FACTOR = 2
DN_ALPHA = 2.0 ** 0.25
LN_EPS = 1e-5
CHUNK = 64

COL_R, COL_K, COL_V, COL_LORA, COL_GD = 0, 512, 1024, 1536, 1664
RW_PAD = 1792
COL_Q, COL_KG, COL_VG, COL_GG, COL_GA = 1792, 2048, 2304, 2816, 3328
IN_PAD = 3456

VMEM_LIMIT = 56 * 1024 * 1024


def _params(sem):
    return pltpu.CompilerParams(dimension_semantics=sem, vmem_limit_bytes=VMEM_LIMIT)


def _dot(a, b):
    return jnp.dot(a.astype(BF16), b.astype(BF16), preferred_element_type=F32)


def _dot_nt(a, b):
    return lax.dot_general(a.astype(BF16), b.astype(BF16), (((1,), (1,)), ((), ())),
                           preferred_element_type=F32)


def _dot_tn(a, b):
    return lax.dot_general(a.astype(BF16), b.astype(BF16), (((0,), (0,)), ((), ())),
                           preferred_element_type=F32)


def _dot_split(x, w):
    hi = x.astype(BF16)
    lo = (x - hi.astype(F32)).astype(BF16)
    return (jnp.dot(hi, w, preferred_element_type=F32) + jnp.dot(lo, w, preferred_element_type=F32))


def _sigmoid(x):
    return 1.0 / (1.0 + jnp.exp(-x))


def _softplus(x):
    return jnp.maximum(x, 0.0) + jnp.log(1.0 + jnp.exp(-jnp.abs(x)))


def _ln(x):
    mu = jnp.mean(x, axis=-1, keepdims=True)
    xc = x - mu
    var = jnp.mean(xc * xc, axis=-1, keepdims=True)
    return xc * lax.rsqrt(var + LN_EPS)


def _iota(shape, dim):
    return lax.broadcasted_iota(jnp.int32, shape, dim)


def _mod_kernel(c_ref, w_ref, b_ref, o_ref):
    c = c_ref[...]
    s = c * _sigmoid(c)
    o_ref[...] = jnp.dot(s, w_ref[...], precision=HIGHEST, preferred_element_type=F32) + b_ref[...]


def _mod_call(cs, ada_w, ada_b):
    rows, d = cs.shape
    n = ada_w.shape[1]
    bn = 1536
    return pl.pallas_call(
        _mod_kernel,
        grid=(n // bn,),
        in_specs=[pl.BlockSpec((rows, d), lambda j: (0, 0)),
                  pl.BlockSpec((d, bn), lambda j: (0, j)),
                  pl.BlockSpec((1, bn), lambda j: (0, j))],
        out_specs=pl.BlockSpec((rows, bn), lambda j: (0, j)),
        out_shape=jax.ShapeDtypeStruct((rows, n), F32),
        compiler_params=_params(("arbitrary",)),
        name="mod",
    )(cs, ada_w, ada_b)


def _inproj_kernel(x_ref, sh_ref, sc_ref, w_ref, o_ref):
    h = _ln(x_ref[0]) * (1.0 + sc_ref[0]) + sh_ref[0]
    o_ref[0] = jnp.dot(h.astype(BF16), w_ref[...], preferred_element_type=F32)


def _inproj_call(x, sh, sc, w):
    b, t, d = x.shape
    n = w.shape[1]
    tm = min(t, 512)
    return pl.pallas_call(
        _inproj_kernel,
        grid=(b, t // tm),
        in_specs=[pl.BlockSpec((1, tm, d), lambda i, j: (i, j, 0)),
                  pl.BlockSpec((1, 1, d), lambda i, j: (i, 0, 0)),
                  pl.BlockSpec((1, 1, d), lambda i, j: (i, 0, 0)),
                  pl.BlockSpec((d, n), lambda i, j: (0, 0))],
        out_specs=pl.BlockSpec((1, tm, n), lambda i, j: (i, j, 0)),
        out_shape=jax.ShapeDtypeStruct((b, t, n), F32),
        compiler_params=_params(("arbitrary", "arbitrary")),
        name="inproj",
    )(x, sh, sc, w)


_PREP_OUT = (
    ("r", 512, BF16), ("kk", 512, BF16), ("v", 512, BF16),
    ("kh_f", 512, BF16), ("kh_b", 512, BF16), ("kka_f", 512, BF16), ("kka_b", 512, BF16),
    ("lw_f", 512, F32), ("lw_b", 512, F32), ("gout", 512, F32), ("bonus", 512, F32),
    ("q", 256, BF16), ("kg", 256, BF16), ("vg", 512, BF16),
    ("la_f", 256, F32), ("la_b", 256, F32), ("gate", 512, F32),
)


def _prep_kernel(grid_shift, t_total, tr, *refs):
    if grid_shift:
        z_ref, zp_ref, zn_ref = refs[:3]
        refs = refs[3:]
    else:
        z_ref = refs[0]
        refs = refs[1:]
    (mu_ref, kk_p, ka_p, rk_p, w0_ref, a0_ref, ab_ref, wl_ref, g2_ref, a2_ref, seg_ref) = refs[:11]
    outs = dict(zip([n for n, _, _ in _PREP_OUT], refs[11:11 + len(_PREP_OUT)]))
    zs = refs[11 + len(_PREP_OUT)]

    i = pl.program_id(1)
    zc = z_ref[0, :, 0:RW_PAD]
    zs[0:8, :] = jnp.zeros((8, RW_PAD), F32)
    zs[8 + tr:16 + tr, :] = jnp.zeros((8, RW_PAD), F32)
    zs[8:8 + tr, :] = zc
    left = zs[7:7 + tr, :]
    right = zs[9:9 + tr, :]
    row = _iota((tr, 1), 0) + i * tr
    if grid_shift:
        col = row & (GRID_W - 1)
        left = jnp.where(col > 0, left, 0.0)
        right = jnp.where(col < GRID_W - 1, right, 0.0)
        up = jnp.concatenate([zp_ref[0], zc[:tr - GRID_W]], axis=0)
        up = jnp.where(row >= GRID_W, up, 0.0)
        down = jnp.concatenate([zc[GRID_W:], zn_ref[0]], axis=0)
        down = jnp.where(row < t_total - GRID_W, down, 0.0)
        nb = 0.25 * (up + down + left + right)
    else:
        left = jnp.where(row > 0, left, 0.0)
        right = jnp.where(row < t_total - 1, right, 0.0)
        nb = 0.5 * (left + right)
    zr = zc + (nb - zc) * mu_ref[...]

    r = zr[:, COL_R:COL_R + 512]
    k = zr[:, COL_K:COL_K + 512]
    v = zr[:, COL_V:COL_V + 512]
    lo = zr[:, COL_LORA:COL_LORA + 128]
    gd = zr[:, COL_GD:COL_GD + 128]
    seg = seg_ref[...]

    lane = _iota((1, 128), 1)
    lor = _dot(jnp.where(lane < 64, jnp.tanh(lo), lo), wl_ref[...])
    kkr = k * kk_p[...]
    kk = kkr * lax.rsqrt(_dot_split(kkr * kkr, seg) + 1e-12)
    outs["r"][0] = r.astype(BF16)
    outs["kk"][0] = kk.astype(BF16)
    outs["v"][0] = v.astype(BF16)
    outs["gout"][0] = _dot(_sigmoid(gd), g2_ref[...])
    rrk = r * rk_p[...]
    bon = None
    for d, sfx in ((0, "_f"), (1, "_b")):
        w_log = -_softplus(-(lor[:, d * 512:(d + 1) * 512] + w0_ref[d:d + 1, :])) - 0.5
        outs["lw" + sfx][0] = -jnp.exp(w_log)
        a = _sigmoid(a0_ref[d:d + 1, :] + lor[:, 1024 + d * 512:1536 + d * 512])
        kh = k * (1.0 + (a - 1.0) * ka_p[...])
        outs["kh" + sfx][0] = kh.astype(BF16)
        outs["kka" + sfx][0] = (kk * a).astype(BF16)
        s = _dot_split(rrk * kh, seg)
        bon = s if bon is None else bon + s
    outs["bonus"][0] = bon * v

    outs["q"][0] = (z_ref[0, :, COL_Q:COL_Q + 256] * (RW_HEAD ** -0.5)).astype(BF16)
    outs["kg"][0] = z_ref[0, :, COL_KG:COL_KG + 256].astype(BF16)
    outs["vg"][0] = z_ref[0, :, COL_VG:COL_VG + 512].astype(BF16)
    gg = z_ref[0, :, COL_GG:COL_GG + 512]
    outs["gate"][0] = gg * _sigmoid(gg)
    gl = _dot(z_ref[0, :, COL_GA:COL_GA + 128], a2_ref[...])
    for d, sfx in ((0, "_f"), (1, "_b")):
        xg = gl[:, d * 256:(d + 1) * 256] + ab_ref[d:d + 1, :]
        outs["la" + sfx][0] = -_softplus(-xg) * (1.0 / GLA_GATE_NORM)


def _prep_call(z, grid_shift, pp):
    b, t, n = z.shape
    tr = min(t, 512)
    nt = t // tr
    hb = tr // GRID_W
    nh = t // GRID_W
    full = lambda shape: pl.BlockSpec(shape, lambda i, j: (0,) * len(shape))
    in_specs = [pl.BlockSpec((1, tr, n), lambda i, j: (i, j, 0))]
    args = [z]
    if grid_shift:
        in_specs += [
            pl.BlockSpec((1, GRID_W, RW_PAD), lambda i, j: (i, jnp.maximum(j * hb - 1, 0), 0)),
            pl.BlockSpec((1, GRID_W, RW_PAD), lambda i, j: (i, jnp.minimum((j + 1) * hb, nh - 1), 0)),
        ]
        args += [z, z]
    names = ("mu", "k_k", "k_a", "r_k", "w0", "a0", "ab", "w_lora", "g2", "a2", "seg64")
    for nm in names:
        in_specs.append(full(pp[nm].shape))
        args.append(pp[nm])
    out_specs = [pl.BlockSpec((1, tr, w), lambda i, j: (i, j, 0)) for _, w, _ in _PREP_OUT]
    out_shape = [jax.ShapeDtypeStruct((b, t, w), dt) for _, w, dt in _PREP_OUT]
    res = pl.pallas_call(
        functools.partial(_prep_kernel, grid_shift, t, tr),
        grid=(b, nt),
        in_specs=in_specs,
        out_specs=out_specs,
        out_shape=out_shape,
        scratch_shapes=[pltpu.VMEM((tr + 16, RW_PAD), F32)],
        compiler_params=_params(("arbitrary", "arbitrary")),
        name="prep_grid" if grid_shift else "prep_seq",
    )(*args)
    return dict(zip([nm for nm, _, _ in _PREP_OUT], res))


def _bd(x, blk):
    reps = blk.shape[0] // x.shape[0]
    return jnp.where(blk, jnp.concatenate([x] * reps, axis=0), jnp.zeros((), x.dtype))


INV_SPLIT = False


def _split(x):
    hi = x.astype(BF16)
    return hi, (x - hi.astype(F32)).astype(BF16)


def _bd_hl(x, blk):
    if not INV_SPLIT:
        return (_bd(x.astype(BF16), blk),)
    hi, lo = _split(x)
    return _bd(hi, blk), _bd(lo, blk)


def _mm(a, b):
    if not INV_SPLIT:
        return jnp.dot(a.astype(BF16), b[0], preferred_element_type=F32)
    ah, al = _split(a)
    f = lambda p, q: jnp.dot(p, q, preferred_element_type=F32)
    return f(ah, b[0]) + (f(al, b[0]) + f(ah, b[1]))


def _scan_masks(rev, width):
    c = CHUNK
    ri = _iota((c, width), 0)
    jj = _iota((c, width), 1) & (c - 1)
    strict = (jj > ri) if rev else (jj < ri)
    incl = (jj >= ri) if rev else (jj <= ri)
    eye = jj == ri
    a = _iota((c, c), 0)
    bq = _iota((c, c), 1)
    mcum = jnp.where((bq >= a) if rev else (bq <= a), 1.0, 0.0).astype(BF16)
    levels = [strict & ((jj >> (l + 1)) == (ri >> (l + 1))) & ((jj >> l) != (ri >> l)) for l in range(6)]
    return strict, incl, eye, mcum, levels


def _tri_inverse(lo, masks, blk):
    _, _, eye, _, levels = masks
    tm = jnp.where(eye, 1.0, 0.0) - jnp.where(levels[0], lo, 0.0)
    for l in range(1, 6):
        cl = jnp.where(levels[l], lo, 0.0)
        x = _mm(tm, _bd_hl(cl, blk))
        tm = tm - _mm(x, _bd_hl(tm, blk))
    return tm


def _cumsum(x, mcum):
    hi = x.astype(BF16)
    lo = (x - hi.astype(F32)).astype(BF16)
    return (jnp.dot(mcum, hi, preferred_element_type=F32) + jnp.dot(mcum, lo, preferred_element_type=F32))


def _wkv_chain(rev, need_y, r, kk, kka, kh, v, lw, ht, masks, blk):
    strict, incl, eye, mcum, _ = masks
    c = CHUNK
    g = _cumsum(lw, mcum)
    gl = g[0:1] if rev else g[c - 1:c]
    eng = jnp.exp(-g)
    eo = jnp.exp(gl - g)
    rt = (r * jnp.exp(g)).astype(BF16)
    bt = (kk * jnp.exp(g - lw)).astype(BF16)
    at = (kka * eng).astype(BF16)
    kt = (kh * eng).astype(BF16)
    ao = (kka * eo).astype(BF16)
    ko = (kh * eo).astype(BF16)
    vb = v.astype(BF16)
    x2 = jnp.concatenate([bt, rt], axis=0)
    sa = _dot_nt(x2, _bd(at, blk))
    sk = _dot_nt(x2, _bd(kt, blk))
    lk = jnp.where(strict, sk[:c], 0.0)
    tm = _tri_inverse(sa[:c], masks, blk)
    xh = _dot_nt(x2, ht)
    vbd = _bd(vb, blk)
    u = -_dot(tm, _bd((xh[:c] + _dot(lk, vbd)).astype(BF16), blk))
    ub = u.astype(BF16)
    upd = _dot_tn(jnp.concatenate([ub, vb], axis=0), jnp.concatenate([ao, ko], axis=0))
    ht_new = ht * jnp.exp(gl) + jnp.where(blk, upd, 0.0)
    y = None
    if need_y:
        mra = jnp.where(incl, sa[c:], 0.0)
        mrk = jnp.where(incl, sk[c:], 0.0)
        y = xh[c:] + _dot(mra, _bd(ub, blk)) + _dot(mrk, vbd)
    return y, ht_new


def _wkv_kernel(need_y, t, r_ref, kk_ref, v_ref, khf_ref, khb_ref, kaf_ref, kab_ref,
                lwf_ref, lwb_ref, h0_ref, *rest):
    if need_y:
        y_ref, hout_ref, hs = rest
    else:
        hout_ref, hs = rest
    c = CHUNK
    n = t // c
    w = 256
    blk = (_iota((w, w), 0) >> 6) == (_iota((w, w), 1) >> 6)
    masks = (_scan_masks(False, w), _scan_masks(True, w))
    hs[...] = h0_ref[0, 0]
    if need_y:
        y_ref[...] = jnp.zeros(y_ref.shape, F32)

    def body(ci, carry):
        for d, (kh_ref, ka_ref, lw_ref) in enumerate(((khf_ref, kaf_ref, lwf_ref),
                                                      (khb_ref, kab_ref, lwb_ref))):
            cc = ci if d == 0 else n - 1 - ci
            rows = pl.ds(pl.multiple_of(cc * c, c), c)
            ld = lambda ref: ref[0, rows, :].astype(F32)
            y, ht = _wkv_chain(d == 1, need_y, ld(r_ref), ld(kk_ref), ld(ka_ref), ld(kh_ref),
                               ld(v_ref), lw_ref[0, rows, :], hs[d], masks[d], blk)
            hs[d] = ht
            if need_y:
                y_ref[0, rows, :] += y
        return carry

    lax.fori_loop(0, n, body, 0)
    hout_ref[0, 0] = hs[...]


def _wkv_call(p, h0, need_y):
    b, t, _ = p["r"].shape
    ng = RW_WIDTH // 256
    seq = lambda: pl.BlockSpec((1, t, 256), lambda i, g: (i, 0, g))
    st = pl.BlockSpec((1, 1, 2, 256, 256), lambda i, g: (i, g, 0, 0, 0))
    out_specs = [st]
    out_shape = [jax.ShapeDtypeStruct((b, ng, 2, 256, 256), F32)]
    if need_y:
        out_specs = [seq()] + out_specs
        out_shape = [jax.ShapeDtypeStruct((b, t, RW_WIDTH), F32)] + out_shape
    res = pl.pallas_call(
        functools.partial(_wkv_kernel, need_y, t),
        grid=(b, ng),
        in_specs=[seq() for _ in range(9)] + [st],
        out_specs=out_specs,
        out_shape=out_shape,
        scratch_shapes=[pltpu.VMEM((2, 256, 256), F32)],
        compiler_params=_params(("arbitrary", "arbitrary")),
        name="wkv_y" if need_y else "wkv_state",
    )(p["r"], p["kk"], p["v"], p["kh_f"], p["kh_b"], p["kka_f"], p["kka_b"], p["lw_f"], p["lw_b"], h0)
    return res if need_y else (None, res[0])


def _gla_chain(rev, need_o, q, k, v, la, st, masks, blk_k, blk_v, blk_s):
    _, incl, _, mcum, _ = masks
    c = CHUNK
    b = _cumsum(la, mcum)
    bl = b[0:1] if rev else b[c - 1:c]
    q_in = (q * jnp.exp(b)).astype(BF16)
    k_in = (k * jnp.exp(-b)).astype(BF16)
    k_out = (k * jnp.exp(bl - b)).astype(BF16)
    vb = v.astype(BF16)
    st_new = st * jnp.exp(bl) + jnp.where(blk_s, _dot_tn(vb, k_out), 0.0)
    o = None
    if need_o:
        sc = jnp.where(incl, _dot_nt(q_in, _bd(k_in, blk_k)), 0.0)
        o = _dot(sc, _bd(vb, blk_v)) + _dot_nt(q_in, st)
    return o, st_new


def _gla_kernel(need_o, t, q_ref, k_ref, v_ref, laf_ref, lab_ref, s0_ref, *rest):
    if need_o:
        o_ref, sout_ref, ss = rest
    else:
        sout_ref, ss = rest
    c = CHUNK
    n = t // c
    blk_k = (_iota((256, 256), 0) >> 6) == (_iota((256, 256), 1) >> 6)
    blk_v = (_iota((256, 512), 0) >> 6) == (_iota((256, 512), 1) >> 7)
    blk_s = (_iota((512, 256), 0) >> 7) == (_iota((512, 256), 1) >> 6)
    masks = (_scan_masks(False, 256), _scan_masks(True, 256))
    ss[...] = s0_ref[0]
    if need_o:
        o_ref[...] = jnp.zeros(o_ref.shape, F32)

    def body(ci, carry):
        for d, la_ref in enumerate((laf_ref, lab_ref)):
            cc = ci if d == 0 else n - 1 - ci
            rows = pl.ds(pl.multiple_of(cc * c, c), c)
            ld = lambda ref: ref[0, rows, :].astype(F32)
            o, st = _gla_chain(d == 1, need_o, ld(q_ref), ld(k_ref), ld(v_ref), la_ref[0, rows, :],
                               ss[d], masks[d], blk_k, blk_v, blk_s)
            ss[d] = st
            if need_o:
                o_ref[0, rows, :] += o
        return carry

    lax.fori_loop(0, n, body, 0)
    sout_ref[0] = ss[...]


def _gla_call(p, s0, need_o):
    b, t, _ = p["q"].shape
    seq = lambda w: pl.BlockSpec((1, t, w), lambda i: (i, 0, 0))
    st = pl.BlockSpec((1, 2, GLA_VAL, GLA_KEY), lambda i: (i, 0, 0, 0))
    out_specs = [st]
    out_shape = [jax.ShapeDtypeStruct((b, 2, GLA_VAL, GLA_KEY), F32)]
    if need_o:
        out_specs = [seq(GLA_VAL)] + out_specs
        out_shape = [jax.ShapeDtypeStruct((b, t, GLA_VAL), F32)] + out_shape
    res = pl.pallas_call(
        functools.partial(_gla_kernel, need_o, t),
        grid=(b,),
        in_specs=[seq(256), seq(256), seq(512), seq(256), seq(256), st],
        out_specs=out_specs,
        out_shape=out_shape,
        scratch_shapes=[pltpu.VMEM((2, GLA_VAL, GLA_KEY), F32)],
        compiler_params=_params(("arbitrary",)),
        name="gla_o" if need_o else "gla_state",
    )(p["q"], p["kg"], p["vg"], p["la_f"], p["la_b"], s0)
    return res if need_o else (None, res[0])


def _post_kernel(y_ref, bon_ref, gout_ref, o_ref, gate_ref, x_ref, g1_ref, gnw_ref, gnb_ref,
                 nw_ref, wout_ref, l1g_ref, l1b_ref, seg64_ref, seg128_ref, out_ref):
    y = y_ref[0]
    seg64 = seg64_ref[...]
    mu = _dot_split(y, seg64) * (1.0 / RW_HEAD)
    yc = y - mu
    var = _dot_split(yc * yc, seg64) * (1.0 / RW_HEAD)
    y_n = yc * lax.rsqrt(var + RW_GN_EPS) * gnw_ref[...] + gnb_ref[...]
    rw_out = (y_n + bon_ref[0]) * gout_ref[0]
    o = o_ref[0]
    ms = _dot_split(o * o, seg128_ref[...]) * (1.0 / GLA_DV)
    gla_out = o * lax.rsqrt(ms + GLA_NORM_EPS) * nw_ref[...] * gate_ref[0]
    mix = jnp.concatenate([rw_out.astype(BF16), gla_out.astype(BF16)], axis=-1)
    proj = jnp.dot(mix, wout_ref[...], preferred_element_type=F32)
    xs = DN_ALPHA * x_ref[0] + g1_ref[0] * proj
    out_ref[0] = _ln(xs) * l1g_ref[...] + l1b_ref[...]


def _post_call(y, bonus, gout, o, gate, x, g1, pp):
    b, t, d = x.shape
    tm = min(t, 512)
    tile = lambda w: pl.BlockSpec((1, tm, w), lambda i, j: (i, j, 0))
    full = lambda a: pl.BlockSpec(a.shape, lambda i, j: (0,) * a.ndim)
    consts = [pp["gn_w"], pp["gn_b"], pp["gla_nw"], pp["w_out"], pp["ln1_g"], pp["ln1_b"],
              pp["seg64"], pp["seg128"]]
    return pl.pallas_call(
        _post_kernel,
        grid=(b, t // tm),
        in_specs=[tile(512), tile(512), tile(512), tile(512), tile(512), tile(d),
                  pl.BlockSpec((1, 1, d), lambda i, j: (i, 0, 0))] + [full(a) for a in consts],
        out_specs=tile(d),
        out_shape=jax.ShapeDtypeStruct((b, t, d), F32),
        compiler_params=_params(("arbitrary", "arbitrary")),
        name="post",
    )(y, bonus, gout, o, gate, x, g1, *consts)


def _prefix_excl(m, su):
    e, t = m.shape
    off = jnp.zeros((e, 1), F32)
    parts = []
    for j in range(t // 128):
        blk = m[:, j * 128:(j + 1) * 128]
        parts.append(_dot(blk, su) + off)
        off = off + jnp.sum(blk, axis=1, keepdims=True)
    return jnp.concatenate(parts, axis=1)


def _route_kernel(cap, x_ref, sh_ref, sc_ref, rw_ref, xg_ref, aff_ref, pos_ref, h2_s, pe_s):
    e = pl.program_id(1)
    t = x_ref.shape[1]

    @pl.when(e == 0)
    def _():
        h2 = _ln(x_ref[0]) * (1.0 + sc_ref[0]) + sh_ref[0]
        h2_s[...] = h2.astype(BF16)
        logits = jnp.dot(h2, rw_ref[...], precision=HIGHEST, preferred_element_type=F32)
        lane = _iota((1, 128), 1)
        logits = jnp.where(lane < N_EXPERTS, logits, -1e30)
        m = jnp.max(logits, axis=-1, keepdims=True)
        ex = jnp.exp(logits - m)
        aff = ex / jnp.sum(ex, axis=-1, keepdims=True)
        aff_ref[0] = aff
        aff_t = aff.T[0:N_EXPERTS, :]
        bits = pltpu.bitcast(aff_t, jnp.int32)

        def bs(_, lohi):
            lo, hi = lohi
            mid = lo + ((hi - lo) >> 1)
            cnt = jnp.sum(jnp.where(bits >= mid, 1.0, 0.0), axis=1, keepdims=True)
            ok = cnt >= cap
            return jnp.where(ok, mid, lo), jnp.where(ok, hi, mid)

        lo0 = jnp.zeros((N_EXPERTS, 1), jnp.int32)
        hi0 = jnp.full((N_EXPERTS, 1), 0x3F800001, jnp.int32)
        thr, _ = lax.fori_loop(0, 31, bs, (lo0, hi0))
        gt = jnp.where(bits > thr, 1.0, 0.0)
        eq = jnp.where(bits == thr, 1.0, 0.0)
        need = cap - jnp.sum(gt, axis=1, keepdims=True)
        su = jnp.where(_iota((128, 128), 0) < _iota((128, 128), 1), 1.0, 0.0).astype(BF16)
        sel = gt + eq * jnp.where(_prefix_excl(eq, su) < need, 1.0, 0.0)
        pos = jnp.where(sel > 0.0, _prefix_excl(sel, su), -1.0)
        pe_s[0:N_EXPERTS, :] = pos
        pe_s[N_EXPERTS:, :] = jnp.full((128 - N_EXPERTS, t), -1.0, F32)
        pos_ref[0] = pe_s[...].T

    prow = pe_s[pl.ds(e, 1), :]
    slot = _iota((cap, t), 0).astype(F32)
    onehot = jnp.where(prow == slot, 1.0, 0.0).astype(BF16)
    xg_ref[0, 0] = jnp.dot(onehot, h2_s[...], preferred_element_type=F32).astype(BF16)


def _route_call(x1, sh2, sc2, router_pad, cap):
    b, t, d = x1.shape
    return pl.pallas_call(
        functools.partial(_route_kernel, cap),
        grid=(b, N_EXPERTS),
        in_specs=[pl.BlockSpec((1, t, d), lambda i, e: (i, 0, 0)),
                  pl.BlockSpec((1, 1, d), lambda i, e: (i, 0, 0)),
                  pl.BlockSpec((1, 1, d), lambda i, e: (i, 0, 0)),
                  pl.BlockSpec((d, 128), lambda i, e: (0, 0))],
        out_specs=[pl.BlockSpec((1, 1, cap, d), lambda i, e: (i, e, 0, 0)),
                   pl.BlockSpec((1, t, 128), lambda i, e: (i, 0, 0)),
                   pl.BlockSpec((1, t, 128), lambda i, e: (i, 0, 0))],
        out_shape=[jax.ShapeDtypeStruct((b, N_EXPERTS, cap, d), BF16),
                   jax.ShapeDtypeStruct((b, t, 128), F32),
                   jax.ShapeDtypeStruct((b, t, 128), F32)],
        scratch_shapes=[pltpu.VMEM((t, d), BF16), pltpu.VMEM((128, t), F32)],
        compiler_params=_params(("arbitrary", "arbitrary")),
        name="route",
    )(x1, sh2, sc2, router_pad)


def _ffn_kernel(x_ref, wg_ref, wu_ref, wd_ref, y_ref):
    x = x_ref[0, 0]
    g = jnp.dot(x, wg_ref[0], preferred_element_type=F32)
    u = jnp.dot(x, wu_ref[0], preferred_element_type=F32)
    h = (g * _sigmoid(g) * u).astype(BF16)
    y_ref[0, 0] = jnp.dot(h, wd_ref[0], preferred_element_type=F32).astype(BF16)


def _ffn_call(xg, wg, wu, wd):
    b, ne, cap, d = xg.shape
    f = wg.shape[2]
    return pl.pallas_call(
        _ffn_kernel,
        grid=(ne, b),
        in_specs=[pl.BlockSpec((1, 1, cap, d), lambda e, i: (i, e, 0, 0)),
                  pl.BlockSpec((1, d, f), lambda e, i: (e, 0, 0)),
                  pl.BlockSpec((1, d, f), lambda e, i: (e, 0, 0)),
                  pl.BlockSpec((1, f, d), lambda e, i: (e, 0, 0))],
        out_specs=pl.BlockSpec((1, 1, cap, d), lambda e, i: (i, e, 0, 0)),
        out_shape=jax.ShapeDtypeStruct((b, ne, cap, d), BF16),
        compiler_params=_params(("arbitrary", "arbitrary")),
        name="ffn",
    )(xg, wg, wu, wd)


def _combine_kernel(cap, y_ref, pos_ref, aff_ref, x_ref, g2_ref, l2g_ref, l2b_ref, ex_ref, out_ref):
    tm = x_ref.shape[1]
    ne = N_EXPERTS
    posb = _dot(pos_ref[0], ex_ref[...])
    slot = (_iota((1, ne * cap), 1) & (cap - 1)).astype(F32)
    hit = posb == slot
    gates = aff_ref[0]
    acc = None
    for e in range(ne):
        oh = jnp.where(hit[:, e * cap:(e + 1) * cap], 1.0, 0.0).astype(BF16)
        part = jnp.dot(oh, y_ref[0, e], preferred_element_type=F32) * gates[:, e:e + 1]
        acc = part if acc is None else acc + part
    xs = DN_ALPHA * x_ref[0] + g2_ref[0] * acc
    out_ref[0] = _ln(xs) * l2g_ref[...] + l2b_ref[...]


def _combine_call(y, pos_t, aff_t, x1, g2, l2g, l2b, expand, cap):
    b, t, d = x1.shape
    tm = min(t, 512)
    return pl.pallas_call(
        functools.partial(_combine_kernel, cap),
        grid=(b, t // tm),
        in_specs=[pl.BlockSpec((1, N_EXPERTS, cap, d), lambda i, j: (i, 0, 0, 0)),
                  pl.BlockSpec((1, tm, 128), lambda i, j: (i, j, 0)),
                  pl.BlockSpec((1, tm, 128), lambda i, j: (i, j, 0)),
                  pl.BlockSpec((1, tm, d), lambda i, j: (i, j, 0)),
                  pl.BlockSpec((1, 1, d), lambda i, j: (i, 0, 0)),
                  pl.BlockSpec((1, d), lambda i, j: (0, 0)),
                  pl.BlockSpec((1, d), lambda i, j: (0, 0)),
                  pl.BlockSpec(expand.shape, lambda i, j: (0, 0))],
        out_specs=pl.BlockSpec((1, tm, d), lambda i, j: (i, j, 0)),
        out_shape=jax.ShapeDtypeStruct((b, t, d), F32),
        compiler_params=_params(("arbitrary", "arbitrary")),
        name="combine",
    )(y, pos_t, aff_t, x1, g2, l2g, l2b, expand)


def _pad_cols(parts, total):
    rows = parts[0][0].shape[0]
    out = jnp.zeros((rows, total), parts[0][0].dtype)
    for a, off in parts:
        out = lax.dynamic_update_slice(out, a, (0, off))
    return out


def _block_ones(n, width):
    i = jnp.arange(n) // width
    return (i[:, None] == i[None, :]).astype(BF16)


def _layout_params(w_in, rw_mu, rw_w0, rw_w2, rw_a0, rw_a2, rw_g2, rw_k_k, rw_k_a, rw_r_k,
                   rw_gn_w, rw_gn_b, gla_a2, gla_a_b, gla_norm_w, w_out, ln1_g, ln1_b):
    rw_in = 1760
    segs = [(0, 1536, COL_R), (1536, 1664, COL_LORA), (1664, 1760, COL_GD),
            (rw_in, rw_in + 256, COL_Q), (rw_in + 256, rw_in + 512, COL_KG),
            (rw_in + 512, rw_in + 1024, COL_VG), (rw_in + 1024, rw_in + 1536, COL_GG),
            (rw_in + 1536, rw_in + 1568, COL_GA)]
    w_pad = _pad_cols([(w_in[:, a:b], off) for a, b, off in segs], IN_PAD).astype(BF16)
    mu = _pad_cols([(rw_mu[None, a:b], off) for a, b, off in segs[:3]], RW_PAD)
    w_lora = jnp.zeros((128, 2048), F32)
    for i, m in enumerate((rw_w2[0], rw_w2[1], rw_a2[0], rw_a2[1])):
        w_lora = lax.dynamic_update_slice(w_lora, m, (32 * i, 512 * i))
    g2 = jnp.zeros((128, 512), F32).at[:96].set(rw_g2)
    a2 = jnp.zeros((128, 512), F32).at[0:16, 0:256].set(gla_a2[0]).at[16:32, 256:512].set(gla_a2[1])
    row = lambda a: a.reshape(1, -1)
    return dict(
        w_in=w_pad, mu=mu, k_k=row(rw_k_k), k_a=row(rw_k_a), r_k=row(rw_r_k), w0=rw_w0, a0=rw_a0,
        ab=gla_a_b, w_lora=w_lora.astype(BF16), g2=g2.astype(BF16), a2=a2.astype(BF16),
        seg64=_block_ones(512, 64), seg128=_block_ones(512, 128),
        gn_w=row(rw_gn_w), gn_b=row(rw_gn_b), gla_nw=row(gla_norm_w), w_out=w_out.astype(BF16),
        ln1_g=row(ln1_g), ln1_b=row(ln1_b))


def kernel(x, c, ctx, c_ctx, ada_w, ada_b, w_in, rw_mu, rw_w0, rw_w2, rw_a0, rw_a2, rw_g2, rw_k_k, rw_k_a, rw_r_k, rw_gn_w, rw_gn_b, gla_a2, gla_a_b, gla_norm_w, w_out, ln1_g, ln1_b, router_w, ex_gate, ex_up, ex_down, ln2_g, ln2_b):
    assert ada_w.shape[0] == 1, "single-layer block"
    b, t, d = x.shape
    cap = CAPACITY_FACTOR * t // N_EXPERTS
    pp = _layout_params(w_in[0], rw_mu[0], rw_w0[0], rw_w2[0], rw_a0[0], rw_a2[0], rw_g2[0],
                        rw_k_k[0], rw_k_a[0], rw_r_k[0], rw_gn_w[0], rw_gn_b[0], gla_a2[0],
                        gla_a_b[0], gla_norm_w[0], w_out[0], ln1_g[0], ln1_b[0])

    rows = -(-(b + 1) // 8) * 8
    cs = jnp.zeros((rows, d), F32).at[:b].set(c).at[b].set(c_ctx)
    mod = _mod_call(cs, ada_w[0], ada_b[0][None])
    sh1, sc1, g1, sh2, sc2, g2 = [m[:, None, :] for m in jnp.split(mod[:b], 6, axis=-1)]
    sh1c, sc1c = [jnp.broadcast_to(m[None, None, :], (b, 1, d)) for m in jnp.split(mod[b], 6)[:2]]

    pc = _prep_call(_inproj_call(ctx, sh1c, sc1c, pp["w_in"]), False, pp)
    zero_h = jnp.zeros((b, RW_WIDTH // 256, 2, 256, 256), F32)
    zero_s = jnp.zeros((b, 2, GLA_VAL, GLA_KEY), F32)
    _, h_ctx = _wkv_call(pc, zero_h, False)
    _, s_ctx = _gla_call(pc, zero_s, False)

    pz = _prep_call(_inproj_call(x, sh1, sc1, pp["w_in"]), True, pp)
    y, _ = _wkv_call(pz, h_ctx, True)
    o, _ = _gla_call(pz, s_ctx, True)
    x1 = _post_call(y, pz["bonus"], pz["gout"], o, pz["gate"], x, g1, pp)

    router_pad = jnp.zeros((d, 128), F32).at[:, :N_EXPERTS].set(router_w[0])
    xg, aff_t, pos_t = _route_call(x1, sh2, sc2, router_pad, cap)
    ye = _ffn_call(xg, ex_gate[0].astype(BF16), ex_up[0].astype(BF16), ex_down[0].astype(BF16))
    eidx = jnp.arange(N_EXPERTS * cap) // cap
    expand = (jnp.arange(128)[:, None] == eidx[None, :]).astype(BF16)
    return _combine_call(ye, pos_t, aff_t, x1, g2, ln2_g[0][None], ln2_b[0][None], expand, cap)
```

```python
import functools

import jax
import jax.numpy as jnp
from jax import lax
from jax.experimental import pallas as pl
from jax.experimental.pallas import tpu as pltpu

F32 = jnp.float32
BF16 = jnp.bfloat16
HIGHEST = lax.Precision.HIGHEST

D_MODEL = 1024
GRID_W = 64
RW_WIDTH = 512
RW_HEAD = 64
RW_GN_EPS = 64e-5
GLA_KEY = 256
GLA_VAL = 512
GLA_DV = 128
GLA_GATE_NORM = 16.0
GLA_NORM_EPS = 1e-5
N_EXPERTS = 16
CAPACITY_FACTOR = 2
DN_ALPHA = 2.0 ** 0.25
LN_EPS = 1e-5
CHUNK = 64

COL_R, COL_K, COL_V, COL_LORA, COL_GD = 0, 512, 1024, 1536, 1664
RW_PAD = 1792
COL_Q, COL_KG, COL_VG, COL_GG, COL_GA = 1792, 2048, 2304, 2816, 3328
IN_PAD = 3456

VMEM_LIMIT = 56 * 1024 * 1024


def _params(sem):
    return pltpu.CompilerParams(dimension_semantics=sem, vmem_limit_bytes=VMEM_LIMIT)


def _dot(a, b):
    return jnp.dot(a.astype(BF16), b.astype(BF16), preferred_element_type=F32)


def _dot_nt(a, b):
    return lax.dot_general(a.astype(BF16), b.astype(BF16), (((1,), (1,)), ((), ())),
                           preferred_element_type=F32)


def _dot_tn(a, b):
    return lax.dot_general(a.astype(BF16), b.astype(BF16), (((0,), (0,)), ((), ())),
                           preferred_element_type=F32)


def _dot_split(x, w):
    hi = x.astype(BF16)
    lo = (x - hi.astype(F32)).astype(BF16)
    return (jnp.dot(hi, w, preferred_element_type=F32) + jnp.dot(lo, w, preferred_element_type=F32))


def _sigmoid(x):
    return 1.0 / (1.0 + jnp.exp(-x))


def _softplus(x):
    return jnp.maximum(x, 0.0) + jnp.log(1.0 + jnp.exp(-jnp.abs(x)))


def _ln(x):
    mu = jnp.mean(x, axis=-1, keepdims=True)
    xc = x - mu
    var = jnp.mean(xc * xc, axis=-1, keepdims=True)
    return xc * lax.rsqrt(var + LN_EPS)


def _iota(shape, dim):
    return lax.broadcasted_iota(jnp.int32, shape, dim)


def _mod_kernel(c_ref, w_ref, b_ref, o_ref):
    c = c_ref[...]
    s = c * _sigmoid(c)
    o_ref[...] = jnp.dot(s, w_ref[...], precision=HIGHEST, preferred_element_type=F32) + b_ref[...]


def _mod_call(cs, ada_w, ada_b):
    rows, d = cs.shape
    n = ada_w.shape[1]
    bn = 1536
    return pl.pallas_call(
        _mod_kernel,
        grid=(n // bn,),
        in_specs=[pl.BlockSpec((rows, d), lambda j: (0, 0)),
                  pl.BlockSpec((d, bn), lambda j: (0, j)),
                  pl.BlockSpec((1, bn), lambda j: (0, j))],
        out_specs=pl.BlockSpec((rows, bn), lambda j: (0, j)),
        out_shape=jax.ShapeDtypeStruct((rows, n), F32),
        compiler_params=_params(("arbitrary",)),
        name="mod",
    )(cs, ada_w, ada_b)


def _inproj_kernel(x_ref, sh_ref, sc_ref, w_ref, o_ref):
    h = _ln(x_ref[0]) * (1.0 + sc_ref[0]) + sh_ref[0]
    o_ref[0] = jnp.dot(h.astype(BF16), w_ref[...], preferred_element_type=F32)


def _inproj_call(x, sh, sc, w):
    b, t, d = x.shape
    n = w.shape[1]
    tm = min(t, 512)
    return pl.pallas_call(
        _inproj_kernel,
        grid=(b, t // tm),
        in_specs=[pl.BlockSpec((1, tm, d), lambda i, j: (i, j, 0)),
                  pl.BlockSpec((1, 1, d), lambda i, j: (i, 0, 0)),
                  pl.BlockSpec((1, 1, d), lambda i, j: (i, 0, 0)),
                  pl.BlockSpec((d, n), lambda i, j: (0, 0))],
        out_specs=pl.BlockSpec((1, tm, n), lambda i, j: (i, j, 0)),
        out_shape=jax.ShapeDtypeStruct((b, t, n), F32),
        compiler_params=_params(("arbitrary", "arbitrary")),
        name="inproj",
    )(x, sh, sc, w)


_PREP_OUT = (
    ("r", 512, BF16), ("kk", 512, BF16), ("v", 512, BF16),
    ("kh_f", 512, BF16), ("kh_b", 512, BF16), ("kka_f", 512, BF16), ("kka_b", 512, BF16),
    ("lw_f", 512, F32), ("lw_b", 512, F32), ("gout", 512, F32), ("bonus", 512, F32),
    ("q", 256, BF16), ("kg", 256, BF16), ("vg", 512, BF16),
    ("la_f", 256, F32), ("la_b", 256, F32), ("gate", 512, F32),
)


def _prep_kernel(grid_shift, t_total, tr, *refs):
    if grid_shift:
        z_ref, zp_ref, zn_ref = refs[:3]
        refs = refs[3:]
    else:
        z_ref = refs[0]
        refs = refs[1:]
    (mu_ref, kk_p, ka_p, rk_p, w0_ref, a0_ref, ab_ref, wl_ref, g2_ref, a2_ref, seg_ref) = refs[:11]
    outs = dict(zip([n for n, _, _ in _PREP_OUT], refs[11:11 + len(_PREP_OUT)]))
    zs = refs[11 + len(_PREP_OUT)]

    i = pl.program_id(1)
    zc = z_ref[0, :, 0:RW_PAD]
    zs[0:8, :] = jnp.zeros((8, RW_PAD), F32)
    zs[8 + tr:16 + tr, :] = jnp.zeros((8, RW_PAD), F32)
    zs[8:8 + tr, :] = zc
    left = zs[7:7 + tr, :]
    right = zs[9:9 + tr, :]
    row = _iota((tr, 1), 0) + i * tr
    if grid_shift:
        col = row & (GRID_W - 1)
        left = jnp.where(col > 0, left, 0.0)
        right = jnp.where(col < GRID_W - 1, right, 0.0)
        up = jnp.concatenate([zp_ref[0], zc[:tr - GRID_W]], axis=0)
        up = jnp.where(row >= GRID_W, up, 0.0)
        down = jnp.concatenate([zc[GRID_W:], zn_ref[0]], axis=0)
        down = jnp.where(row < t_total - GRID_W, down, 0.0)
        nb = 0.25 * (up + down + left + right)
    else:
        left = jnp.where(row > 0, left, 0.0)
        right = jnp.where(row < t_total - 1, right, 0.0)
        nb = 0.5 * (left + right)
    zr = zc + (nb - zc) * mu_ref[...]

    r = zr[:, COL_R:COL_R + 512]
    k = zr[:, COL_K:COL_K + 512]
    v = zr[:, COL_V:COL_V + 512]
    lo = zr[:, COL_LORA:COL_LORA + 128]
    gd = zr[:, COL_GD:COL_GD + 128]
    seg = seg_ref[...]

    lane = _iota((1, 128), 1)
    lor = _dot(jnp.where(lane < 64, jnp.tanh(lo), lo), wl_ref[...])
    kkr = k * kk_p[...]
    kk = kkr * lax.rsqrt(_dot_split(kkr * kkr, seg) + 1e-12)
    outs["r"][0] = r.astype(BF16)
    outs["kk"][0] = kk.astype(BF16)
    outs["v"][0] = v.astype(BF16)
    outs["gout"][0] = _dot(_sigmoid(gd), g2_ref[...])
    rrk = r * rk_p[...]
    bon = None
    for d, sfx in ((0, "_f"), (1, "_b")):
        w_log = -_softplus(-(lor[:, d * 512:(d + 1) * 512] + w0_ref[d:d + 1, :])) - 0.5
        outs["lw" + sfx][0] = -jnp.exp(w_log)
        a = _sigmoid(a0_ref[d:d + 1, :] + lor[:, 1024 + d * 512:1536 + d * 512])
        kh = k * (1.0 + (a - 1.0) * ka_p[...])
        outs["kh" + sfx][0] = kh.astype(BF16)
        outs["kka" + sfx][0] = (kk * a).astype(BF16)
        s = _dot_split(rrk * kh, seg)
        bon = s if bon is None else bon + s
    outs["bonus"][0] = bon * v

    outs["q"][0] = (z_ref[0, :, COL_Q:COL_Q + 256] * (RW_HEAD ** -0.5)).astype(BF16)
    outs["kg"][0] = z_ref[0, :, COL_KG:COL_KG + 256].astype(BF16)
    outs["vg"][0] = z_ref[0, :, COL_VG:COL_VG + 512].astype(BF16)
    gg = z_ref[0, :, COL_GG:COL_GG + 512]
    outs["gate"][0] = gg * _sigmoid(gg)
    gl = _dot(z_ref[0, :, COL_GA:COL_GA + 128], a2_ref[...])
    for d, sfx in ((0, "_f"), (1, "_b")):
        xg = gl[:, d * 256:(d + 1) * 256] + ab_ref[d:d + 1, :]
        outs["la" + sfx][0] = -_softplus(-xg) * (1.0 / GLA_GATE_NORM)


def _prep_call(z, grid_shift, pp):
    b, t, n = z.shape
    tr = min(t, 512)
    nt = t // tr
    hb = tr // GRID_W
    nh = t // GRID_W
    full = lambda shape: pl.BlockSpec(shape, lambda i, j: (0,) * len(shape))
    in_specs = [pl.BlockSpec((1, tr, n), lambda i, j: (i, j, 0))]
    args = [z]
    if grid_shift:
        in_specs += [
            pl.BlockSpec((1, GRID_W, RW_PAD), lambda i, j: (i, jnp.maximum(j * hb - 1, 0), 0)),
            pl.BlockSpec((1, GRID_W, RW_PAD), lambda i, j: (i, jnp.minimum((j + 1) * hb, nh - 1), 0)),
        ]
        args += [z, z]
    names = ("mu", "k_k", "k_a", "r_k", "w0", "a0", "ab", "w_lora", "g2", "a2", "seg64")
    for nm in names:
        in_specs.append(full(pp[nm].shape))
        args.append(pp[nm])
    out_specs = [pl.BlockSpec((1, tr, w), lambda i, j: (i, j, 0)) for _, w, _ in _PREP_OUT]
    out_shape = [jax.ShapeDtypeStruct((b, t, w), dt) for _, w, dt in _PREP_OUT]
    res = pl.pallas_call(
        functools.partial(_prep_kernel, grid_shift, t, tr),
        grid=(b, nt),
        in_specs=in_specs,
        out_specs=out_specs,
        out_shape=out_shape,
        scratch_shapes=[pltpu.VMEM((tr + 16, RW_PAD), F32)],
        compiler_params=_params(("arbitrary", "arbitrary")),
        name="prep_grid" if grid_shift else "prep_seq",
    )(*args)
    return dict(zip([nm for nm, _, _ in _PREP_OUT], res))


def _bd(x, blk):
    reps = blk.shape[0] // x.shape[0]
    return jnp.where(blk, jnp.concatenate([x] * reps, axis=0), jnp.zeros((), x.dtype))


def _scan_masks(rev, width):
    c = CHUNK
    ri = _iota((c, width), 0)
    jj = _iota((c, width), 1) & (c - 1)
    strict = (jj > ri) if rev else (jj < ri)
    incl = (jj >= ri) if rev else (jj <= ri)
    eye = jj == ri
    a = _iota((c, c), 0)
    bq = _iota((c, c), 1)
    mcum = jnp.where((bq >= a) if rev else (bq <= a), 1.0, 0.0).astype(BF16)
    levels = [strict & ((jj >> (l + 1)) == (ri >> (l + 1))) & ((jj >> l) != (ri >> l)) for l in range(6)]
    return strict, incl, eye, mcum, levels


def _each(f, *lists):
    return [f(*a) for a in zip(*lists)]


def _cumsum(x, mcum):
    hi = x.astype(BF16)
    lo = (x - hi.astype(F32)).astype(BF16)
    return (jnp.dot(mcum, hi, preferred_element_type=F32) + jnp.dot(mcum, lo, preferred_element_type=F32))


def _wkv_chains(need_y, ins, hts, masks, blk):
    c = CHUNK
    revs = [i[0] for i in ins]
    mk = [masks[int(rv)] for rv in revs]
    g = _each(lambda i, m: _cumsum(i[6], m[3]), ins, mk)
    gl = [gg[0:1] if rv else gg[c - 1:c] for gg, rv in zip(g, revs)]
    x2 = _each(lambda i, gg: jnp.concatenate([(i[2] * jnp.exp(gg - i[6])).astype(BF16),
                                              (i[1] * jnp.exp(gg)).astype(BF16)], axis=0), ins, g)
    at = _each(lambda i, gg: (i[3] * jnp.exp(-gg)).astype(BF16), ins, g)
    kt = _each(lambda i, gg: (i[4] * jnp.exp(-gg)).astype(BF16), ins, g)
    sa = _each(lambda x, a: _dot_nt(x, _bd(a, blk)), x2, at)
    sk = _each(lambda x, k: _dot_nt(x, _bd(k, blk)), x2, kt)
    tm = _each(lambda s, m: jnp.where(m[2], 1.0, 0.0) - jnp.where(m[4][0], s[:c], 0.0), sa, mk)
    for l in range(1, 6):
        x = _each(lambda t, s, m: _dot(t, _bd(jnp.where(m[4][l], s[:c], 0.0).astype(BF16), blk)), tm, sa, mk)
        tm = _each(lambda t, xx: t - _dot(xx, _bd(t.astype(BF16), blk)), tm, x)
    xh = _each(_dot_nt, x2, hts)
    vb = [i[5].astype(BF16) for i in ins]
    vbd = [_bd(v, blk) for v in vb]
    lkv = _each(lambda s, m, vd: _dot(jnp.where(m[0], s[:c], 0.0), vd), sk, mk, vbd)
    ub = _each(lambda t, h, lv: (-_dot(t, _bd((h[:c] + lv).astype(BF16), blk))).astype(BF16), tm, xh, lkv)
    upd = _each(lambda u, v, i, gg, gle: _dot_tn(
        jnp.concatenate([u, v], axis=0),
        jnp.concatenate([(i[3] * jnp.exp(gle - gg)).astype(BF16), (i[4] * jnp.exp(gle - gg)).astype(BF16)], axis=0)),
        ub, vb, ins, g, gl)
    ht_new = _each(lambda h, gle, up: h * jnp.exp(gle) + jnp.where(blk, up, 0.0), hts, gl, upd)
    ys = [None] * len(ins)
    if need_y:
        ya = _each(lambda s, m, u: _dot(jnp.where(m[1], s[c:], 0.0), _bd(u, blk)), sa, mk, ub)
        yk = _each(lambda s, m, vd: _dot(jnp.where(m[1], s[c:], 0.0), vd), sk, mk, vbd)
        ys = _each(lambda h, a, k: h[c:] + a + k, xh, ya, yk)
    return ys, ht_new


SCAN_TB = 256
WKV_NB = 2
GLA_NB = 4


def _wkv_kernel(need_y, nb, tb, *refs):
    fwd, bwd, h0_ref, rest = refs[:6], refs[6:12], refs[12], refs[13:]
    if need_y:
        yf_ref, yb_ref, hout_ref, hs = rest
    else:
        hout_ref, hs = rest
    c = CHUNK
    nc = tb // c
    w = 256
    j = pl.program_id(1)
    blk = (_iota((w, w), 0) >> 6) == (_iota((w, w), 1) >> 6)
    masks = (_scan_masks(False, w), _scan_masks(True, w))

    @pl.when(j == 0)
    def _():
        hs[...] = h0_ref[...]

    def body(ci, carry):
        where, ins, hts = [], [], []
        for ib in range(nb):
            for g in range(RW_WIDTH // w):
                lanes = slice(g * w, (g + 1) * w)
                for d, views in enumerate((fwd, bwd)):
                    cc = ci if d == 0 else nc - 1 - ci
                    rows = pl.ds(pl.multiple_of(cc * c, c), c)
                    r, kk, v, kh, kka = [ref[ib, rows, lanes].astype(F32) for ref in views[:5]]
                    ins.append((d == 1, r, kk, kka, kh, v, views[5][ib, rows, lanes]))
                    hts.append(hs[ib, g, d])
                    where.append((ib, g, d, rows, lanes))
        ys, hts = _wkv_chains(need_y, ins, hts, masks, blk)
        for (ib, g, d, rows, lanes), y, ht in zip(where, ys, hts):
            hs[ib, g, d] = ht
            if need_y:
                y_ref = yb_ref if d else yf_ref
                y_ref[ib, rows, lanes] = y
        return carry

    lax.fori_loop(0, nc, body, 0)

    @pl.when(j == pl.num_programs(1) - 1)
    def _():
        hout_ref[...] = hs[...]


def _wkv_call(p, h0, need_y):
    b, t, _ = p["r"].shape
    nb = min(WKV_NB, b)
    tb = min(SCAN_TB, t)
    nblk = t // tb
    ng = RW_WIDTH // 256
    fwd = pl.BlockSpec((nb, tb, RW_WIDTH), lambda i, j: (i, j, 0))
    bwd = pl.BlockSpec((nb, tb, RW_WIDTH), lambda i, j: (i, nblk - 1 - j, 0))
    st = pl.BlockSpec((nb, ng, 2, 256, 256), lambda i, j: (i, 0, 0, 0, 0))
    out_specs = [st]
    out_shape = [jax.ShapeDtypeStruct((b, ng, 2, 256, 256), F32)]
    if need_y:
        out_specs = [fwd, bwd] + out_specs
        out_shape = [jax.ShapeDtypeStruct((b, t, RW_WIDTH), F32)] * 2 + out_shape
    res = pl.pallas_call(
        functools.partial(_wkv_kernel, need_y, nb, tb),
        grid=(b // nb, nblk),
        in_specs=[fwd] * 6 + [bwd] * 6 + [st],
        out_specs=out_specs,
        out_shape=out_shape,
        scratch_shapes=[pltpu.VMEM((nb, ng, 2, 256, 256), F32)],
        compiler_params=_params(("arbitrary", "arbitrary")),
        name="wkv_y" if need_y else "wkv_state",
    )(p["r"], p["kk"], p["v"], p["kh_f"], p["kka_f"], p["lw_f"],
      p["r"], p["kk"], p["v"], p["kh_b"], p["kka_b"], p["lw_b"], h0)
    return res if need_y else (None, None, res[0])


def _gla_chains(need_o, ins, sts, masks, blk_k, blk_v, blk_s):
    c = CHUNK
    revs = [i[0] for i in ins]
    mk = [masks[int(rv)] for rv in revs]
    b = _each(lambda i, m: _cumsum(i[4], m[3]), ins, mk)
    bl = [bb[0:1] if rv else bb[c - 1:c] for bb, rv in zip(b, revs)]
    vb = [i[3].astype(BF16) for i in ins]
    upd = _each(lambda v, i, bb, ble: _dot_tn(v, (i[2] * jnp.exp(ble - bb)).astype(BF16)), vb, ins, b, bl)
    st_new = _each(lambda s, ble, up: s * jnp.exp(ble) + jnp.where(blk_s, up, 0.0), sts, bl, upd)
    os_ = [None] * len(ins)
    if need_o:
        q_in = _each(lambda i, bb: (i[1] * jnp.exp(bb)).astype(BF16), ins, b)
        sc = _each(lambda q, i, bb, m: jnp.where(
            m[1], _dot_nt(q, _bd((i[2] * jnp.exp(-bb)).astype(BF16), blk_k)), 0.0), q_in, ins, b, mk)
        oi = _each(lambda s, v: _dot(s, _bd(v, blk_v)), sc, vb)
        ox = _each(_dot_nt, q_in, sts)
        os_ = _each(lambda a, x: a + x, oi, ox)
    return os_, st_new


def _gla_kernel(need_o, nb, tb, *refs):
    fwd, bwd, s0_ref, rest = refs[:4], refs[4:8], refs[8], refs[9:]
    if need_o:
        of_ref, ob_ref, sout_ref, ss = rest
    else:
        sout_ref, ss = rest
    c = CHUNK
    nc = tb // c
    j = pl.program_id(1)
    blk_k = (_iota((256, 256), 0) >> 6) == (_iota((256, 256), 1) >> 6)
    blk_v = (_iota((256, 512), 0) >> 6) == (_iota((256, 512), 1) >> 7)
    blk_s = (_iota((512, 256), 0) >> 7) == (_iota((512, 256), 1) >> 6)
    masks = (_scan_masks(False, 256), _scan_masks(True, 256))

    @pl.when(j == 0)
    def _():
        ss[...] = s0_ref[...]

    def body(ci, carry):
        where, ins, sts = [], [], []
        for ib in range(nb):
            for d, views in enumerate((fwd, bwd)):
                cc = ci if d == 0 else nc - 1 - ci
                rows = pl.ds(pl.multiple_of(cc * c, c), c)
                q, k, v = [ref[ib, rows, :].astype(F32) for ref in views[:3]]
                ins.append((d == 1, q, k, v, views[3][ib, rows, :]))
                sts.append(ss[ib, d])
                where.append((ib, d, rows))
        os_, sts = _gla_chains(need_o, ins, sts, masks, blk_k, blk_v, blk_s)
        for (ib, d, rows), o, st in zip(where, os_, sts):
            ss[ib, d] = st
            if need_o:
                o_ref = ob_ref if d else of_ref
                o_ref[ib, rows, :] = o
        return carry

    lax.fori_loop(0, nc, body, 0)

    @pl.when(j == pl.num_programs(1) - 1)
    def _():
        sout_ref[...] = ss[...]


def _gla_call(p, s0, need_o):
    b, t, _ = p["q"].shape
    nb = min(GLA_NB, b)
    tb = min(SCAN_TB, t)
    nblk = t // tb
    fwd = lambda w: pl.BlockSpec((nb, tb, w), lambda i, j: (i, j, 0))
    bwd = lambda w: pl.BlockSpec((nb, tb, w), lambda i, j: (i, nblk - 1 - j, 0))
    st = pl.BlockSpec((nb, 2, GLA_VAL, GLA_KEY), lambda i, j: (i, 0, 0, 0))
    out_specs = [st]
    out_shape = [jax.ShapeDtypeStruct((b, 2, GLA_VAL, GLA_KEY), F32)]
    if need_o:
        out_specs = [fwd(GLA_VAL), bwd(GLA_VAL)] + out_specs
        out_shape = [jax.ShapeDtypeStruct((b, t, GLA_VAL), F32)] * 2 + out_shape
    res = pl.pallas_call(
        functools.partial(_gla_kernel, need_o, nb, tb),
        grid=(b // nb, nblk),
        in_specs=[fwd(256), fwd(256), fwd(512), fwd(256), bwd(256), bwd(256), bwd(512), bwd(256), st],
        out_specs=out_specs,
        out_shape=out_shape,
        scratch_shapes=[pltpu.VMEM((nb, 2, GLA_VAL, GLA_KEY), F32)],
        compiler_params=_params(("arbitrary", "arbitrary")),
        name="gla_o" if need_o else "gla_state",
    )(p["q"], p["kg"], p["vg"], p["la_f"], p["q"], p["kg"], p["vg"], p["la_b"], s0)
    return res if need_o else (None, None, res[0])


def _post_kernel(yf_ref, yb_ref, bon_ref, gout_ref, of_ref, ob_ref, gate_ref, x_ref, g1_ref, gnw_ref,
                 gnb_ref, nw_ref, wout_ref, l1g_ref, l1b_ref, seg64_ref, seg128_ref, out_ref):
    y = yf_ref[0] + yb_ref[0]
    seg64 = seg64_ref[...]
    mu = _dot_split(y, seg64) * (1.0 / RW_HEAD)
    yc = y - mu
    var = _dot_split(yc * yc, seg64) * (1.0 / RW_HEAD)
    y_n = yc * lax.rsqrt(var + RW_GN_EPS) * gnw_ref[...] + gnb_ref[...]
    rw_out = (y_n + bon_ref[0]) * gout_ref[0]
    o = of_ref[0] + ob_ref[0]
    ms = _dot_split(o * o, seg128_ref[...]) * (1.0 / GLA_DV)
    gla_out = o * lax.rsqrt(ms + GLA_NORM_EPS) * nw_ref[...] * gate_ref[0]
    mix = jnp.concatenate([rw_out.astype(BF16), gla_out.astype(BF16)], axis=-1)
    proj = jnp.dot(mix, wout_ref[...], preferred_element_type=F32)
    xs = DN_ALPHA * x_ref[0] + g1_ref[0] * proj
    out_ref[0] = _ln(xs) * l1g_ref[...] + l1b_ref[...]


def _post_call(yf, yb, bonus, gout, of, ob, gate, x, g1, pp):
    b, t, d = x.shape
    tm = min(t, 512)
    tile = lambda w: pl.BlockSpec((1, tm, w), lambda i, j: (i, j, 0))
    full = lambda a: pl.BlockSpec(a.shape, lambda i, j: (0,) * a.ndim)
    consts = [pp["gn_w"], pp["gn_b"], pp["gla_nw"], pp["w_out"], pp["ln1_g"], pp["ln1_b"],
              pp["seg64"], pp["seg128"]]
    return pl.pallas_call(
        _post_kernel,
        grid=(b, t // tm),
        in_specs=[tile(512)] * 7 + [tile(d), pl.BlockSpec((1, 1, d), lambda i, j: (i, 0, 0))]
        + [full(a) for a in consts],
        out_specs=tile(d),
        out_shape=jax.ShapeDtypeStruct((b, t, d), F32),
        compiler_params=_params(("arbitrary", "arbitrary")),
        name="post",
    )(yf, yb, bonus, gout, of, ob, gate, x, g1, *consts)


def _prefix_excl(m, su):
    e, t = m.shape
    off = jnp.zeros((e, 1), F32)
    parts = []
    for j in range(t // 128):
        blk = m[:, j * 128:(j + 1) * 128]
        parts.append(_dot(blk, su) + off)
        off = off + jnp.sum(blk, axis=1, keepdims=True)
    return jnp.concatenate(parts, axis=1)


def _route_kernel(cap, x_ref, sh_ref, sc_ref, rw_ref, xg_ref, aff_ref, pos_ref, h2_s, pe_s):
    e = pl.program_id(1)
    t = x_ref.shape[1]

    @pl.when(e == 0)
    def _():
        h2 = _ln(x_ref[0]) * (1.0 + sc_ref[0]) + sh_ref[0]
        h2_s[...] = h2.astype(BF16)
        logits = jnp.dot(h2, rw_ref[...], precision=HIGHEST, preferred_element_type=F32)
        lane = _iota((1, 128), 1)
        logits = jnp.where(lane < N_EXPERTS, logits, -1e30)
        m = jnp.max(logits, axis=-1, keepdims=True)
        ex = jnp.exp(logits - m)
        aff = ex / jnp.sum(ex, axis=-1, keepdims=True)
        aff_ref[0] = aff
        aff_t = aff.T[0:N_EXPERTS, :]
        def bs(_, lohi):
            lo, hi = lohi
            mid = lo + ((hi - lo) >> 1)
            mid_f = pltpu.bitcast(mid, F32)[:, 0:1]
            cnt = jnp.sum(jnp.where(aff_t >= mid_f, 1.0, 0.0), axis=1, keepdims=True)
            ok = cnt >= cap
            return jnp.where(ok, mid, lo), jnp.where(ok, hi, mid)

        lo0 = jnp.zeros((N_EXPERTS, 128), jnp.int32)
        hi0 = jnp.full((N_EXPERTS, 128), 0x3F800001, jnp.int32)
        thr_bits, _ = lax.fori_loop(0, 31, bs, (lo0, hi0))
        thr = pltpu.bitcast(thr_bits, F32)[:, 0:1]
        gt = jnp.where(aff_t > thr, 1.0, 0.0)
        eq = jnp.where(aff_t == thr, 1.0, 0.0)
        need = cap - jnp.sum(gt, axis=1, keepdims=True)
        su = jnp.where(_iota((128, 128), 0) < _iota((128, 128), 1), 1.0, 0.0).astype(BF16)
        sel = gt + eq * jnp.where(_prefix_excl(eq, su) < need, 1.0, 0.0)
        pos = jnp.where(sel > 0.0, _prefix_excl(sel, su), -1.0)
        pe_s[0:N_EXPERTS, :] = pos
        pe_s[N_EXPERTS:, :] = jnp.full((128 - N_EXPERTS, t), -1.0, F32)
        pos_ref[0] = pe_s[...].T

    prow = pe_s[pl.ds(e, 1), :]
    slot = _iota((cap, t), 0).astype(F32)
    onehot = jnp.where(prow == slot, 1.0, 0.0).astype(BF16)
    xg_ref[0, 0] = jnp.dot(onehot, h2_s[...], preferred_element_type=F32).astype(BF16)


def _route_call(x1, sh2, sc2, router_pad, cap):
    b, t, d = x1.shape
    return pl.pallas_call(
        functools.partial(_route_kernel, cap),
        grid=(b, N_EXPERTS),
        in_specs=[pl.BlockSpec((1, t, d), lambda i, e: (i, 0, 0)),
                  pl.BlockSpec((1, 1, d), lambda i, e: (i, 0, 0)),
                  pl.BlockSpec((1, 1, d), lambda i, e: (i, 0, 0)),
                  pl.BlockSpec((d, 128), lambda i, e: (0, 0))],
        out_specs=[pl.BlockSpec((1, 1, cap, d), lambda i, e: (i, e, 0, 0)),
                   pl.BlockSpec((1, t, 128), lambda i, e: (i, 0, 0)),
                   pl.BlockSpec((1, t, 128), lambda i, e: (i, 0, 0))],
        out_shape=[jax.ShapeDtypeStruct((b, N_EXPERTS, cap, d), BF16),
                   jax.ShapeDtypeStruct((b, t, 128), F32),
                   jax.ShapeDtypeStruct((b, t, 128), F32)],
        scratch_shapes=[pltpu.VMEM((t, d), BF16), pltpu.VMEM((128, t), F32)],
        compiler_params=_params(("arbitrary", "arbitrary")),
        name="route",
    )(x1, sh2, sc2, router_pad)


def _ffn_kernel(x_ref, wg_ref, wu_ref, wd_ref, y_ref):
    x = x_ref[0, 0]
    g = jnp.dot(x, wg_ref[0], preferred_element_type=F32)
    u = jnp.dot(x, wu_ref[0], preferred_element_type=F32)
    h = (g * _sigmoid(g) * u).astype(BF16)
    y_ref[0, 0] = jnp.dot(h, wd_ref[0], preferred_element_type=F32).astype(BF16)


def _ffn_call(xg, wg, wu, wd):
    b, ne, cap, d = xg.shape
    f = wg.shape[2]
    return pl.pallas_call(
        _ffn_kernel,
        grid=(ne, b),
        in_specs=[pl.BlockSpec((1, 1, cap, d), lambda e, i: (i, e, 0, 0)),
                  pl.BlockSpec((1, d, f), lambda e, i: (e, 0, 0)),
                  pl.BlockSpec((1, d, f), lambda e, i: (e, 0, 0)),
                  pl.BlockSpec((1, f, d), lambda e, i: (e, 0, 0))],
        out_specs=pl.BlockSpec((1, 1, cap, d), lambda e, i: (i, e, 0, 0)),
        out_shape=jax.ShapeDtypeStruct((b, ne, cap, d), BF16),
        compiler_params=_params(("arbitrary", "arbitrary")),
        name="ffn",
    )(xg, wg, wu, wd)


def _combine_kernel(cap, y_ref, pos_ref, aff_ref, x_ref, g2_ref, l2g_ref, l2b_ref, ex_ref, out_ref):
    tm = x_ref.shape[1]
    ne = N_EXPERTS
    posb = _dot(pos_ref[0], ex_ref[...])
    slot = (_iota((1, ne * cap), 1) & (cap - 1)).astype(F32)
    hit = posb == slot
    gates = aff_ref[0]
    acc = None
    for e in range(ne):
        oh = jnp.where(hit[:, e * cap:(e + 1) * cap], 1.0, 0.0).astype(BF16)
        part = jnp.dot(oh, y_ref[0, e], preferred_element_type=F32) * gates[:, e:e + 1]
        acc = part if acc is None else acc + part
    xs = DN_ALPHA * x_ref[0] + g2_ref[0] * acc
    out_ref[0] = _ln(xs) * l2g_ref[...] + l2b_ref[...]


def _combine_call(y, pos_t, aff_t, x1, g2, l2g, l2b, expand, cap):
    b, t, d = x1.shape
    tm = min(t, 512)
    return pl.pallas_call(
        functools.partial(_combine_kernel, cap),
        grid=(b, t // tm),
        in_specs=[pl.BlockSpec((1, N_EXPERTS, cap, d), lambda i, j: (i, 0, 0, 0)),
                  pl.BlockSpec((1, tm, 128), lambda i, j: (i, j, 0)),
                  pl.BlockSpec((1, tm, 128), lambda i, j: (i, j, 0)),
                  pl.BlockSpec((1, tm, d), lambda i, j: (i, j, 0)),
                  pl.BlockSpec((1, 1, d), lambda i, j: (i, 0, 0)),
                  pl.BlockSpec((1, d), lambda i, j: (0, 0)),
                  pl.BlockSpec((1, d), lambda i, j: (0, 0)),
                  pl.BlockSpec(expand.shape, lambda i, j: (0, 0))],
        out_specs=pl.BlockSpec((1, tm, d), lambda i, j: (i, j, 0)),
        out_shape=jax.ShapeDtypeStruct((b, t, d), F32),
        compiler_params=_params(("arbitrary", "arbitrary")),
        name="combine",
    )(y, pos_t, aff_t, x1, g2, l2g, l2b, expand)


def _pad_cols(parts, total):
    rows = parts[0][0].shape[0]
    out = jnp.zeros((rows, total), parts[0][0].dtype)
    for a, off in parts:
        out = lax.dynamic_update_slice(out, a, (0, off))
    return out


def _block_ones(n, width):
    i = jnp.arange(n) // width
    return (i[:, None] == i[None, :]).astype(BF16)


def _layout_params(w_in, rw_mu, rw_w0, rw_w2, rw_a0, rw_a2, rw_g2, rw_k_k, rw_k_a, rw_r_k,
                   rw_gn_w, rw_gn_b, gla_a2, gla_a_b, gla_norm_w, w_out, ln1_g, ln1_b):
    rw_in = 1760
    segs = [(0, 1536, COL_R), (1536, 1664, COL_LORA), (1664, 1760, COL_GD),
            (rw_in, rw_in + 256, COL_Q), (rw_in + 256, rw_in + 512, COL_KG),
            (rw_in + 512, rw_in + 1024, COL_VG), (rw_in + 1024, rw_in + 1536, COL_GG),
            (rw_in + 1536, rw_in + 1568, COL_GA)]
    w_pad = _pad_cols([(w_in[:, a:b], off) for a, b, off in segs], IN_PAD).astype(BF16)
    mu = _pad_cols([(rw_mu[None, a:b], off) for a, b, off in segs[:3]], RW_PAD)
    w_lora = jnp.zeros((128, 2048), F32)
    for i, m in enumerate((rw_w2[0], rw_w2[1], rw_a2[0], rw_a2[1])):
        w_lora = lax.dynamic_update_slice(w_lora, m, (32 * i, 512 * i))
    g2 = jnp.zeros((128, 512), F32).at[:96].set(rw_g2)
    a2 = jnp.zeros((128, 512), F32).at[0:16, 0:256].set(gla_a2[0]).at[16:32, 256:512].set(gla_a2[1])
    row = lambda a: a.reshape(1, -1)
    return dict(
        w_in=w_pad, mu=mu, k_k=row(rw_k_k), k_a=row(rw_k_a), r_k=row(rw_r_k), w0=rw_w0, a0=rw_a0,
        ab=gla_a_b, w_lora=w_lora.astype(BF16), g2=g2.astype(BF16), a2=a2.astype(BF16),
        seg64=_block_ones(512, 64), seg128=_block_ones(512, 128),
        gn_w=row(rw_gn_w), gn_b=row(rw_gn_b), gla_nw=row(gla_norm_w), w_out=w_out.astype(BF16),
        ln1_g=row(ln1_g), ln1_b=row(ln1_b))


def kernel(x, c, ctx, c_ctx, ada_w, ada_b, w_in, rw_mu, rw_w0, rw_w2, rw_a0, rw_a2, rw_g2, rw_k_k, rw_k_a, rw_r_k, rw_gn_w, rw_gn_b, gla_a2, gla_a_b, gla_norm_w, w_out, ln1_g, ln1_b, router_w, ex_gate, ex_up, ex_down, ln2_g, ln2_b):
    assert ada_w.shape[0] == 1, "single-layer block"
    b, t, d = x.shape
    cap = CAPACITY_FACTOR * t // N_EXPERTS
    pp = _layout_params(w_in[0], rw_mu[0], rw_w0[0], rw_w2[0], rw_a0[0], rw_a2[0], rw_g2[0],
                        rw_k_k[0], rw_k_a[0], rw_r_k[0], rw_gn_w[0], rw_gn_b[0], gla_a2[0],
                        gla_a_b[0], gla_norm_w[0], w_out[0], ln1_g[0], ln1_b[0])

    rows = -(-(b + 1) // 8) * 8
    cs = jnp.zeros((rows, d), F32).at[:b].set(c).at[b].set(c_ctx)
    mod = _mod_call(cs, ada_w[0], ada_b[0][None])
    sh1, sc1, g1, sh2, sc2, g2 = [m[:, None, :] for m in jnp.split(mod[:b], 6, axis=-1)]
    sh1c, sc1c = [jnp.broadcast_to(m[None, None, :], (b, 1, d)) for m in jnp.split(mod[b], 6)[:2]]

    pc = _prep_call(_inproj_call(ctx, sh1c, sc1c, pp["w_in"]), False, pp)
    zero_h = jnp.zeros((b, RW_WIDTH // 256, 2, 256, 256), F32)
    zero_s = jnp.zeros((b, 2, GLA_VAL, GLA_KEY), F32)
    _, _, h_ctx = _wkv_call(pc, zero_h, False)
    _, _, s_ctx = _gla_call(pc, zero_s, False)

    pz = _prep_call(_inproj_call(x, sh1, sc1, pp["w_in"]), True, pp)
    yf, yb, _ = _wkv_call(pz, h_ctx, True)
    of, ob, _ = _gla_call(pz, s_ctx, True)
    x1 = _post_call(yf, yb, pz["bonus"], pz["gout"], of, ob, pz["gate"], x, g1, pp)

    router_pad = jnp.zeros((d, 128), F32).at[:, :N_EXPERTS].set(router_w[0])
    xg, aff_t, pos_t = _route_call(x1, sh2, sc2, router_pad, cap)
    ye = _ffn_call(xg, ex_gate[0].astype(BF16), ex_up[0].astype(BF16), ex_down[0].astype(BF16))
    eidx = jnp.arange(N_EXPERTS * cap) // cap
    expand = (jnp.arange(128)[:, None] == eidx[None, :]).astype(BF16)
    return _combine_call(ye, pos_t, aff_t, x1, g2, ln2_g[0][None], ln2_b[0][None], expand, cap)
```

```python
import functools

import jax
import jax.numpy as jnp
from jax import lax
from jax.experimental import pallas as pl
from jax.experimental.pallas import tpu as pltpu

F32 = jnp.float32
BF16 = jnp.bfloat16
HIGHEST = lax.Precision.HIGHEST

D_MODEL = 1024
GRID_W = 64
RW_WIDTH = 512
RW_HEAD = 64
RW_GN_EPS = 64e-5
GLA_KEY = 256
GLA_VAL = 512
GLA_DV = 128
GLA_GATE_NORM = 16.0
GLA_NORM_EPS = 1e-5
N_EXPERTS = 16
CAPACITY_FACTOR = 2
DN_ALPHA = 2.0 ** 0.25
LN_EPS = 1e-5
CHUNK = 64
LOG_DECAY_SCALE = 0.6065306597126334

COL_R, COL_K, COL_V, COL_LORA, COL_GD = 0, 512, 1024, 1536, 1664
RW_PAD = 1792
COL_Q, COL_KG, COL_VG, COL_GG, COL_GA = 1792, 2048, 2304, 2816, 3328
IN_PAD = 3456

VMEM_LIMIT = 56 * 1024 * 1024


def _params(sem):
    return pltpu.CompilerParams(dimension_semantics=sem, vmem_limit_bytes=VMEM_LIMIT)


def _dot(a, b):
    return jnp.dot(a.astype(BF16), b.astype(BF16), preferred_element_type=F32)


def _dot_nt(a, b):
    return lax.dot_general(a.astype(BF16), b.astype(BF16), (((1,), (1,)), ((), ())),
                           preferred_element_type=F32)


def _dot_tn(a, b):
    return lax.dot_general(a.astype(BF16), b.astype(BF16), (((0,), (0,)), ((), ())),
                           preferred_element_type=F32)


def _sigmoid(x):
    return 1.0 / (1.0 + jnp.exp(-x))


def _softplus(x):
    return jnp.maximum(x, 0.0) + jnp.log(1.0 + jnp.exp(-jnp.abs(x)))


def _ln(x):
    mu = jnp.mean(x, axis=-1, keepdims=True)
    xc = x - mu
    var = jnp.mean(xc * xc, axis=-1, keepdims=True)
    return xc * lax.rsqrt(var + LN_EPS)


def _iota(shape, dim):
    return lax.broadcasted_iota(jnp.int32, shape, dim)


def _mod_kernel(c_ref, w_ref, b_ref, o_ref):
    c = c_ref[...]
    s = c * _sigmoid(c)
    o_ref[...] = jnp.dot(s, w_ref[...], precision=HIGHEST, preferred_element_type=F32) + b_ref[...]


def _mod_call(cs, ada_w, ada_b):
    rows, d = cs.shape
    n = ada_w.shape[1]
    bn = 1536
    return pl.pallas_call(
        _mod_kernel,
        grid=(n // bn,),
        in_specs=[pl.BlockSpec((rows, d), lambda j: (0, 0)),
                  pl.BlockSpec((d, bn), lambda j: (0, j)),
                  pl.BlockSpec((1, bn), lambda j: (0, j))],
        out_specs=pl.BlockSpec((rows, bn), lambda j: (0, j)),
        out_shape=jax.ShapeDtypeStruct((rows, n), F32),
        compiler_params=_params(("arbitrary",)),
        name="mod",
    )(cs, ada_w, ada_b)


_FRONT_OUT = (
    ("r", 512, BF16), ("kk", 512, BF16), ("v", 512, BF16),
    ("kh_f", 512, BF16), ("kh_b", 512, BF16), ("kka_f", 512, BF16), ("kka_b", 512, BF16),
    ("lw_f", 512, F32), ("lw_b", 512, F32), ("gout", 512, BF16), ("bonus", 512, BF16),
    ("q", 256, BF16), ("kg", 256, BF16), ("vg", 512, BF16),
    ("la_f", 256, F32), ("la_b", 256, F32), ("gate", 512, BF16),
)
_FRONT_PARAMS = ("w_in", "mu", "k_k", "k_a", "r_k", "w0", "a0", "ab", "w_lora", "g2", "a2", "seg64")


def _front_kernel(grid_shift, t_total, tr, *refs):
    if grid_shift:
        x_ref, xp_ref, xn_ref = refs[:3]
        refs = refs[3:]
    else:
        x_ref = refs[0]
        refs = refs[1:]
    sh_ref, sc_ref = refs[:2]
    (w_ref, mu_ref, kk_p, ka_p, rk_p, w0_ref, a0_ref, ab_ref, wl_ref, g2_ref, a2_ref, seg_ref) = refs[2:14]
    outs = dict(zip([n for n, _, _ in _FRONT_OUT], refs[14:14 + len(_FRONT_OUT)]))
    zs = refs[14 + len(_FRONT_OUT)]

    i = pl.program_id(1)
    modulate = lambda xx: _ln(xx) * (1.0 + sc_ref[0]) + sh_ref[0]
    hc = modulate(x_ref[0]).astype(BF16)
    row = _iota((tr, 1), 0) + i * tr
    if grid_shift:
        keep_p = jnp.where(i > 0, 1.0, 0.0)
        keep_n = jnp.where(i < t_total // tr - 1, 1.0, 0.0)
        hall = jnp.concatenate([(modulate(xp_ref[0]) * keep_p).astype(BF16), hc,
                                (modulate(xn_ref[0]) * keep_n).astype(BF16)], axis=0)
        zall = jnp.dot(hall, w_ref[:, 0:RW_PAD], preferred_element_type=F32)
        zc = zall[GRID_W:GRID_W + tr]
        up = zall[0:tr]
        down = zall[2 * GRID_W:2 * GRID_W + tr]
    else:
        zc = jnp.dot(hc, w_ref[:, 0:RW_PAD], preferred_element_type=F32)
    zg = jnp.dot(hc, w_ref[:, RW_PAD:IN_PAD], preferred_element_type=F32)
    zs[0:8, :] = jnp.zeros((8, RW_PAD), F32)
    zs[8 + tr:16 + tr, :] = jnp.zeros((8, RW_PAD), F32)
    zs[8:8 + tr, :] = zc
    left = zs[7:7 + tr, :]
    right = zs[9:9 + tr, :]
    if grid_shift:
        col = row & (GRID_W - 1)
        left = jnp.where(col > 0, left, 0.0)
        right = jnp.where(col < GRID_W - 1, right, 0.0)
        nb = 0.25 * (up + down + left + right)
    else:
        left = jnp.where(row > 0, left, 0.0)
        right = jnp.where(row < t_total - 1, right, 0.0)
        nb = 0.5 * (left + right)
    zr = zc + (nb - zc) * mu_ref[...]

    r = zr[:, COL_R:COL_R + 512]
    k = zr[:, COL_K:COL_K + 512]
    v = zr[:, COL_V:COL_V + 512]
    lo = zr[:, COL_LORA:COL_LORA + 128]
    gd = zr[:, COL_GD:COL_GD + 128]
    seg = seg_ref[...]

    lane = _iota((1, 128), 1)
    lor = _dot(jnp.where(lane < 64, jnp.tanh(lo), lo), wl_ref[...])
    kkr = k * kk_p[...]
    kk = kkr * lax.rsqrt(_dot(kkr * kkr, seg) + 1e-12)
    outs["r"][0] = r.astype(BF16)
    outs["kk"][0] = kk.astype(BF16)
    outs["v"][0] = v.astype(BF16)
    outs["gout"][0] = _dot(_sigmoid(gd), g2_ref[...]).astype(BF16)
    kh_sum = None
    for d, sfx in ((0, "_f"), (1, "_b")):
        u = lor[:, d * 512:(d + 1) * 512] + w0_ref[d:d + 1, :]
        outs["lw" + sfx][0] = (-LOG_DECAY_SCALE) * _sigmoid(u)
        a = _sigmoid(a0_ref[d:d + 1, :] + lor[:, 1024 + d * 512:1536 + d * 512])
        kh = k * (1.0 + (a - 1.0) * ka_p[...])
        outs["kh" + sfx][0] = kh.astype(BF16)
        outs["kka" + sfx][0] = (kk * a).astype(BF16)
        kh_sum = kh if kh_sum is None else kh_sum + kh
    outs["bonus"][0] = (_dot(r * rk_p[...] * kh_sum, seg) * v).astype(BF16)

    g0 = RW_PAD
    outs["q"][0] = (zg[:, COL_Q - g0:COL_Q - g0 + 256] * (RW_HEAD ** -0.5)).astype(BF16)
    outs["kg"][0] = zg[:, COL_KG - g0:COL_KG - g0 + 256].astype(BF16)
    outs["vg"][0] = zg[:, COL_VG - g0:COL_VG - g0 + 512].astype(BF16)
    gg = zg[:, COL_GG - g0:COL_GG - g0 + 512]
    outs["gate"][0] = (gg * _sigmoid(gg)).astype(BF16)
    gl = _dot(zg[:, COL_GA - g0:COL_GA - g0 + 128], a2_ref[...])
    for d, sfx in ((0, "_f"), (1, "_b")):
        xg = gl[:, d * 256:(d + 1) * 256] + ab_ref[d:d + 1, :]
        outs["la" + sfx][0] = -_softplus(-xg) * (1.0 / GLA_GATE_NORM)


def _front_call(x, sh, sc, grid_shift, pp):
    b, t, d = x.shape
    tr = min(t, 512)
    nt = t // tr
    hb = tr // GRID_W
    nh = t // GRID_W
    full = lambda shape: pl.BlockSpec(shape, lambda i, j: (0,) * len(shape))
    vec = pl.BlockSpec((1, 1, d), lambda i, j: (i, 0, 0))
    in_specs = [pl.BlockSpec((1, tr, d), lambda i, j: (i, j, 0))]
    args = [x]
    if grid_shift:
        in_specs += [
            pl.BlockSpec((1, GRID_W, d), lambda i, j: (i, jnp.maximum(j * hb - 1, 0), 0)),
            pl.BlockSpec((1, GRID_W, d), lambda i, j: (i, jnp.minimum((j + 1) * hb, nh - 1), 0)),
        ]
        args += [x, x]
    in_specs += [vec, vec]
    args += [sh, sc]
    for nm in _FRONT_PARAMS:
        in_specs.append(full(pp[nm].shape))
        args.append(pp[nm])
    out_specs = [pl.BlockSpec((1, tr, w), lambda i, j: (i, j, 0)) for _, w, _ in _FRONT_OUT]
    out_shape = [jax.ShapeDtypeStruct((b, t, w), dt) for _, w, dt in _FRONT_OUT]
    res = pl.pallas_call(
        functools.partial(_front_kernel, grid_shift, t, tr),
        grid=(b, nt),
        in_specs=in_specs,
        out_specs=out_specs,
        out_shape=out_shape,
        scratch_shapes=[pltpu.VMEM((tr + 16, RW_PAD), F32)],
        compiler_params=_params(("arbitrary", "arbitrary")),
        name="front_grid" if grid_shift else "front_seq",
    )(*args)
    return dict(zip([nm for nm, _, _ in _FRONT_OUT], res))


def _bd(x, blk):
    reps = blk.shape[0] // x.shape[0]
    return jnp.where(blk, jnp.concatenate([x] * reps, axis=0), jnp.zeros((), x.dtype))


def _scan_masks(rev, width):
    c = CHUNK
    ri = _iota((c, width), 0)
    jj = _iota((c, width), 1) & (c - 1)
    strict = (jj > ri) if rev else (jj < ri)
    incl = (jj >= ri) if rev else (jj <= ri)
    eye = jj == ri
    a = _iota((c, c), 0)
    bq = _iota((c, c), 1)
    mcum = jnp.where((bq >= a) if rev else (bq <= a), 1.0, 0.0).astype(BF16)
    levels = [strict & ((jj >> (l + 1)) == (ri >> (l + 1))) & ((jj >> l) != (ri >> l)) for l in range(6)]
    return strict, incl, eye, mcum, levels


def _each(f, *lists):
    return [f(*a) for a in zip(*lists)]


def _cumsum(x, mcum):
    hi = x.astype(BF16)
    lo = (x - hi.astype(F32)).astype(BF16)
    return (jnp.dot(mcum, hi, preferred_element_type=F32) + jnp.dot(mcum, lo, preferred_element_type=F32))


def _wkv_chains(need_y, ins, hts, masks, blk):
    c = CHUNK
    revs = [i[0] for i in ins]
    mk = [masks[int(rv)] for rv in revs]
    g = _each(lambda i, m: _cumsum(i[6], m[3]), ins, mk)
    gl = [gg[0:1] if rv else gg[c - 1:c] for gg, rv in zip(g, revs)]
    x2 = _each(lambda i, gg: jnp.concatenate([(i[2] * jnp.exp(gg - i[6])).astype(BF16),
                                              (i[1] * jnp.exp(gg)).astype(BF16)], axis=0), ins, g)
    at = _each(lambda i, gg: (i[3] * jnp.exp(-gg)).astype(BF16), ins, g)
    kt = _each(lambda i, gg: (i[4] * jnp.exp(-gg)).astype(BF16), ins, g)
    sa = _each(lambda x, a: _dot_nt(x, _bd(a, blk)), x2, at)
    sk = _each(lambda x, k: _dot_nt(x, _bd(k, blk)), x2, kt)
    tm = _each(lambda s, m: jnp.where(m[2], 1.0, 0.0) - jnp.where(m[4][0], s[:c], 0.0), sa, mk)
    for l in range(1, 6):
        x = _each(lambda t, s, m: _dot(t, _bd(jnp.where(m[4][l], s[:c], 0.0).astype(BF16), blk)), tm, sa, mk)
        tm = _each(lambda t, xx: t - _dot(xx, _bd(t.astype(BF16), blk)), tm, x)
    xh = _each(_dot_nt, x2, hts)
    vb = [i[5].astype(BF16) for i in ins]
    vbd = [_bd(v, blk) for v in vb]
    lkv = _each(lambda s, m, vd: _dot(jnp.where(m[0], s[:c], 0.0), vd), sk, mk, vbd)
    ub = _each(lambda t, h, lv: (-_dot(t, _bd((h[:c] + lv).astype(BF16), blk))).astype(BF16), tm, xh, lkv)
    upd = _each(lambda u, v, i, gg, gle: _dot_tn(
        jnp.concatenate([u, v], axis=0),
        jnp.concatenate([(i[3] * jnp.exp(gle - gg)).astype(BF16), (i[4] * jnp.exp(gle - gg)).astype(BF16)], axis=0)),
        ub, vb, ins, g, gl)
    ht_new = _each(lambda h, gle, up: h * jnp.exp(gle) + jnp.where(blk, up, 0.0), hts, gl, upd)
    ys = [None] * len(ins)
    if need_y:
        ya = _each(lambda s, m, u: _dot(jnp.where(m[1], s[c:], 0.0), _bd(u, blk)), sa, mk, ub)
        yk = _each(lambda s, m, vd: _dot(jnp.where(m[1], s[c:], 0.0), vd), sk, mk, vbd)
        ys = _each(lambda h, a, k: h[c:] + a + k, xh, ya, yk)
    return ys, ht_new


SCAN_TB = 256
WKV_NB = 2
GLA_NB = 4


def _wkv_kernel(need_y, nb, tb, *refs):
    fwd, bwd, h0_ref, rest = refs[:6], refs[6:12], refs[12], refs[13:]
    if need_y:
        yf_ref, yb_ref, hout_ref, hs = rest
    else:
        hout_ref, hs = rest
    c = CHUNK
    nc = tb // c
    w = 256
    j = pl.program_id(1)
    blk = (_iota((w, w), 0) >> 6) == (_iota((w, w), 1) >> 6)
    masks = (_scan_masks(False, w), _scan_masks(True, w))

    @pl.when(j == 0)
    def _():
        hs[...] = h0_ref[...]

    def body(ci, carry):
        where, ins, hts = [], [], []
        for ib in range(nb):
            for g in range(RW_WIDTH // w):
                lanes = slice(g * w, (g + 1) * w)
                for d, views in enumerate((fwd, bwd)):
                    cc = ci if d == 0 else nc - 1 - ci
                    rows = pl.ds(pl.multiple_of(cc * c, c), c)
                    r, kk, v, kh, kka = [ref[ib, rows, lanes].astype(F32) for ref in views[:5]]
                    ins.append((d == 1, r, kk, kka, kh, v, views[5][ib, rows, lanes]))
                    hts.append(hs[ib, g, d])
                    where.append((ib, g, d, rows, lanes))
        ys, hts = _wkv_chains(need_y, ins, hts, masks, blk)
        for (ib, g, d, rows, lanes), y, ht in zip(where, ys, hts):
            hs[ib, g, d] = ht
            if need_y:
                y_ref = yb_ref if d else yf_ref
                y_ref[ib, rows, lanes] = y.astype(BF16)
        return carry

    lax.fori_loop(0, nc, body, 0)

    @pl.when(j == pl.num_programs(1) - 1)
    def _():
        hout_ref[...] = hs[...]


def _wkv_call(p, h0, need_y):
    b, t, _ = p["r"].shape
    nb = min(WKV_NB, b)
    tb = min(SCAN_TB, t)
    nblk = t // tb
    ng = RW_WIDTH // 256
    fwd = pl.BlockSpec((nb, tb, RW_WIDTH), lambda i, j: (i, j, 0))
    bwd = pl.BlockSpec((nb, tb, RW_WIDTH), lambda i, j: (i, nblk - 1 - j, 0))
    st = pl.BlockSpec((nb, ng, 2, 256, 256), lambda i, j: (i, 0, 0, 0, 0))
    out_specs = [st]
    out_shape = [jax.ShapeDtypeStruct((b, ng, 2, 256, 256), F32)]
    if need_y:
        out_specs = [fwd, bwd] + out_specs
        out_shape = [jax.ShapeDtypeStruct((b, t, RW_WIDTH), BF16)] * 2 + out_shape
    res = pl.pallas_call(
        functools.partial(_wkv_kernel, need_y, nb, tb),
        grid=(b // nb, nblk),
        in_specs=[fwd] * 6 + [bwd] * 6 + [st],
        out_specs=out_specs,
        out_shape=out_shape,
        scratch_shapes=[pltpu.VMEM((nb, ng, 2, 256, 256), F32)],
        compiler_params=_params(("arbitrary", "arbitrary")),
        name="wkv_y" if need_y else "wkv_state",
    )(p["r"], p["kk"], p["v"], p["kh_f"], p["kka_f"], p["lw_f"],
      p["r"], p["kk"], p["v"], p["kh_b"], p["kka_b"], p["lw_b"], h0)
    return res if need_y else (None, None, res[0])


def _gla_chains(need_o, ins, sts, masks, blk_k, blk_v, blk_s):
    c = CHUNK
    revs = [i[0] for i in ins]
    mk = [masks[int(rv)] for rv in revs]
    b = _each(lambda i, m: _cumsum(i[4], m[3]), ins, mk)
    bl = [bb[0:1] if rv else bb[c - 1:c] for bb, rv in zip(b, revs)]
    vb = [i[3].astype(BF16) for i in ins]
    upd = _each(lambda v, i, bb, ble: _dot_tn(v, (i[2] * jnp.exp(ble - bb)).astype(BF16)), vb, ins, b, bl)
    st_new = _each(lambda s, ble, up: s * jnp.exp(ble) + jnp.where(blk_s, up, 0.0), sts, bl, upd)
    os_ = [None] * len(ins)
    if need_o:
        q_in = _each(lambda i, bb: (i[1] * jnp.exp(bb)).astype(BF16), ins, b)
        sc = _each(lambda q, i, bb, m: jnp.where(
            m[1], _dot_nt(q, _bd((i[2] * jnp.exp(-bb)).astype(BF16), blk_k)), 0.0), q_in, ins, b, mk)
        oi = _each(lambda s, v: _dot(s, _bd(v, blk_v)), sc, vb)
        ox = _each(_dot_nt, q_in, sts)
        os_ = _each(lambda a, x: a + x, oi, ox)
    return os_, st_new


def _gla_kernel(need_o, nb, tb, *refs):
    fwd, bwd, s0_ref, rest = refs[:4], refs[4:8], refs[8], refs[9:]
    if need_o:
        of_ref, ob_ref, sout_ref, ss = rest
    else:
        sout_ref, ss = rest
    c = CHUNK
    nc = tb // c
    j = pl.program_id(1)
    blk_k = (_iota((256, 256), 0) >> 6) == (_iota((256, 256), 1) >> 6)
    blk_v = (_iota((256, 512), 0) >> 6) == (_iota((256, 512), 1) >> 7)
    blk_s = (_iota((512, 256), 0) >> 7) == (_iota((512, 256), 1) >> 6)
    masks = (_scan_masks(False, 256), _scan_masks(True, 256))

    @pl.when(j == 0)
    def _():
        ss[...] = s0_ref[...]

    def body(ci, carry):
        where, ins, sts = [], [], []
        for ib in range(nb):
            for d, views in enumerate((fwd, bwd)):
                cc = ci if d == 0 else nc - 1 - ci
                rows = pl.ds(pl.multiple_of(cc * c, c), c)
                q, k, v = [ref[ib, rows, :].astype(F32) for ref in views[:3]]
                ins.append((d == 1, q, k, v, views[3][ib, rows, :]))
                sts.append(ss[ib, d])
                where.append((ib, d, rows))
        os_, sts = _gla_chains(need_o, ins, sts, masks, blk_k, blk_v, blk_s)
        for (ib, d, rows), o, st in zip(where, os_, sts):
            ss[ib, d] = st
            if need_o:
                o_ref = ob_ref if d else of_ref
                o_ref[ib, rows, :] = o.astype(BF16)
        return carry

    lax.fori_loop(0, nc, body, 0)

    @pl.when(j == pl.num_programs(1) - 1)
    def _():
        sout_ref[...] = ss[...]


def _gla_call(p, s0, need_o):
    b, t, _ = p["q"].shape
    nb = min(GLA_NB, b)
    tb = min(SCAN_TB, t)
    nblk = t // tb
    fwd = lambda w: pl.BlockSpec((nb, tb, w), lambda i, j: (i, j, 0))
    bwd = lambda w: pl.BlockSpec((nb, tb, w), lambda i, j: (i, nblk - 1 - j, 0))
    st = pl.BlockSpec((nb, 2, GLA_VAL, GLA_KEY), lambda i, j: (i, 0, 0, 0))
    out_specs = [st]
    out_shape = [jax.ShapeDtypeStruct((b, 2, GLA_VAL, GLA_KEY), F32)]
    if need_o:
        out_specs = [fwd(GLA_VAL), bwd(GLA_VAL)] + out_specs
        out_shape = [jax.ShapeDtypeStruct((b, t, GLA_VAL), BF16)] * 2 + out_shape
    res = pl.pallas_call(
        functools.partial(_gla_kernel, need_o, nb, tb),
        grid=(b // nb, nblk),
        in_specs=[fwd(256), fwd(256), fwd(512), fwd(256), bwd(256), bwd(256), bwd(512), bwd(256), st],
        out_specs=out_specs,
        out_shape=out_shape,
        scratch_shapes=[pltpu.VMEM((nb, 2, GLA_VAL, GLA_KEY), F32)],
        compiler_params=_params(("arbitrary", "arbitrary")),
        name="gla_o" if need_o else "gla_state",
    )(p["q"], p["kg"], p["vg"], p["la_f"], p["q"], p["kg"], p["vg"], p["la_b"], s0)
    return res if need_o else (None, None, res[0])


def _post_kernel(yf_ref, yb_ref, bon_ref, gout_ref, of_ref, ob_ref, gate_ref, x_ref, g1_ref, gnw_ref,
                 gnb_ref, nw_ref, wout_ref, l1g_ref, l1b_ref, seg64_ref, seg128_ref, out_ref):
    y = yf_ref[0].astype(F32) + yb_ref[0].astype(F32)
    seg64 = seg64_ref[...]
    mu = _dot(y, seg64) * (1.0 / RW_HEAD)
    yc = y - mu
    var = _dot(yc * yc, seg64) * (1.0 / RW_HEAD)
    y_n = yc * lax.rsqrt(var + RW_GN_EPS) * gnw_ref[...] + gnb_ref[...]
    rw_out = (y_n + bon_ref[0].astype(F32)) * gout_ref[0].astype(F32)
    o = of_ref[0].astype(F32) + ob_ref[0].astype(F32)
    ms = _dot(o * o, seg128_ref[...]) * (1.0 / GLA_DV)
    gla_out = o * lax.rsqrt(ms + GLA_NORM_EPS) * nw_ref[...] * gate_ref[0].astype(F32)
    mix = jnp.concatenate([rw_out.astype(BF16), gla_out.astype(BF16)], axis=-1)
    proj = jnp.dot(mix, wout_ref[...], preferred_element_type=F32)
    xs = DN_ALPHA * x_ref[0] + g1_ref[0] * proj
    out_ref[0] = _ln(xs) * l1g_ref[...] + l1b_ref[...]


def _post_call(yf, yb, bonus, gout, of, ob, gate, x, g1, pp):
    b, t, d = x.shape
    tm = min(t, 512)
    tile = lambda w: pl.BlockSpec((1, tm, w), lambda i, j: (i, j, 0))
    full = lambda a: pl.BlockSpec(a.shape, lambda i, j: (0,) * a.ndim)
    consts = [pp["gn_w"], pp["gn_b"], pp["gla_nw"], pp["w_out"], pp["ln1_g"], pp["ln1_b"],
              pp["seg64"], pp["seg128"]]
    return pl.pallas_call(
        _post_kernel,
        grid=(b, t // tm),
        in_specs=[tile(512)] * 7 + [tile(d), pl.BlockSpec((1, 1, d), lambda i, j: (i, 0, 0))]
        + [full(a) for a in consts],
        out_specs=tile(d),
        out_shape=jax.ShapeDtypeStruct((b, t, d), F32),
        compiler_params=_params(("arbitrary", "arbitrary")),
        name="post",
    )(yf, yb, bonus, gout, of, ob, gate, x, g1, *consts)


def _prefix_excl(m, su):
    e, t = m.shape
    off = jnp.zeros((e, 1), F32)
    parts = []
    for j in range(t // 128):
        blk = m[:, j * 128:(j + 1) * 128]
        parts.append(_dot(blk, su) + off)
        off = off + jnp.sum(blk, axis=1, keepdims=True)
    return jnp.concatenate(parts, axis=1)


def _route_kernel(cap, x_ref, sh_ref, sc_ref, rw_ref, xg_ref, aff_ref, pos_ref, h2_s, pe_s):
    e = pl.program_id(1)
    t = x_ref.shape[1]

    @pl.when(e == 0)
    def _():
        h2 = _ln(x_ref[0]) * (1.0 + sc_ref[0]) + sh_ref[0]
        h2_s[...] = h2.astype(BF16)
        mm = lambda p, q: jnp.dot(p, q, preferred_element_type=F32)
        h_hi = h2.astype(BF16)
        h_lo = (h2 - h_hi.astype(F32)).astype(BF16)
        w_hi = rw_ref[...].astype(BF16)
        w_lo = (rw_ref[...] - w_hi.astype(F32)).astype(BF16)
        logits = mm(h_hi, w_hi) + (mm(h_lo, w_hi) + mm(h_hi, w_lo))
        lane = _iota((1, 128), 1)
        logits = jnp.where(lane < N_EXPERTS, logits, -1e30)
        m = jnp.max(logits, axis=-1, keepdims=True)
        ex = jnp.exp(logits - m)
        aff = ex / jnp.sum(ex, axis=-1, keepdims=True)
        aff_ref[0] = aff
        aff_t = aff.T[0:N_EXPERTS, :]
        def bs(_, lohi):
            lo, hi = lohi
            mid = lo + ((hi - lo) >> 1)
            mid_f = pltpu.bitcast(mid, F32)[:, 0:1]
            cnt = jnp.sum(jnp.where(aff_t >= mid_f, 1.0, 0.0), axis=1, keepdims=True)
            ok = cnt >= cap
            return jnp.where(ok, mid, lo), jnp.where(ok, hi, mid)

        lo0 = jnp.zeros((N_EXPERTS, 128), jnp.int32)
        hi0 = jnp.full((N_EXPERTS, 128), 0x3F800001, jnp.int32)
        thr_bits, _ = lax.fori_loop(0, 31, bs, (lo0, hi0))
        thr = pltpu.bitcast(thr_bits, F32)[:, 0:1]
        gt = jnp.where(aff_t > thr, 1.0, 0.0)
        eq = jnp.where(aff_t == thr, 1.0, 0.0)
        need = cap - jnp.sum(gt, axis=1, keepdims=True)
        su = jnp.where(_iota((128, 128), 0) < _iota((128, 128), 1), 1.0, 0.0).astype(BF16)
        sel = gt + eq * jnp.where(_prefix_excl(eq, su) < need, 1.0, 0.0)
        pos = jnp.where(sel > 0.0, _prefix_excl(sel, su), -1.0)
        pe_s[0:N_EXPERTS, :] = pos
        pe_s[N_EXPERTS:, :] = jnp.full((128 - N_EXPERTS, t), -1.0, F32)
        pos_ref[0] = pe_s[...].T

    prow = pe_s[pl.ds(e, 1), :]
    slot = _iota((cap, t), 0).astype(F32)
    onehot = jnp.where(prow == slot, 1.0, 0.0).astype(BF16)
    xg_ref[0, 0] = jnp.dot(onehot, h2_s[...], preferred_element_type=F32).astype(BF16)


def _route_call(x1, sh2, sc2, router_pad, cap):
    b, t, d = x1.shape
    return pl.pallas_call(
        functools.partial(_route_kernel, cap),
        grid=(b, N_EXPERTS),
        in_specs=[pl.BlockSpec((1, t, d), lambda i, e: (i, 0, 0)),
                  pl.BlockSpec((1, 1, d), lambda i, e: (i, 0, 0)),
                  pl.BlockSpec((1, 1, d), lambda i, e: (i, 0, 0)),
                  pl.BlockSpec((d, 128), lambda i, e: (0, 0))],
        out_specs=[pl.BlockSpec((1, 1, cap, d), lambda i, e: (i, e, 0, 0)),
                   pl.BlockSpec((1, t, 128), lambda i, e: (i, 0, 0)),
                   pl.BlockSpec((1, t, 128), lambda i, e: (i, 0, 0))],
        out_shape=[jax.ShapeDtypeStruct((b, N_EXPERTS, cap, d), BF16),
                   jax.ShapeDtypeStruct((b, t, 128), F32),
                   jax.ShapeDtypeStruct((b, t, 128), F32)],
        scratch_shapes=[pltpu.VMEM((t, d), BF16), pltpu.VMEM((128, t), F32)],
        compiler_params=_params(("arbitrary", "arbitrary")),
        name="route",
    )(x1, sh2, sc2, router_pad)


def _ffn_kernel(x_ref, wg_ref, wu_ref, wd_ref, y_ref):
    x = x_ref[0, 0]
    g = jnp.dot(x, wg_ref[0], preferred_element_type=F32)
    u = jnp.dot(x, wu_ref[0], preferred_element_type=F32)
    h = (g * _sigmoid(g) * u).astype(BF16)
    y_ref[0, 0] = jnp.dot(h, wd_ref[0], preferred_element_type=F32).astype(BF16)


def _ffn_call(xg, wg, wu, wd):
    b, ne, cap, d = xg.shape
    f = wg.shape[2]
    return pl.pallas_call(
        _ffn_kernel,
        grid=(ne, b),
        in_specs=[pl.BlockSpec((1, 1, cap, d), lambda e, i: (i, e, 0, 0)),
                  pl.BlockSpec((1, d, f), lambda e, i: (e, 0, 0)),
                  pl.BlockSpec((1, d, f), lambda e, i: (e, 0, 0)),
                  pl.BlockSpec((1, f, d), lambda e, i: (e, 0, 0))],
        out_specs=pl.BlockSpec((1, 1, cap, d), lambda e, i: (i, e, 0, 0)),
        out_shape=jax.ShapeDtypeStruct((b, ne, cap, d), BF16),
        compiler_params=_params(("arbitrary", "arbitrary")),
        name="ffn",
    )(xg, wg, wu, wd)


def _combine_kernel(cap, y_ref, pos_ref, aff_ref, x_ref, g2_ref, l2g_ref, l2b_ref, ex_ref, out_ref):
    tm = x_ref.shape[1]
    ne = N_EXPERTS
    posb = _dot(pos_ref[0], ex_ref[...])
    slot = (_iota((1, ne * cap), 1) & (cap - 1)).astype(F32)
    hit = posb == slot
    gates = aff_ref[0]
    acc = None
    for e in range(ne):
        oh = jnp.where(hit[:, e * cap:(e + 1) * cap], 1.0, 0.0).astype(BF16)
        part = jnp.dot(oh, y_ref[0, e], preferred_element_type=F32) * gates[:, e:e + 1]
        acc = part if acc is None else acc + part
    xs = DN_ALPHA * x_ref[0] + g2_ref[0] * acc
    out_ref[0] = _ln(xs) * l2g_ref[...] + l2b_ref[...]


def _combine_call(y, pos_t, aff_t, x1, g2, l2g, l2b, expand, cap):
    b, t, d = x1.shape
    tm = min(t, 512)
    return pl.pallas_call(
        functools.partial(_combine_kernel, cap),
        grid=(b, t // tm),
        in_specs=[pl.BlockSpec((1, N_EXPERTS, cap, d), lambda i, j: (i, 0, 0, 0)),
                  pl.BlockSpec((1, tm, 128), lambda i, j: (i, j, 0)),
                  pl.BlockSpec((1, tm, 128), lambda i, j: (i, j, 0)),
                  pl.BlockSpec((1, tm, d), lambda i, j: (i, j, 0)),
                  pl.BlockSpec((1, 1, d), lambda i, j: (i, 0, 0)),
                  pl.BlockSpec((1, d), lambda i, j: (0, 0)),
                  pl.BlockSpec((1, d), lambda i, j: (0, 0)),
                  pl.BlockSpec(expand.shape, lambda i, j: (0, 0))],
        out_specs=pl.BlockSpec((1, tm, d), lambda i, j: (i, j, 0)),
        out_shape=jax.ShapeDtypeStruct((b, t, d), F32),
        compiler_params=_params(("arbitrary", "arbitrary")),
        name="combine",
    )(y, pos_t, aff_t, x1, g2, l2g, l2b, expand)


def _pad_cols(parts, total):
    rows = parts[0][0].shape[0]
    out = jnp.zeros((rows, total), parts[0][0].dtype)
    for a, off in parts:
        out = lax.dynamic_update_slice(out, a, (0, off))
    return out


def _block_ones(n, width):
    i = jnp.arange(n) // width
    return (i[:, None] == i[None, :]).astype(BF16)


def _layout_params(w_in, rw_mu, rw_w0, rw_w2, rw_a0, rw_a2, rw_g2, rw_k_k, rw_k_a, rw_r_k,
                   rw_gn_w, rw_gn_b, gla_a2, gla_a_b, gla_norm_w, w_out, ln1_g, ln1_b):
    rw_in = 1760
    segs = [(0, 1536, COL_R), (1536, 1664, COL_LORA), (1664, 1760, COL_GD),
            (rw_in, rw_in + 256, COL_Q), (rw_in + 256, rw_in + 512, COL_KG),
            (rw_in + 512, rw_in + 1024, COL_VG), (rw_in + 1024, rw_in + 1536, COL_GG),
            (rw_in + 1536, rw_in + 1568, COL_GA)]
    w_pad = _pad_cols([(w_in[:, a:b], off) for a, b, off in segs], IN_PAD).astype(BF16)
    mu = _pad_cols([(rw_mu[None, a:b], off) for a, b, off in segs[:3]], RW_PAD)
    w_lora = jnp.zeros((128, 2048), F32)
    for i, m in enumerate((rw_w2[0], rw_w2[1], rw_a2[0], rw_a2[1])):
        w_lora = lax.dynamic_update_slice(w_lora, m, (32 * i, 512 * i))
    g2 = jnp.zeros((128, 512), F32).at[:96].set(rw_g2)
    a2 = jnp.zeros((128, 512), F32).at[0:16, 0:256].set(gla_a2[0]).at[16:32, 256:512].set(gla_a2[1])
    row = lambda a: a.reshape(1, -1)
    return dict(
        w_in=w_pad, mu=mu, k_k=row(rw_k_k), k_a=row(rw_k_a), r_k=row(rw_r_k), w0=rw_w0, a0=rw_a0,
        ab=gla_a_b, w_lora=w_lora.astype(BF16), g2=g2.astype(BF16), a2=a2.astype(BF16),
        seg64=_block_ones(512, 64), seg128=_block_ones(512, 128),
        gn_w=row(rw_gn_w), gn_b=row(rw_gn_b), gla_nw=row(gla_norm_w), w_out=w_out.astype(BF16),
        ln1_g=row(ln1_g), ln1_b=row(ln1_b))


def kernel(x, c, ctx, c_ctx, ada_w, ada_b, w_in, rw_mu, rw_w0, rw_w2, rw_a0, rw_a2, rw_g2, rw_k_k, rw_k_a, rw_r_k, rw_gn_w, rw_gn_b, gla_a2, gla_a_b, gla_norm_w, w_out, ln1_g, ln1_b, router_w, ex_gate, ex_up, ex_down, ln2_g, ln2_b):
    assert ada_w.shape[0] == 1, "single-layer block"
    b, t, d = x.shape
    cap = CAPACITY_FACTOR * t // N_EXPERTS
    pp = _layout_params(w_in[0], rw_mu[0], rw_w0[0], rw_w2[0], rw_a0[0], rw_a2[0], rw_g2[0],
                        rw_k_k[0], rw_k_a[0], rw_r_k[0], rw_gn_w[0], rw_gn_b[0], gla_a2[0],
                        gla_a_b[0], gla_norm_w[0], w_out[0], ln1_g[0], ln1_b[0])

    rows = -(-(b + 1) // 8) * 8
    cs = jnp.zeros((rows, d), F32).at[:b].set(c).at[b].set(c_ctx)
    mod = _mod_call(cs, ada_w[0], ada_b[0][None])
    sh1, sc1, g1, sh2, sc2, g2 = [m[:, None, :] for m in jnp.split(mod[:b], 6, axis=-1)]
    sh1c, sc1c = [jnp.broadcast_to(m[None, None, :], (b, 1, d)) for m in jnp.split(mod[b], 6)[:2]]

    pc = _front_call(ctx, sh1c, sc1c, False, pp)
    zero_h = jnp.zeros((b, RW_WIDTH // 256, 2, 256, 256), F32)
    zero_s = jnp.zeros((b, 2, GLA_VAL, GLA_KEY), F32)
    _, _, h_ctx = _wkv_call(pc, zero_h, False)
    _, _, s_ctx = _gla_call(pc, zero_s, False)

    pz = _front_call(x, sh1, sc1, True, pp)
    yf, yb, _ = _wkv_call(pz, h_ctx, True)
    of, ob, _ = _gla_call(pz, s_ctx, True)
    x1 = _post_call(yf, yb, pz["bonus"], pz["gout"], of, ob, pz["gate"], x, g1, pp)

    router_pad = jnp.zeros((d, 128), F32).at[:, :N_EXPERTS].set(router_w[0])
    xg, aff_t, pos_t = _route_call(x1, sh2, sc2, router_pad, cap)
    ye = _ffn_call(xg, ex_gate[0].astype(BF16), ex_up[0].astype(BF16), ex_down[0].astype(BF16))
    eidx = jnp.arange(N_EXPERTS * cap) // cap
    expand = (jnp.arange(128)[:, None] == eidx[None, :]).astype(BF16)
    return _combine_call(ye, pos_t, aff_t, x1, g2, ln2_g[0][None], ln2_b[0][None], expand, cap)
```

```python
import functools

import jax
import jax.numpy as jnp
from jax import lax
from jax.experimental import pallas as pl
from jax.experimental.pallas import tpu as pltpu

F32 = jnp.float32
BF16 = jnp.bfloat16
HIGHEST = lax.Precision.HIGHEST

D_MODEL = 1024
GRID_W = 64
RW_WIDTH = 512
RW_HEAD = 64
RW_GN_EPS = 64e-5
GLA_KEY = 256
GLA_VAL = 512
GLA_DV = 128
GLA_GATE_NORM = 16.0
GLA_NORM_EPS = 1e-5
N_EXPERTS = 16
CAPACITY_FACTOR = 2
DN_ALPHA = 2.0 ** 0.25
LN_EPS = 1e-5
CHUNK = 64
LOG_DECAY_SCALE = 0.6065306597126334

COL_R, COL_K, COL_V, COL_LORA, COL_GD = 0, 512, 1024, 1536, 1664
RW_PAD = 1792
COL_Q, COL_KG, COL_VG, COL_GG, COL_GA = 1792, 2048, 2304, 2816, 3328
IN_PAD = 3456

VMEM_LIMIT = 56 * 1024 * 1024


def _params(sem):
    return pltpu.CompilerParams(dimension_semantics=sem, vmem_limit_bytes=VMEM_LIMIT)


def _dot(a, b):
    return jnp.dot(a.astype(BF16), b.astype(BF16), preferred_element_type=F32)


def _dot_nt(a, b):
    return lax.dot_general(a.astype(BF16), b.astype(BF16), (((1,), (1,)), ((), ())),
                           preferred_element_type=F32)


def _dot_tn(a, b):
    return lax.dot_general(a.astype(BF16), b.astype(BF16), (((0,), (0,)), ((), ())),
                           preferred_element_type=F32)


def _sigmoid(x):
    return 1.0 / (1.0 + jnp.exp(-x))


def _softplus(x):
    return jnp.maximum(x, 0.0) + jnp.log(1.0 + jnp.exp(-jnp.abs(x)))


def _ln(x):
    mu = jnp.mean(x, axis=-1, keepdims=True)
    xc = x - mu
    var = jnp.mean(xc * xc, axis=-1, keepdims=True)
    return xc * lax.rsqrt(var + LN_EPS)


def _iota(shape, dim):
    return lax.broadcasted_iota(jnp.int32, shape, dim)


def _mod_kernel(c_ref, w_ref, b_ref, o_ref):
    c = c_ref[...]
    s = c * _sigmoid(c)
    o_ref[...] = jnp.dot(s, w_ref[...], precision=HIGHEST, preferred_element_type=F32) + b_ref[...]


def _mod_call(cs, ada_w, ada_b):
    rows, d = cs.shape
    n = ada_w.shape[1]
    bn = 1536
    return pl.pallas_call(
        _mod_kernel,
        grid=(n // bn,),
        in_specs=[pl.BlockSpec((rows, d), lambda j: (0, 0)),
                  pl.BlockSpec((d, bn), lambda j: (0, j)),
                  pl.BlockSpec((1, bn), lambda j: (0, j))],
        out_specs=pl.BlockSpec((rows, bn), lambda j: (0, j)),
        out_shape=jax.ShapeDtypeStruct((rows, n), F32),
        compiler_params=_params(("arbitrary",)),
        name="mod",
    )(cs, ada_w, ada_b)


_FRONT_OUT = (
    ("r", 512, BF16), ("kk", 512, BF16), ("v", 512, BF16),
    ("kh_f", 512, BF16), ("kh_b", 512, BF16), ("kka_f", 512, BF16), ("kka_b", 512, BF16),
    ("lw_f", 512, F32), ("lw_b", 512, F32), ("gout", 512, BF16), ("bonus", 512, BF16),
    ("q", 256, BF16), ("kg", 256, BF16), ("vg", 512, BF16),
    ("la_f", 256, F32), ("la_b", 256, F32), ("gate", 512, BF16),
)
_FRONT_PARAMS = ("w_in", "mu", "k_k", "k_a", "r_k", "w0", "a0", "ab", "w_lora", "g2", "a2", "seg64")


def _front_kernel(grid_shift, t_total, tr, *refs):
    if grid_shift:
        x_ref, xp_ref, xn_ref = refs[:3]
        refs = refs[3:]
    else:
        x_ref = refs[0]
        refs = refs[1:]
    sh_ref, sc_ref = refs[:2]
    (w_ref, mu_ref, kk_p, ka_p, rk_p, w0_ref, a0_ref, ab_ref, wl_ref, g2_ref, a2_ref, seg_ref) = refs[2:14]
    outs = dict(zip([n for n, _, _ in _FRONT_OUT], refs[14:14 + len(_FRONT_OUT)]))
    zs = refs[14 + len(_FRONT_OUT)]

    i = pl.program_id(1)
    modulate = lambda xx: _ln(xx) * (1.0 + sc_ref[0]) + sh_ref[0]
    hc = modulate(x_ref[0]).astype(BF16)
    row = _iota((tr, 1), 0) + i * tr
    if grid_shift:
        keep_p = jnp.where(i > 0, 1.0, 0.0)
        keep_n = jnp.where(i < t_total // tr - 1, 1.0, 0.0)
        hall = jnp.concatenate([(modulate(xp_ref[0]) * keep_p).astype(BF16), hc,
                                (modulate(xn_ref[0]) * keep_n).astype(BF16)], axis=0)
        zall = jnp.dot(hall, w_ref[:, 0:RW_PAD], preferred_element_type=F32)
        zc = zall[GRID_W:GRID_W + tr]
        up = zall[0:tr]
        down = zall[2 * GRID_W:2 * GRID_W + tr]
    else:
        zc = jnp.dot(hc, w_ref[:, 0:RW_PAD], preferred_element_type=F32)
    zg = jnp.dot(hc, w_ref[:, RW_PAD:IN_PAD], preferred_element_type=F32)
    zs[0:8, :] = jnp.zeros((8, RW_PAD), F32)
    zs[8 + tr:16 + tr, :] = jnp.zeros((8, RW_PAD), F32)
    zs[8:8 + tr, :] = zc
    left = zs[7:7 + tr, :]
    right = zs[9:9 + tr, :]
    if grid_shift:
        col = row & (GRID_W - 1)
        left = jnp.where(col > 0, left, 0.0)
        right = jnp.where(col < GRID_W - 1, right, 0.0)
        nb = 0.25 * (up + down + left + right)
    else:
        left = jnp.where(row > 0, left, 0.0)
        right = jnp.where(row < t_total - 1, right, 0.0)
        nb = 0.5 * (left + right)
    zr = zc + (nb - zc) * mu_ref[...]

    r = zr[:, COL_R:COL_R + 512]
    k = zr[:, COL_K:COL_K + 512]
    v = zr[:, COL_V:COL_V + 512]
    lo = zr[:, COL_LORA:COL_LORA + 128]
    gd = zr[:, COL_GD:COL_GD + 128]
    seg = seg_ref[...]

    lane = _iota((1, 128), 1)
    lor = _dot(jnp.where(lane < 64, jnp.tanh(lo), lo), wl_ref[...])
    kkr = k * kk_p[...]
    kk = kkr * lax.rsqrt(_dot(kkr * kkr, seg) + 1e-12)
    outs["r"][0] = r.astype(BF16)
    outs["kk"][0] = kk.astype(BF16)
    outs["v"][0] = v.astype(BF16)
    outs["gout"][0] = _dot(_sigmoid(gd), g2_ref[...]).astype(BF16)
    kh_sum = None
    for d, sfx in ((0, "_f"), (1, "_b")):
        u = lor[:, d * 512:(d + 1) * 512] + w0_ref[d:d + 1, :]
        outs["lw" + sfx][0] = (-LOG_DECAY_SCALE) * _sigmoid(u)
        a = _sigmoid(a0_ref[d:d + 1, :] + lor[:, 1024 + d * 512:1536 + d * 512])
        kh = k * (1.0 + (a - 1.0) * ka_p[...])
        outs["kh" + sfx][0] = kh.astype(BF16)
        outs["kka" + sfx][0] = (kk * a).astype(BF16)
        kh_sum = kh if kh_sum is None else kh_sum + kh
    outs["bonus"][0] = (_dot(r * rk_p[...] * kh_sum, seg) * v).astype(BF16)

    g0 = RW_PAD
    outs["q"][0] = (zg[:, COL_Q - g0:COL_Q - g0 + 256] * (RW_HEAD ** -0.5)).astype(BF16)
    outs["kg"][0] = zg[:, COL_KG - g0:COL_KG - g0 + 256].astype(BF16)
    outs["vg"][0] = zg[:, COL_VG - g0:COL_VG - g0 + 512].astype(BF16)
    gg = zg[:, COL_GG - g0:COL_GG - g0 + 512]
    outs["gate"][0] = (gg * _sigmoid(gg)).astype(BF16)
    gl = _dot(zg[:, COL_GA - g0:COL_GA - g0 + 128], a2_ref[...])
    for d, sfx in ((0, "_f"), (1, "_b")):
        xg = gl[:, d * 256:(d + 1) * 256] + ab_ref[d:d + 1, :]
        outs["la" + sfx][0] = -_softplus(-xg) * (1.0 / GLA_GATE_NORM)


def _front_call(x, sh, sc, grid_shift, pp):
    b, t, d = x.shape
    tr = min(t, 512)
    nt = t // tr
    hb = tr // GRID_W
    nh = t // GRID_W
    full = lambda shape: pl.BlockSpec(shape, lambda i, j: (0,) * len(shape))
    vec = pl.BlockSpec((1, 1, d), lambda i, j: (i, 0, 0))
    in_specs = [pl.BlockSpec((1, tr, d), lambda i, j: (i, j, 0))]
    args = [x]
    if grid_shift:
        in_specs += [
            pl.BlockSpec((1, GRID_W, d), lambda i, j: (i, jnp.maximum(j * hb - 1, 0), 0)),
            pl.BlockSpec((1, GRID_W, d), lambda i, j: (i, jnp.minimum((j + 1) * hb, nh - 1), 0)),
        ]
        args += [x, x]
    in_specs += [vec, vec]
    args += [sh, sc]
    for nm in _FRONT_PARAMS:
        in_specs.append(full(pp[nm].shape))
        args.append(pp[nm])
    out_specs = [pl.BlockSpec((1, tr, w), lambda i, j: (i, j, 0)) for _, w, _ in _FRONT_OUT]
    out_shape = [jax.ShapeDtypeStruct((b, t, w), dt) for _, w, dt in _FRONT_OUT]
    res = pl.pallas_call(
        functools.partial(_front_kernel, grid_shift, t, tr),
        grid=(b, nt),
        in_specs=in_specs,
        out_specs=out_specs,
        out_shape=out_shape,
        scratch_shapes=[pltpu.VMEM((tr + 16, RW_PAD), F32)],
        compiler_params=_params(("arbitrary", "arbitrary")),
        name="front_grid" if grid_shift else "front_seq",
    )(*args)
    return dict(zip([nm for nm, _, _ in _FRONT_OUT], res))


def _bd(x, blk):
    reps = blk.shape[0] // x.shape[0]
    return jnp.where(blk, jnp.concatenate([x] * reps, axis=0), jnp.zeros((), x.dtype))


def _scan_masks(rev, width):
    c = CHUNK
    ri = _iota((c, width), 0)
    jj = _iota((c, width), 1) & (c - 1)
    strict = (jj > ri) if rev else (jj < ri)
    incl = (jj >= ri) if rev else (jj <= ri)
    eye = jj == ri
    a = _iota((c, c), 0)
    bq = _iota((c, c), 1)
    mcum = jnp.where((bq >= a) if rev else (bq <= a), 1.0, 0.0).astype(BF16)
    levels = [strict & ((jj >> (l + 1)) == (ri >> (l + 1))) & ((jj >> l) != (ri >> l)) for l in range(6)]
    return strict, incl, eye, mcum, levels


def _each(f, *lists):
    return [f(*a) for a in zip(*lists)]


def _cumsum(x, mcum):
    hi = x.astype(BF16)
    lo = (x - hi.astype(F32)).astype(BF16)
    return (jnp.dot(mcum, hi, preferred_element_type=F32) + jnp.dot(mcum, lo, preferred_element_type=F32))


def _wkv_chains(need_y, ins, hts, masks, blk):
    c = CHUNK
    revs = [i[0] for i in ins]
    mk = [masks[int(rv)] for rv in revs]
    g = _each(lambda i, m: _cumsum(i[6], m[3]), ins, mk)
    gl = [gg[0:1] if rv else gg[c - 1:c] for gg, rv in zip(g, revs)]
    x2 = _each(lambda i, gg: jnp.concatenate([(i[2] * jnp.exp(gg - i[6])).astype(BF16),
                                              (i[1] * jnp.exp(gg)).astype(BF16)], axis=0), ins, g)
    at = _each(lambda i, gg: (i[3] * jnp.exp(-gg)).astype(BF16), ins, g)
    kt = _each(lambda i, gg: (i[4] * jnp.exp(-gg)).astype(BF16), ins, g)
    sa = _each(lambda x, a: _dot_nt(x, _bd(a, blk)), x2, at)
    sk = _each(lambda x, k: _dot_nt(x, _bd(k, blk)), x2, kt)
    tm = _each(lambda s, m: jnp.where(m[2], 1.0, 0.0) - jnp.where(m[4][0], s[:c], 0.0), sa, mk)
    for l in range(1, 6):
        x = _each(lambda t, s, m: _dot(t, _bd(jnp.where(m[4][l], s[:c], 0.0).astype(BF16), blk)), tm, sa, mk)
        tm = _each(lambda t, xx: t - _dot(xx, _bd(t.astype(BF16), blk)), tm, x)
    xh = _each(_dot_nt, x2, hts)
    vb = [i[5].astype(BF16) for i in ins]
    vbd = [_bd(v, blk) for v in vb]
    lkv = _each(lambda s, m, vd: _dot(jnp.where(m[0], s[:c], 0.0), vd), sk, mk, vbd)
    ub = _each(lambda t, h, lv: (-_dot(t, _bd((h[:c] + lv).astype(BF16), blk))).astype(BF16), tm, xh, lkv)
    upd = _each(lambda u, v, i, gg, gle: _dot_tn(
        jnp.concatenate([u, v], axis=0),
        jnp.concatenate([(i[3] * jnp.exp(gle - gg)).astype(BF16), (i[4] * jnp.exp(gle - gg)).astype(BF16)], axis=0)),
        ub, vb, ins, g, gl)
    ht_new = _each(lambda h, gle, up: h * jnp.exp(gle) + jnp.where(blk, up, 0.0), hts, gl, upd)
    ys = [None] * len(ins)
    if need_y:
        ya = _each(lambda s, m, u: _dot(jnp.where(m[1], s[c:], 0.0), _bd(u, blk)), sa, mk, ub)
        yk = _each(lambda s, m, vd: _dot(jnp.where(m[1], s[c:], 0.0), vd), sk, mk, vbd)
        ys = _each(lambda h, a, k: h[c:] + a + k, xh, ya, yk)
    return ys, ht_new


SCAN_TB = 256
WKV_NB = 2
GLA_NB = 4


def _wkv_kernel(need_y, nb, tb, *refs):
    fwd, bwd, h0_ref, rest = refs[:6], refs[6:12], refs[12], refs[13:]
    if need_y:
        yf_ref, yb_ref, hout_ref, hs = rest
    else:
        hout_ref, hs = rest
    c = CHUNK
    nc = tb // c
    w = 256
    j = pl.program_id(1)
    blk = (_iota((w, w), 0) >> 6) == (_iota((w, w), 1) >> 6)
    masks = (_scan_masks(False, w), _scan_masks(True, w))

    @pl.when(j == 0)
    def _():
        hs[...] = h0_ref[...]

    def body(ci, carry):
        where, ins, hts = [], [], []
        for ib in range(nb):
            for g in range(RW_WIDTH // w):
                lanes = slice(g * w, (g + 1) * w)
                for d, views in enumerate((fwd, bwd)):
                    cc = ci if d == 0 else nc - 1 - ci
                    rows = pl.ds(pl.multiple_of(cc * c, c), c)
                    r, kk, v, kh, kka = [ref[ib, rows, lanes].astype(F32) for ref in views[:5]]
                    ins.append((d == 1, r, kk, kka, kh, v, views[5][ib, rows, lanes]))
                    hts.append(hs[ib, g, d])
                    where.append((ib, g, d, rows, lanes))
        ys, hts = _wkv_chains(need_y, ins, hts, masks, blk)
        for (ib, g, d, rows, lanes), y, ht in zip(where, ys, hts):
            hs[ib, g, d] = ht
            if need_y:
                y_ref = yb_ref if d else yf_ref
                y_ref[ib, rows, lanes] = y.astype(BF16)
        return carry

    lax.fori_loop(0, nc, body, 0)

    @pl.when(j == pl.num_programs(1) - 1)
    def _():
        hout_ref[...] = hs[...]


def _wkv_call(p, h0, need_y):
    b, t, _ = p["r"].shape
    nb = min(WKV_NB, b)
    tb = min(SCAN_TB, t)
    nblk = t // tb
    ng = RW_WIDTH // 256
    fwd = pl.BlockSpec((nb, tb, RW_WIDTH), lambda i, j: (i, j, 0))
    bwd = pl.BlockSpec((nb, tb, RW_WIDTH), lambda i, j: (i, nblk - 1 - j, 0))
    st = pl.BlockSpec((nb, ng, 2, 256, 256), lambda i, j: (i, 0, 0, 0, 0))
    out_specs = [st]
    out_shape = [jax.ShapeDtypeStruct((b, ng, 2, 256, 256), F32)]
    if need_y:
        out_specs = [fwd, bwd] + out_specs
        out_shape = [jax.ShapeDtypeStruct((b, t, RW_WIDTH), BF16)] * 2 + out_shape
    res = pl.pallas_call(
        functools.partial(_wkv_kernel, need_y, nb, tb),
        grid=(b // nb, nblk),
        in_specs=[fwd] * 6 + [bwd] * 6 + [st],
        out_specs=out_specs,
        out_shape=out_shape,
        scratch_shapes=[pltpu.VMEM((nb, ng, 2, 256, 256), F32)],
        compiler_params=_params(("arbitrary", "arbitrary")),
        name="wkv_y" if need_y else "wkv_state",
    )(p["r"], p["kk"], p["v"], p["kh_f"], p["kka_f"], p["lw_f"],
      p["r"], p["kk"], p["v"], p["kh_b"], p["kka_b"], p["lw_b"], h0)
    return res if need_y else (None, None, res[0])


def _gla_chains(need_o, ins, sts, masks, blk_k, blk_v, blk_s):
    c = CHUNK
    revs = [i[0] for i in ins]
    mk = [masks[int(rv)] for rv in revs]
    b = _each(lambda i, m: _cumsum(i[4], m[3]), ins, mk)
    bl = [bb[0:1] if rv else bb[c - 1:c] for bb, rv in zip(b, revs)]
    vb = [i[3].astype(BF16) for i in ins]
    upd = _each(lambda v, i, bb, ble: _dot_tn(v, (i[2] * jnp.exp(ble - bb)).astype(BF16)), vb, ins, b, bl)
    st_new = _each(lambda s, ble, up: s * jnp.exp(ble) + jnp.where(blk_s, up, 0.0), sts, bl, upd)
    os_ = [None] * len(ins)
    if need_o:
        q_in = _each(lambda i, bb: (i[1] * jnp.exp(bb)).astype(BF16), ins, b)
        sc = _each(lambda q, i, bb, m: jnp.where(
            m[1], _dot_nt(q, _bd((i[2] * jnp.exp(-bb)).astype(BF16), blk_k)), 0.0), q_in, ins, b, mk)
        oi = _each(lambda s, v: _dot(s, _bd(v, blk_v)), sc, vb)
        ox = _each(_dot_nt, q_in, sts)
        os_ = _each(lambda a, x: a + x, oi, ox)
    return os_, st_new


def _gla_kernel(need_o, nb, tb, *refs):
    fwd, bwd, s0_ref, rest = refs[:4], refs[4:8], refs[8], refs[9:]
    if need_o:
        of_ref, ob_ref, sout_ref, ss = rest
    else:
        sout_ref, ss = rest
    c = CHUNK
    nc = tb // c
    j = pl.program_id(1)
    blk_k = (_iota((256, 256), 0) >> 6) == (_iota((256, 256), 1) >> 6)
    blk_v = (_iota((256, 512), 0) >> 6) == (_iota((256, 512), 1) >> 7)
    blk_s = (_iota((512, 256), 0) >> 7) == (_iota((512, 256), 1) >> 6)
    masks = (_scan_masks(False, 256), _scan_masks(True, 256))

    @pl.when(j == 0)
    def _():
        ss[...] = s0_ref[...]

    def body(ci, carry):
        where, ins, sts = [], [], []
        for ib in range(nb):
            for d, views in enumerate((fwd, bwd)):
                cc = ci if d == 0 else nc - 1 - ci
                rows = pl.ds(pl.multiple_of(cc * c, c), c)
                q, k, v = [ref[ib, rows, :].astype(F32) for ref in views[:3]]
                ins.append((d == 1, q, k, v, views[3][ib, rows, :]))
                sts.append(ss[ib, d])
                where.append((ib, d, rows))
        os_, sts = _gla_chains(need_o, ins, sts, masks, blk_k, blk_v, blk_s)
        for (ib, d, rows), o, st in zip(where, os_, sts):
            ss[ib, d] = st
            if need_o:
                o_ref = ob_ref if d else of_ref
                o_ref[ib, rows, :] = o.astype(BF16)
        return carry

    lax.fori_loop(0, nc, body, 0)

    @pl.when(j == pl.num_programs(1) - 1)
    def _():
        sout_ref[...] = ss[...]


def _gla_call(p, s0, need_o):
    b, t, _ = p["q"].shape
    nb = min(GLA_NB, b)
    tb = min(SCAN_TB, t)
    nblk = t // tb
    fwd = lambda w: pl.BlockSpec((nb, tb, w), lambda i, j: (i, j, 0))
    bwd = lambda w: pl.BlockSpec((nb, tb, w), lambda i, j: (i, nblk - 1 - j, 0))
    st = pl.BlockSpec((nb, 2, GLA_VAL, GLA_KEY), lambda i, j: (i, 0, 0, 0))
    out_specs = [st]
    out_shape = [jax.ShapeDtypeStruct((b, 2, GLA_VAL, GLA_KEY), F32)]
    if need_o:
        out_specs = [fwd(GLA_VAL), bwd(GLA_VAL)] + out_specs
        out_shape = [jax.ShapeDtypeStruct((b, t, GLA_VAL), BF16)] * 2 + out_shape
    res = pl.pallas_call(
        functools.partial(_gla_kernel, need_o, nb, tb),
        grid=(b // nb, nblk),
        in_specs=[fwd(256), fwd(256), fwd(512), fwd(256), bwd(256), bwd(256), bwd(512), bwd(256), st],
        out_specs=out_specs,
        out_shape=out_shape,
        scratch_shapes=[pltpu.VMEM((nb, 2, GLA_VAL, GLA_KEY), F32)],
        compiler_params=_params(("arbitrary", "arbitrary")),
        name="gla_o" if need_o else "gla_state",
    )(p["q"], p["kg"], p["vg"], p["la_f"], p["q"], p["kg"], p["vg"], p["la_b"], s0)
    return res if need_o else (None, None, res[0])


def _post_kernel(yf_ref, yb_ref, bon_ref, gout_ref, of_ref, ob_ref, gate_ref, x_ref, g1_ref, gnw_ref,
                 gnb_ref, nw_ref, wout_ref, l1g_ref, l1b_ref, seg64_ref, seg128_ref, out_ref):
    y = yf_ref[0].astype(F32) + yb_ref[0].astype(F32)
    seg64 = seg64_ref[...]
    mu = _dot(y, seg64) * (1.0 / RW_HEAD)
    yc = y - mu
    var = _dot(yc * yc, seg64) * (1.0 / RW_HEAD)
    y_n = yc * lax.rsqrt(var + RW_GN_EPS) * gnw_ref[...] + gnb_ref[...]
    rw_out = (y_n + bon_ref[0].astype(F32)) * gout_ref[0].astype(F32)
    o = of_ref[0].astype(F32) + ob_ref[0].astype(F32)
    ms = _dot(o * o, seg128_ref[...]) * (1.0 / GLA_DV)
    gla_out = o * lax.rsqrt(ms + GLA_NORM_EPS) * nw_ref[...] * gate_ref[0].astype(F32)
    mix = jnp.concatenate([rw_out.astype(BF16), gla_out.astype(BF16)], axis=-1)
    proj = jnp.dot(mix, wout_ref[...], preferred_element_type=F32)
    xs = DN_ALPHA * x_ref[0] + g1_ref[0] * proj
    out_ref[0] = _ln(xs) * l1g_ref[...] + l1b_ref[...]


def _post_call(yf, yb, bonus, gout, of, ob, gate, x, g1, pp):
    b, t, d = x.shape
    tm = min(t, 512)
    tile = lambda w: pl.BlockSpec((1, tm, w), lambda i, j: (i, j, 0))
    full = lambda a: pl.BlockSpec(a.shape, lambda i, j: (0,) * a.ndim)
    consts = [pp["gn_w"], pp["gn_b"], pp["gla_nw"], pp["w_out"], pp["ln1_g"], pp["ln1_b"],
              pp["seg64"], pp["seg128"]]
    return pl.pallas_call(
        _post_kernel,
        grid=(b, t // tm),
        in_specs=[tile(512)] * 7 + [tile(d), pl.BlockSpec((1, 1, d), lambda i, j: (i, 0, 0))]
        + [full(a) for a in consts],
        out_specs=tile(d),
        out_shape=jax.ShapeDtypeStruct((b, t, d), F32),
        compiler_params=_params(("arbitrary", "arbitrary")),
        name="post",
    )(yf, yb, bonus, gout, of, ob, gate, x, g1, *consts)


def _prefix_excl(m, su):
    e, t = m.shape
    off = jnp.zeros((e, 1), F32)
    parts = []
    for j in range(t // 128):
        blk = m[:, j * 128:(j + 1) * 128]
        parts.append(_dot(blk, su) + off)
        off = off + jnp.sum(blk, axis=1, keepdims=True)
    return jnp.concatenate(parts, axis=1)


def _route_kernel(cap, x_ref, sh_ref, sc_ref, rw_ref, xg_ref, aff_ref, pos_ref, h2_s, pe_s):
    e = pl.program_id(1)
    t = x_ref.shape[1]

    @pl.when(e == 0)
    def _():
        h2 = _ln(x_ref[0]) * (1.0 + sc_ref[0]) + sh_ref[0]
        h2_s[...] = h2.astype(BF16)
        mm = lambda p, q: jnp.dot(p, q, preferred_element_type=F32)
        h_hi = h2.astype(BF16)
        h_lo = (h2 - h_hi.astype(F32)).astype(BF16)
        w_hi = rw_ref[...].astype(BF16)
        w_lo = (rw_ref[...] - w_hi.astype(F32)).astype(BF16)
        logits = mm(h_hi, w_hi) + (mm(h_lo, w_hi) + mm(h_hi, w_lo))
        lane = _iota((1, 128), 1)
        logits = jnp.where(lane < N_EXPERTS, logits, -1e30)
        m = jnp.max(logits, axis=-1, keepdims=True)
        ex = jnp.exp(logits - m)
        aff = ex / jnp.sum(ex, axis=-1, keepdims=True)
        aff_ref[0] = aff
        aff_t = aff.T[0:N_EXPERTS, :]
        def bs(_, lohi):
            lo, hi = lohi
            mid = lo + ((hi - lo) >> 1)
            mid_f = pltpu.bitcast(mid, F32)[:, 0:1]
            cnt = jnp.sum(jnp.where(aff_t >= mid_f, 1.0, 0.0), axis=1, keepdims=True)
            ok = cnt >= cap
            return jnp.where(ok, mid, lo), jnp.where(ok, hi, mid)

        lo0 = jnp.zeros((N_EXPERTS, 128), jnp.int32)
        hi0 = jnp.full((N_EXPERTS, 128), 0x3F800001, jnp.int32)
        thr_bits, _ = lax.fori_loop(0, 31, bs, (lo0, hi0))
        thr = pltpu.bitcast(thr_bits, F32)[:, 0:1]
        gt = jnp.where(aff_t > thr, 1.0, 0.0)
        eq = jnp.where(aff_t == thr, 1.0, 0.0)
        need = cap - jnp.sum(gt, axis=1, keepdims=True)
        su = jnp.where(_iota((128, 128), 0) < _iota((128, 128), 1), 1.0, 0.0).astype(BF16)
        sel = gt + eq * jnp.where(_prefix_excl(eq, su) < need, 1.0, 0.0)
        pos = jnp.where(sel > 0.0, _prefix_excl(sel, su), -1.0)
        pe_s[0:N_EXPERTS, :] = pos
        pe_s[N_EXPERTS:, :] = jnp.full((128 - N_EXPERTS, t), -1.0, F32)
        pos_ref[0] = pe_s[...].T

    prow = pe_s[pl.ds(e, 1), :]
    slot = _iota((cap, t), 0).astype(F32)
    onehot = jnp.where(prow == slot, 1.0, 0.0).astype(BF16)
    xg_ref[0, 0] = jnp.dot(onehot, h2_s[...], preferred_element_type=F32).astype(BF16)


def _route_call(x1, sh2, sc2, router_pad, cap):
    b, t, d = x1.shape
    return pl.pallas_call(
        functools.partial(_route_kernel, cap),
        grid=(b, N_EXPERTS),
        in_specs=[pl.BlockSpec((1, t, d), lambda i, e: (i, 0, 0)),
                  pl.BlockSpec((1, 1, d), lambda i, e: (i, 0, 0)),
                  pl.BlockSpec((1, 1, d), lambda i, e: (i, 0, 0)),
                  pl.BlockSpec((d, 128), lambda i, e: (0, 0))],
        out_specs=[pl.BlockSpec((1, 1, cap, d), lambda i, e: (i, e, 0, 0)),
                   pl.BlockSpec((1, t, 128), lambda i, e: (i, 0, 0)),
                   pl.BlockSpec((1, t, 128), lambda i, e: (i, 0, 0))],
        out_shape=[jax.ShapeDtypeStruct((b, N_EXPERTS, cap, d), BF16),
                   jax.ShapeDtypeStruct((b, t, 128), F32),
                   jax.ShapeDtypeStruct((b, t, 128), F32)],
        scratch_shapes=[pltpu.VMEM((t, d), BF16), pltpu.VMEM((128, t), F32)],
        compiler_params=_params(("arbitrary", "arbitrary")),
        name="route",
    )(x1, sh2, sc2, router_pad)


def _ffn_kernel(nb, x_ref, wg_hbm, wu_hbm, wd_hbm, y_ref, wg_s, wu_s, wd_s, sg, su, sd, sems):
    e = pl.program_id(0)
    b = pl.program_id(1)
    ne = pl.num_programs(0)
    rg = wg_s.shape[1] // nb
    rd = wd_s.shape[1] // nb

    def chunk_copies(ee, cc, slot):
        og = pl.multiple_of(cc * rg, rg)
        od = pl.multiple_of(cc * rd, rd)
        return (pltpu.make_async_copy(wg_hbm.at[ee, pl.ds(og, rg), :], sg.at[slot], sems.at[slot, 0]),
                pltpu.make_async_copy(wu_hbm.at[ee, pl.ds(og, rg), :], su.at[slot], sems.at[slot, 1]),
                pltpu.make_async_copy(wd_hbm.at[ee, pl.ds(od, rd), :], sd.at[slot], sems.at[slot, 2]))

    def cast_chunk(wslot, cc, slot):
        og = pl.multiple_of(cc * rg, rg)
        od = pl.multiple_of(cc * rd, rd)
        wg_s[wslot, pl.ds(og, rg), :] = sg[slot].astype(BF16)
        wu_s[wslot, pl.ds(og, rg), :] = su[slot].astype(BF16)
        wd_s[wslot, pl.ds(od, rd), :] = sd[slot].astype(BF16)

    s = e * nb + b
    slot_now = s % 2

    @pl.when(s == 0)
    def _():
        for cc in range(nb):
            cps = chunk_copies(0, cc, cc % 2)
            for cp in cps:
                cp.start()
            for cp in cps:
                cp.wait()
            cast_chunk(0, cc, cc % 2)

        @pl.when(ne > 1)
        def _():
            for cp in chunk_copies(1, 0, 0):
                cp.start()

    @pl.when(e + 1 < ne)
    def _():
        for cp in chunk_copies(e + 1, b, slot_now):
            cp.wait()
        cast_chunk((e + 1) % 2, b, slot_now)

    last_chunk = b + 1 == nb
    has_next = jnp.where(last_chunk, e + 2 < ne, e + 1 < ne)

    @pl.when(has_next)
    def _():
        for cp in chunk_copies(jnp.where(last_chunk, e + 2, e + 1), jnp.where(last_chunk, 0, b + 1), 1 - slot_now):
            cp.start()

    wslot = e % 2
    x = x_ref[0, 0]
    g = jnp.dot(x, wg_s[wslot], preferred_element_type=F32)
    u = jnp.dot(x, wu_s[wslot], preferred_element_type=F32)
    h = (g * _sigmoid(g) * u).astype(BF16)
    y_ref[0, 0] = jnp.dot(h, wd_s[wslot], preferred_element_type=F32).astype(BF16)


def _ffn_call(xg, wg, wu, wd):
    b, ne, cap, d = xg.shape
    f = wg.shape[2]
    assert d % (16 * b) == 0 and f % (16 * b) == 0, "weight chunks must be whole bf16 sublane tiles"
    assert b % 2 == 0, "staging slots alternate with the grid step; an expert must span an even number of steps"
    hbm = pl.BlockSpec(memory_space=pl.ANY)
    return pl.pallas_call(
        functools.partial(_ffn_kernel, b),
        grid=(ne, b),
        in_specs=[pl.BlockSpec((1, 1, cap, d), lambda e, i: (i, e, 0, 0)), hbm, hbm, hbm],
        out_specs=pl.BlockSpec((1, 1, cap, d), lambda e, i: (i, e, 0, 0)),
        out_shape=jax.ShapeDtypeStruct((b, ne, cap, d), BF16),
        scratch_shapes=[pltpu.VMEM((2, d, f), BF16), pltpu.VMEM((2, d, f), BF16), pltpu.VMEM((2, f, d), BF16),
                        pltpu.VMEM((2, d // b, f), F32), pltpu.VMEM((2, d // b, f), F32),
                        pltpu.VMEM((2, f // b, d), F32), pltpu.SemaphoreType.DMA((2, 3))],
        compiler_params=_params(("arbitrary", "arbitrary")),
        name="ffn",
    )(xg, wg, wu, wd)


def _combine_kernel(cap, y_ref, pos_ref, aff_ref, x_ref, g2_ref, l2g_ref, l2b_ref, out_ref):
    pos = pos_ref[0]
    gates = aff_ref[0]
    slot = _iota((1, cap), 1).astype(F32)
    acc = None
    for e in range(N_EXPERTS):
        oh = jnp.where(pos[:, e:e + 1] == slot, 1.0, 0.0).astype(BF16)
        part = jnp.dot(oh, y_ref[0, e], preferred_element_type=F32) * gates[:, e:e + 1]
        acc = part if acc is None else acc + part
    xs = DN_ALPHA * x_ref[0] + g2_ref[0] * acc
    out_ref[0] = _ln(xs) * l2g_ref[...] + l2b_ref[...]


def _combine_call(y, pos_t, aff_t, x1, g2, l2g, l2b, cap):
    b, t, d = x1.shape
    tm = min(t, 512)
    return pl.pallas_call(
        functools.partial(_combine_kernel, cap),
        grid=(b, t // tm),
        in_specs=[pl.BlockSpec((1, N_EXPERTS, cap, d), lambda i, j: (i, 0, 0, 0)),
                  pl.BlockSpec((1, tm, 128), lambda i, j: (i, j, 0)),
                  pl.BlockSpec((1, tm, 128), lambda i, j: (i, j, 0)),
                  pl.BlockSpec((1, tm, d), lambda i, j: (i, j, 0)),
                  pl.BlockSpec((1, 1, d), lambda i, j: (i, 0, 0)),
                  pl.BlockSpec((1, d), lambda i, j: (0, 0)),
                  pl.BlockSpec((1, d), lambda i, j: (0, 0))],
        out_specs=pl.BlockSpec((1, tm, d), lambda i, j: (i, j, 0)),
        out_shape=jax.ShapeDtypeStruct((b, t, d), F32),
        compiler_params=_params(("arbitrary", "arbitrary")),
        name="combine",
    )(y, pos_t, aff_t, x1, g2, l2g, l2b)


def _pad_cols(parts, total):
    rows = parts[0][0].shape[0]
    out = jnp.zeros((rows, total), parts[0][0].dtype)
    for a, off in parts:
        out = lax.dynamic_update_slice(out, a, (0, off))
    return out


def _block_ones(n, width):
    i = jnp.arange(n) // width
    return (i[:, None] == i[None, :]).astype(BF16)


def _layout_params(w_in, rw_mu, rw_w0, rw_w2, rw_a0, rw_a2, rw_g2, rw_k_k, rw_k_a, rw_r_k,
                   rw_gn_w, rw_gn_b, gla_a2, gla_a_b, gla_norm_w, w_out, ln1_g, ln1_b):
    rw_in = 1760
    segs = [(0, 1536, COL_R), (1536, 1664, COL_LORA), (1664, 1760, COL_GD),
            (rw_in, rw_in + 256, COL_Q), (rw_in + 256, rw_in + 512, COL_KG),
            (rw_in + 512, rw_in + 1024, COL_VG), (rw_in + 1024, rw_in + 1536, COL_GG),
            (rw_in + 1536, rw_in + 1568, COL_GA)]
    w_pad = _pad_cols([(w_in[:, a:b], off) for a, b, off in segs], IN_PAD).astype(BF16)
    mu = _pad_cols([(rw_mu[None, a:b], off) for a, b, off in segs[:3]], RW_PAD)
    w_lora = jnp.zeros((128, 2048), F32)
    for i, m in enumerate((rw_w2[0], rw_w2[1], rw_a2[0], rw_a2[1])):
        w_lora = lax.dynamic_update_slice(w_lora, m, (32 * i, 512 * i))
    g2 = jnp.zeros((128, 512), F32).at[:96].set(rw_g2)
    a2 = jnp.zeros((128, 512), F32).at[0:16, 0:256].set(gla_a2[0]).at[16:32, 256:512].set(gla_a2[1])
    row = lambda a: a.reshape(1, -1)
    return dict(
        w_in=w_pad, mu=mu, k_k=row(rw_k_k), k_a=row(rw_k_a), r_k=row(rw_r_k), w0=rw_w0, a0=rw_a0,
        ab=gla_a_b, w_lora=w_lora.astype(BF16), g2=g2.astype(BF16), a2=a2.astype(BF16),
        seg64=_block_ones(512, 64), seg128=_block_ones(512, 128),
        gn_w=row(rw_gn_w), gn_b=row(rw_gn_b), gla_nw=row(gla_norm_w), w_out=w_out.astype(BF16),
        ln1_g=row(ln1_g), ln1_b=row(ln1_b))


def kernel(x, c, ctx, c_ctx, ada_w, ada_b, w_in, rw_mu, rw_w0, rw_w2, rw_a0, rw_a2, rw_g2, rw_k_k, rw_k_a, rw_r_k, rw_gn_w, rw_gn_b, gla_a2, gla_a_b, gla_norm_w, w_out, ln1_g, ln1_b, router_w, ex_gate, ex_up, ex_down, ln2_g, ln2_b):
    assert ada_w.shape[0] == 1, "single-layer block"
    b, t, d = x.shape
    cap = CAPACITY_FACTOR * t // N_EXPERTS
    pp = _layout_params(w_in[0], rw_mu[0], rw_w0[0], rw_w2[0], rw_a0[0], rw_a2[0], rw_g2[0],
                        rw_k_k[0], rw_k_a[0], rw_r_k[0], rw_gn_w[0], rw_gn_b[0], gla_a2[0],
                        gla_a_b[0], gla_norm_w[0], w_out[0], ln1_g[0], ln1_b[0])

    rows = -(-(b + 1) // 8) * 8
    cs = jnp.zeros((rows, d), F32).at[:b].set(c).at[b].set(c_ctx)
    mod = _mod_call(cs, ada_w[0], ada_b[0][None])
    sh1, sc1, g1, sh2, sc2, g2 = [m[:, None, :] for m in jnp.split(mod[:b], 6, axis=-1)]
    sh1c, sc1c = [jnp.broadcast_to(m[None, None, :], (b, 1, d)) for m in jnp.split(mod[b], 6)[:2]]

    pc = _front_call(ctx, sh1c, sc1c, False, pp)
    zero_h = jnp.zeros((b, RW_WIDTH // 256, 2, 256, 256), F32)
    zero_s = jnp.zeros((b, 2, GLA_VAL, GLA_KEY), F32)
    _, _, h_ctx = _wkv_call(pc, zero_h, False)
    _, _, s_ctx = _gla_call(pc, zero_s, False)

    pz = _front_call(x, sh1, sc1, True, pp)
    yf, yb, _ = _wkv_call(pz, h_ctx, True)
    of, ob, _ = _gla_call(pz, s_ctx, True)
    x1 = _post_call(yf, yb, pz["bonus"], pz["gout"], of, ob, pz["gate"], x, g1, pp)

    router_pad = jnp.zeros((d, 128), F32).at[:, :N_EXPERTS].set(router_w[0])
    xg, aff_t, pos_t = _route_call(x1, sh2, sc2, router_pad, cap)
    ye = _ffn_call(xg, ex_gate[0], ex_up[0], ex_down[0])
    return _combine_call(ye, pos_t, aff_t, x1, g2, ln2_g[0][None], ln2_b[0][None], cap)
```

```python
import functools

import jax
import jax.numpy as jnp
from jax import lax
from jax.experimental import pallas as pl
from jax.experimental.pallas import tpu as pltpu

F32 = jnp.float32
BF16 = jnp.bfloat16
HIGHEST = lax.Precision.HIGHEST

D_MODEL = 1024
GRID_W = 64
RW_WIDTH = 512
RW_HEAD = 64
RW_GN_EPS = 64e-5
GLA_KEY = 256
GLA_VAL = 512
GLA_DV = 128
GLA_GATE_NORM = 16.0
GLA_NORM_EPS = 1e-5
N_EXPERTS = 16
CAPACITY_FACTOR = 2
DN_ALPHA = 2.0 ** 0.25
LN_EPS = 1e-5
CHUNK = 64
LOG_DECAY_SCALE = 0.6065306597126334

COL_R, COL_K, COL_V, COL_LORA, COL_GD = 0, 512, 1024, 1536, 1664
RW_PAD = 1792
COL_Q, COL_KG, COL_VG, COL_GG, COL_GA = 1792, 2048, 2304, 2816, 3328
IN_PAD = 3456

VMEM_LIMIT = 56 * 1024 * 1024


def _params(sem):
    return pltpu.CompilerParams(dimension_semantics=sem, vmem_limit_bytes=VMEM_LIMIT)


def _dot(a, b):
    return jnp.dot(a.astype(BF16), b.astype(BF16), preferred_element_type=F32)


def _dot_nt(a, b):
    return lax.dot_general(a.astype(BF16), b.astype(BF16), (((1,), (1,)), ((), ())),
                           preferred_element_type=F32)


def _dot_tn(a, b):
    return lax.dot_general(a.astype(BF16), b.astype(BF16), (((0,), (0,)), ((), ())),
                           preferred_element_type=F32)


def _sigmoid(x):
    return 1.0 / (1.0 + jnp.exp(-x))


def _softplus(x):
    return jnp.maximum(x, 0.0) + jnp.log(1.0 + jnp.exp(-jnp.abs(x)))


def _ln(x):
    mu = jnp.mean(x, axis=-1, keepdims=True)
    xc = x - mu
    var = jnp.mean(xc * xc, axis=-1, keepdims=True)
    return xc * lax.rsqrt(var + LN_EPS)


def _iota(shape, dim):
    return lax.broadcasted_iota(jnp.int32, shape, dim)


def _mod_kernel(c_ref, w_ref, b_ref, o_ref):
    c = c_ref[...]
    s = c * _sigmoid(c)
    o_ref[...] = jnp.dot(s, w_ref[...], precision=HIGHEST, preferred_element_type=F32) + b_ref[...]


def _mod_call(cs, ada_w, ada_b):
    rows, d = cs.shape
    n = ada_w.shape[1]
    bn = 1536
    return pl.pallas_call(
        _mod_kernel,
        grid=(n // bn,),
        in_specs=[pl.BlockSpec((rows, d), lambda j: (0, 0)),
                  pl.BlockSpec((d, bn), lambda j: (0, j)),
                  pl.BlockSpec((1, bn), lambda j: (0, j))],
        out_specs=pl.BlockSpec((rows, bn), lambda j: (0, j)),
        out_shape=jax.ShapeDtypeStruct((rows, n), F32),
        compiler_params=_params(("arbitrary",)),
        name="mod",
    )(cs, ada_w, ada_b)


_FRONT_OUT = (
    ("r", 512, BF16), ("kk", 512, BF16), ("v", 512, BF16),
    ("kh_f", 512, BF16), ("kh_b", 512, BF16), ("kka_f", 512, BF16), ("kka_b", 512, BF16),
    ("lw_f", 512, F32), ("lw_b", 512, F32), ("gout", 512, BF16), ("bonus", 512, BF16),
    ("q", 256, BF16), ("kg", 256, BF16), ("vg", 512, BF16),
    ("la_f", 256, F32), ("la_b", 256, F32), ("gate", 512, BF16),
)
_FRONT_PARAMS = ("w_in", "mu", "k_k", "k_a", "r_k", "w0", "a0", "ab", "w_lora", "g2", "a2", "seg64")


def _front_kernel(grid_shift, t_total, tr, *refs):
    if grid_shift:
        x_ref, xp_ref, xn_ref = refs[:3]
        refs = refs[3:]
    else:
        x_ref = refs[0]
        refs = refs[1:]
    sh_ref, sc_ref = refs[:2]
    (w_ref, mu_ref, kk_p, ka_p, rk_p, w0_ref, a0_ref, ab_ref, wl_ref, g2_ref, a2_ref, seg_ref) = refs[2:14]
    outs = dict(zip([n for n, _, _ in _FRONT_OUT], refs[14:14 + len(_FRONT_OUT)]))
    zs = refs[14 + len(_FRONT_OUT)]

    i = pl.program_id(1)
    modulate = lambda xx: _ln(xx) * (1.0 + sc_ref[0]) + sh_ref[0]
    hc = modulate(x_ref[0]).astype(BF16)
    row = _iota((tr, 1), 0) + i * tr
    if grid_shift:
        keep_p = jnp.where(i > 0, 1.0, 0.0)
        keep_n = jnp.where(i < t_total // tr - 1, 1.0, 0.0)
        hall = jnp.concatenate([(modulate(xp_ref[0]) * keep_p).astype(BF16), hc,
                                (modulate(xn_ref[0]) * keep_n).astype(BF16)], axis=0)
        zall = jnp.dot(hall, w_ref[:, 0:RW_PAD], preferred_element_type=F32)
        zc = zall[GRID_W:GRID_W + tr]
        up = zall[0:tr]
        down = zall[2 * GRID_W:2 * GRID_W + tr]
    else:
        zc = jnp.dot(hc, w_ref[:, 0:RW_PAD], preferred_element_type=F32)
    zg = jnp.dot(hc, w_ref[:, RW_PAD:IN_PAD], preferred_element_type=F32)
    zs[0:8, :] = jnp.zeros((8, RW_PAD), F32)
    zs[8 + tr:16 + tr, :] = jnp.zeros((8, RW_PAD), F32)
    zs[8:8 + tr, :] = zc
    left = zs[7:7 + tr, :]
    right = zs[9:9 + tr, :]
    if grid_shift:
        col = row & (GRID_W - 1)
        left = jnp.where(col > 0, left, 0.0)
        right = jnp.where(col < GRID_W - 1, right, 0.0)
        nb = 0.25 * (up + down + left + right)
    else:
        left = jnp.where(row > 0, left, 0.0)
        right = jnp.where(row < t_total - 1, right, 0.0)
        nb = 0.5 * (left + right)
    zr = zc + (nb - zc) * mu_ref[...]

    r = zr[:, COL_R:COL_R + 512]
    k = zr[:, COL_K:COL_K + 512]
    v = zr[:, COL_V:COL_V + 512]
    lo = zr[:, COL_LORA:COL_LORA + 128]
    gd = zr[:, COL_GD:COL_GD + 128]
    seg = seg_ref[...]

    lane = _iota((1, 128), 1)
    lor = _dot(jnp.where(lane < 64, jnp.tanh(lo), lo), wl_ref[...])
    kkr = k * kk_p[...]
    kk = kkr * lax.rsqrt(_dot(kkr * kkr, seg) + 1e-12)
    outs["r"][0] = r.astype(BF16)
    outs["kk"][0] = kk.astype(BF16)
    outs["v"][0] = v.astype(BF16)
    outs["gout"][0] = _dot(_sigmoid(gd), g2_ref[...]).astype(BF16)
    kh_sum = None
    for d, sfx in ((0, "_f"), (1, "_b")):
        u = lor[:, d * 512:(d + 1) * 512] + w0_ref[d:d + 1, :]
        outs["lw" + sfx][0] = (-LOG_DECAY_SCALE) * _sigmoid(u)
        a = _sigmoid(a0_ref[d:d + 1, :] + lor[:, 1024 + d * 512:1536 + d * 512])
        kh = k * (1.0 + (a - 1.0) * ka_p[...])
        outs["kh" + sfx][0] = kh.astype(BF16)
        outs["kka" + sfx][0] = (kk * a).astype(BF16)
        kh_sum = kh if kh_sum is None else kh_sum + kh
    outs["bonus"][0] = (_dot(r * rk_p[...] * kh_sum, seg) * v).astype(BF16)

    g0 = RW_PAD
    outs["q"][0] = (zg[:, COL_Q - g0:COL_Q - g0 + 256] * (RW_HEAD ** -0.5)).astype(BF16)
    outs["kg"][0] = zg[:, COL_KG - g0:COL_KG - g0 + 256].astype(BF16)
    outs["vg"][0] = zg[:, COL_VG - g0:COL_VG - g0 + 512].astype(BF16)
    gg = zg[:, COL_GG - g0:COL_GG - g0 + 512]
    outs["gate"][0] = (gg * _sigmoid(gg)).astype(BF16)
    gl = _dot(zg[:, COL_GA - g0:COL_GA - g0 + 128], a2_ref[...])
    for d, sfx in ((0, "_f"), (1, "_b")):
        xg = gl[:, d * 256:(d + 1) * 256] + ab_ref[d:d + 1, :]
        outs["la" + sfx][0] = -_softplus(-xg) * (1.0 / GLA_GATE_NORM)


def _front_call(x, sh, sc, grid_shift, pp):
    b, t, d = x.shape
    tr = min(t, 512)
    nt = t // tr
    hb = tr // GRID_W
    nh = t // GRID_W
    full = lambda shape: pl.BlockSpec(shape, lambda i, j: (0,) * len(shape))
    vec = pl.BlockSpec((1, 1, d), lambda i, j: (i, 0, 0))
    in_specs = [pl.BlockSpec((1, tr, d), lambda i, j: (i, j, 0))]
    args = [x]
    if grid_shift:
        in_specs += [
            pl.BlockSpec((1, GRID_W, d), lambda i, j: (i, jnp.maximum(j * hb - 1, 0), 0)),
            pl.BlockSpec((1, GRID_W, d), lambda i, j: (i, jnp.minimum((j + 1) * hb, nh - 1), 0)),
        ]
        args += [x, x]
    in_specs += [vec, vec]
    args += [sh, sc]
    for nm in _FRONT_PARAMS:
        in_specs.append(full(pp[nm].shape))
        args.append(pp[nm])
    out_specs = [pl.BlockSpec((1, tr, w), lambda i, j: (i, j, 0)) for _, w, _ in _FRONT_OUT]
    out_shape = [jax.ShapeDtypeStruct((b, t, w), dt) for _, w, dt in _FRONT_OUT]
    res = pl.pallas_call(
        functools.partial(_front_kernel, grid_shift, t, tr),
        grid=(b, nt),
        in_specs=in_specs,
        out_specs=out_specs,
        out_shape=out_shape,
        scratch_shapes=[pltpu.VMEM((tr + 16, RW_PAD), F32)],
        compiler_params=_params(("arbitrary", "arbitrary")),
        name="front_grid" if grid_shift else "front_seq",
    )(*args)
    return dict(zip([nm for nm, _, _ in _FRONT_OUT], res))


def _bd(x, blk):
    reps = blk.shape[0] // x.shape[0]
    return jnp.where(blk, jnp.concatenate([x] * reps, axis=0), jnp.zeros((), x.dtype))


def _scan_masks(rev, width):
    c = CHUNK
    ri = _iota((c, width), 0)
    jj = _iota((c, width), 1) & (c - 1)
    strict = (jj > ri) if rev else (jj < ri)
    incl = (jj >= ri) if rev else (jj <= ri)
    eye = jj == ri
    a = _iota((c, c), 0)
    bq = _iota((c, c), 1)
    mcum = jnp.where((bq >= a) if rev else (bq <= a), 1.0, 0.0).astype(BF16)
    levels = [strict & ((jj >> (l + 1)) == (ri >> (l + 1))) & ((jj >> l) != (ri >> l)) for l in range(6)]
    return strict, incl, eye, mcum, levels


def _each(f, *lists):
    return [f(*a) for a in zip(*lists)]


def _cumsum(x, mcum):
    hi = x.astype(BF16)
    lo = (x - hi.astype(F32)).astype(BF16)
    return (jnp.dot(mcum, hi, preferred_element_type=F32) + jnp.dot(mcum, lo, preferred_element_type=F32))


def _wkv_masks(rev):
    strict, incl, eye, mcum, levels = _scan_masks(rev, 256)
    one = lambda m: jnp.where(m, 1.0, 0.0)
    return one(strict), one(incl), one(eye), mcum, [one(m) for m in levels]


def _wkv_chains(need_y, ins, hts, masks, blk):
    c = CHUNK
    blk_f, blk_b = blk
    bd = lambda x: jnp.concatenate([x] * 4, axis=0) * blk_b
    revs = [i[0] for i in ins]
    mk = [masks[int(rv)] for rv in revs]
    g = _each(lambda i, m: _cumsum(i[6], m[3]), ins, mk)
    gl = [gg[0:1] if rv else gg[c - 1:c] for gg, rv in zip(g, revs)]
    x2 = _each(lambda i, gg: jnp.concatenate([(i[2] * jnp.exp(gg - i[6])).astype(BF16),
                                              (i[1] * jnp.exp(gg)).astype(BF16)], axis=0), ins, g)
    at = _each(lambda i, gg: (i[3] * jnp.exp(-gg)).astype(BF16), ins, g)
    kt = _each(lambda i, gg: (i[4] * jnp.exp(-gg)).astype(BF16), ins, g)
    sa = _each(lambda x, a: _dot_nt(x, bd(a)), x2, at)
    sk = _each(lambda x, k: _dot_nt(x, bd(k)), x2, kt)
    tm = _each(lambda s, m: m[2] - s[:c] * m[4][0], sa, mk)
    for l in range(1, 6):
        x = _each(lambda t, s, m: _dot(t, bd((s[:c] * m[4][l]).astype(BF16))), tm, sa, mk)
        tm = _each(lambda t, xx: t - _dot(xx, bd(t.astype(BF16))), tm, x)
    xh = _each(_dot, x2, hts)
    vb = [i[5].astype(BF16) for i in ins]
    vbd = [bd(v) for v in vb]
    lkv = _each(lambda s, m, vd: _dot(s[:c] * m[0], vd), sk, mk, vbd)
    ub = _each(lambda t, h, lv: (-_dot(t, bd((h[:c] + lv).astype(BF16)))).astype(BF16), tm, xh, lkv)
    upd = _each(lambda u, v, i, gg, gle: _dot_tn(
        jnp.concatenate([(i[3] * jnp.exp(gle - gg)).astype(BF16), (i[4] * jnp.exp(gle - gg)).astype(BF16)], axis=0),
        jnp.concatenate([u, v], axis=0)),
        ub, vb, ins, g, gl)
    ht_new = _each(lambda h, gle, up: h * _column(jnp.exp(gle)) + up * blk_f, hts, gl, upd)
    ys = [None] * len(ins)
    if need_y:
        ya = _each(lambda s, m, u: _dot(s[c:] * m[1], bd(u)), sa, mk, ub)
        yk = _each(lambda s, m, vd: _dot(s[c:] * m[1], vd), sk, mk, vbd)
        ys = _each(lambda h, a, k: h[c:] + a + k, xh, ya, yk)
    return ys, ht_new


SCAN_TB = 256
WKV_NB = 2
GLA_NB = 4


def _wkv_kernel(need_y, nb, tb, *refs):
    fwd, bwd, h0_ref, rest = refs[:6], refs[6:12], refs[12], refs[13:]
    if need_y:
        yf_ref, yb_ref, hout_ref, hs = rest
    else:
        hout_ref, hs = rest
    c = CHUNK
    nc = tb // c
    w = 256
    j = pl.program_id(1)
    blk_f = jnp.where((_iota((w, w), 0) >> 6) == (_iota((w, w), 1) >> 6), 1.0, 0.0)
    blk = (blk_f, blk_f.astype(BF16))
    masks = (_wkv_masks(False), _wkv_masks(True))

    @pl.when(j == 0)
    def _():
        hs[...] = h0_ref[...]

    def body(ci, carry):
        where, ins, hts = [], [], []
        for ib in range(nb):
            for g in range(RW_WIDTH // w):
                lanes = slice(g * w, (g + 1) * w)
                for d, views in enumerate((fwd, bwd)):
                    cc = ci if d == 0 else nc - 1 - ci
                    rows = pl.ds(pl.multiple_of(cc * c, c), c)
                    r, kk, v, kh, kka = [ref[ib, rows, lanes].astype(F32) for ref in views[:5]]
                    ins.append((d == 1, r, kk, kka, kh, v, views[5][ib, rows, lanes]))
                    hts.append(hs[ib, g, d])
                    where.append((ib, g, d, rows, lanes))
        ys, hts = _wkv_chains(need_y, ins, hts, masks, blk)
        for (ib, g, d, rows, lanes), y, ht in zip(where, ys, hts):
            hs[ib, g, d] = ht
            if need_y:
                y_ref = yb_ref if d else yf_ref
                y_ref[ib, rows, lanes] = y.astype(BF16)
        return carry

    lax.fori_loop(0, nc, body, 0)

    @pl.when(j == pl.num_programs(1) - 1)
    def _():
        hout_ref[...] = hs[...]


def _wkv_call(p, h0, need_y):
    b, t, _ = p["r"].shape
    nb = min(WKV_NB, b)
    tb = min(SCAN_TB, t)
    nblk = t // tb
    ng = RW_WIDTH // 256
    fwd = pl.BlockSpec((nb, tb, RW_WIDTH), lambda i, j: (i, j, 0))
    bwd = pl.BlockSpec((nb, tb, RW_WIDTH), lambda i, j: (i, nblk - 1 - j, 0))
    st = pl.BlockSpec((nb, ng, 2, 256, 256), lambda i, j: (i, 0, 0, 0, 0))
    out_specs = [st]
    out_shape = [jax.ShapeDtypeStruct((b, ng, 2, 256, 256), F32)]
    if need_y:
        out_specs = [fwd, bwd] + out_specs
        out_shape = [jax.ShapeDtypeStruct((b, t, RW_WIDTH), BF16)] * 2 + out_shape
    res = pl.pallas_call(
        functools.partial(_wkv_kernel, need_y, nb, tb),
        grid=(b // nb, nblk),
        in_specs=[fwd] * 6 + [bwd] * 6 + [st],
        out_specs=out_specs,
        out_shape=out_shape,
        scratch_shapes=[pltpu.VMEM((nb, ng, 2, 256, 256), F32)],
        compiler_params=_params(("arbitrary", "arbitrary")),
        name="wkv_y" if need_y else "wkv_state",
    )(p["r"], p["kk"], p["v"], p["kh_f"], p["kka_f"], p["lw_f"],
      p["r"], p["kk"], p["v"], p["kh_b"], p["kka_b"], p["lw_b"], h0)
    return res if need_y else (None, None, res[0])


def _column(row):
    return jnp.transpose(jnp.broadcast_to(row, (8, row.shape[1])))[:, 0:1]


def _gla_chains(need_o, ins, sts, masks, blk_k, blk_v, blk_s):
    c = CHUNK
    revs = [i[0] for i in ins]
    mk = [masks[int(rv)] for rv in revs]
    b = _each(lambda i, m: _cumsum(i[4], m[3]), ins, mk)
    bl = [bb[0:1] if rv else bb[c - 1:c] for bb, rv in zip(b, revs)]
    vb = [i[3].astype(BF16) for i in ins]
    upd = _each(lambda v, i, bb, ble: _dot_tn(v, (i[2] * jnp.exp(ble - bb)).astype(BF16)), vb, ins, b, bl)
    st_new = _each(lambda s, ble, up: s * jnp.exp(ble) + jnp.where(blk_s, up, 0.0), sts, bl, upd)
    os_ = [None] * len(ins)
    if need_o:
        q_in = _each(lambda i, bb: (i[1] * jnp.exp(bb)).astype(BF16), ins, b)
        sc = _each(lambda q, i, bb, m: jnp.where(
            m[1], _dot_nt(q, _bd((i[2] * jnp.exp(-bb)).astype(BF16), blk_k)), 0.0), q_in, ins, b, mk)
        oi = _each(lambda s, v: _dot(s, _bd(v, blk_v)), sc, vb)
        ox = _each(_dot_nt, q_in, sts)
        os_ = _each(lambda a, x: a + x, oi, ox)
    return os_, st_new


def _gla_kernel(need_o, nb, tb, *refs):
    fwd, bwd, s0_ref, rest = refs[:4], refs[4:8], refs[8], refs[9:]
    if need_o:
        of_ref, ob_ref, sout_ref, ss = rest
    else:
        sout_ref, ss = rest
    c = CHUNK
    nc = tb // c
    j = pl.program_id(1)
    blk_k = (_iota((256, 256), 0) >> 6) == (_iota((256, 256), 1) >> 6)
    blk_v = (_iota((256, 512), 0) >> 6) == (_iota((256, 512), 1) >> 7)
    blk_s = (_iota((512, 256), 0) >> 7) == (_iota((512, 256), 1) >> 6)
    masks = (_scan_masks(False, 256), _scan_masks(True, 256))

    @pl.when(j == 0)
    def _():
        ss[...] = s0_ref[...]

    def body(ci, carry):
        where, ins, sts = [], [], []
        for ib in range(nb):
            for d, views in enumerate((fwd, bwd)):
                cc = ci if d == 0 else nc - 1 - ci
                rows = pl.ds(pl.multiple_of(cc * c, c), c)
                q, k, v = [ref[ib, rows, :].astype(F32) for ref in views[:3]]
                ins.append((d == 1, q, k, v, views[3][ib, rows, :]))
                sts.append(ss[ib, d])
                where.append((ib, d, rows))
        os_, sts = _gla_chains(need_o, ins, sts, masks, blk_k, blk_v, blk_s)
        for (ib, d, rows), o, st in zip(where, os_, sts):
            ss[ib, d] = st
            if need_o:
                o_ref = ob_ref if d else of_ref
                o_ref[ib, rows, :] = o.astype(BF16)
        return carry

    lax.fori_loop(0, nc, body, 0)

    @pl.when(j == pl.num_programs(1) - 1)
    def _():
        sout_ref[...] = ss[...]


def _gla_call(p, s0, need_o):
    b, t, _ = p["q"].shape
    nb = min(GLA_NB, b)
    tb = min(SCAN_TB, t)
    nblk = t // tb
    fwd = lambda w: pl.BlockSpec((nb, tb, w), lambda i, j: (i, j, 0))
    bwd = lambda w: pl.BlockSpec((nb, tb, w), lambda i, j: (i, nblk - 1 - j, 0))
    st = pl.BlockSpec((nb, 2, GLA_VAL, GLA_KEY), lambda i, j: (i, 0, 0, 0))
    out_specs = [st]
    out_shape = [jax.ShapeDtypeStruct((b, 2, GLA_VAL, GLA_KEY), F32)]
    if need_o:
        out_specs = [fwd(GLA_VAL), bwd(GLA_VAL)] + out_specs
        out_shape = [jax.ShapeDtypeStruct((b, t, GLA_VAL), BF16)] * 2 + out_shape
    res = pl.pallas_call(
        functools.partial(_gla_kernel, need_o, nb, tb),
        grid=(b // nb, nblk),
        in_specs=[fwd(256), fwd(256), fwd(512), fwd(256), bwd(256), bwd(256), bwd(512), bwd(256), st],
        out_specs=out_specs,
        out_shape=out_shape,
        scratch_shapes=[pltpu.VMEM((nb, 2, GLA_VAL, GLA_KEY), F32)],
        compiler_params=_params(("arbitrary", "arbitrary")),
        name="gla_o" if need_o else "gla_state",
    )(p["q"], p["kg"], p["vg"], p["la_f"], p["q"], p["kg"], p["vg"], p["la_b"], s0)
    return res if need_o else (None, None, res[0])


def _post_kernel(yf_ref, yb_ref, bon_ref, gout_ref, of_ref, ob_ref, gate_ref, x_ref, g1_ref, gnw_ref,
                 gnb_ref, nw_ref, wout_ref, l1g_ref, l1b_ref, seg64_ref, seg128_ref, out_ref):
    y = yf_ref[0].astype(F32) + yb_ref[0].astype(F32)
    seg64 = seg64_ref[...]
    mu = _dot(y, seg64) * (1.0 / RW_HEAD)
    yc = y - mu
    var = _dot(yc * yc, seg64) * (1.0 / RW_HEAD)
    y_n = yc * lax.rsqrt(var + RW_GN_EPS) * gnw_ref[...] + gnb_ref[...]
    rw_out = (y_n + bon_ref[0].astype(F32)) * gout_ref[0].astype(F32)
    o = of_ref[0].astype(F32) + ob_ref[0].astype(F32)
    ms = _dot(o * o, seg128_ref[...]) * (1.0 / GLA_DV)
    gla_out = o * lax.rsqrt(ms + GLA_NORM_EPS) * nw_ref[...] * gate_ref[0].astype(F32)
    mix = jnp.concatenate([rw_out.astype(BF16), gla_out.astype(BF16)], axis=-1)
    proj = jnp.dot(mix, wout_ref[...], preferred_element_type=F32)
    xs = DN_ALPHA * x_ref[0] + g1_ref[0] * proj
    out_ref[0] = _ln(xs) * l1g_ref[...] + l1b_ref[...]


def _post_call(yf, yb, bonus, gout, of, ob, gate, x, g1, pp):
    b, t, d = x.shape
    tm = min(t, 512)
    tile = lambda w: pl.BlockSpec((1, tm, w), lambda i, j: (i, j, 0))
    full = lambda a: pl.BlockSpec(a.shape, lambda i, j: (0,) * a.ndim)
    consts = [pp["gn_w"], pp["gn_b"], pp["gla_nw"], pp["w_out"], pp["ln1_g"], pp["ln1_b"],
              pp["seg64"], pp["seg128"]]
    return pl.pallas_call(
        _post_kernel,
        grid=(b, t // tm),
        in_specs=[tile(512)] * 7 + [tile(d), pl.BlockSpec((1, 1, d), lambda i, j: (i, 0, 0))]
        + [full(a) for a in consts],
        out_specs=tile(d),
        out_shape=jax.ShapeDtypeStruct((b, t, d), F32),
        compiler_params=_params(("arbitrary", "arbitrary")),
        name="post",
    )(yf, yb, bonus, gout, of, ob, gate, x, g1, *consts)


def _prefix_excl(m, su):
    e, t = m.shape
    off = jnp.zeros((e, 1), F32)
    parts = []
    for j in range(t // 128):
        blk = m[:, j * 128:(j + 1) * 128]
        parts.append(_dot(blk, su) + off)
        off = off + jnp.sum(blk, axis=1, keepdims=True)
    return jnp.concatenate(parts, axis=1)


def _route_kernel(cap, x_ref, sh_ref, sc_ref, rw_ref, xg_ref, aff_ref, pos_ref, h2_s, pe_s):
    e = pl.program_id(1)
    t = x_ref.shape[1]

    @pl.when(e == 0)
    def _():
        h2 = _ln(x_ref[0]) * (1.0 + sc_ref[0]) + sh_ref[0]
        h2_s[...] = h2.astype(BF16)
        mm = lambda p, q: jnp.dot(p, q, preferred_element_type=F32)
        h_hi = h2.astype(BF16)
        h_lo = (h2 - h_hi.astype(F32)).astype(BF16)
        w_hi = rw_ref[...].astype(BF16)
        w_lo = (rw_ref[...] - w_hi.astype(F32)).astype(BF16)
        logits = mm(h_hi, w_hi) + (mm(h_lo, w_hi) + mm(h_hi, w_lo))
        lane = _iota((1, 128), 1)
        logits = jnp.where(lane < N_EXPERTS, logits, -1e30)
        m = jnp.max(logits, axis=-1, keepdims=True)
        ex = jnp.exp(logits - m)
        aff = ex / jnp.sum(ex, axis=-1, keepdims=True)
        aff_ref[0] = aff
        aff_t = aff.T[0:N_EXPERTS, :]
        def bs(_, lohi):
            lo, hi = lohi
            mid = lo + ((hi - lo) >> 1)
            mid_f = pltpu.bitcast(mid, F32)[:, 0:1]
            cnt = jnp.sum(jnp.where(aff_t >= mid_f, 1.0, 0.0), axis=1, keepdims=True)
            ok = cnt >= cap
            return jnp.where(ok, mid, lo), jnp.where(ok, hi, mid)

        lo0 = jnp.zeros((N_EXPERTS, 128), jnp.int32)
        hi0 = jnp.full((N_EXPERTS, 128), 0x3F800001, jnp.int32)
        thr_bits, _ = lax.fori_loop(0, 31, bs, (lo0, hi0))
        thr = pltpu.bitcast(thr_bits, F32)[:, 0:1]
        gt = jnp.where(aff_t > thr, 1.0, 0.0)
        eq = jnp.where(aff_t == thr, 1.0, 0.0)
        need = cap - jnp.sum(gt, axis=1, keepdims=True)
        su = jnp.where(_iota((128, 128), 0) < _iota((128, 128), 1), 1.0, 0.0).astype(BF16)
        sel = gt + eq * jnp.where(_prefix_excl(eq, su) < need, 1.0, 0.0)
        pos = jnp.where(sel > 0.0, _prefix_excl(sel, su), -1.0)
        pe_s[0:N_EXPERTS, :] = pos
        pe_s[N_EXPERTS:, :] = jnp.full((128 - N_EXPERTS, t), -1.0, F32)
        pos_ref[0] = pe_s[...].T

    prow = pe_s[pl.ds(e, 1), :]
    slot = _iota((cap, t), 0).astype(F32)
    onehot = jnp.where(prow == slot, 1.0, 0.0).astype(BF16)
    xg_ref[0, 0] = jnp.dot(onehot, h2_s[...], preferred_element_type=F32).astype(BF16)


def _route_call(x1, sh2, sc2, router_pad, cap):
    b, t, d = x1.shape
    return pl.pallas_call(
        functools.partial(_route_kernel, cap),
        grid=(b, N_EXPERTS),
        in_specs=[pl.BlockSpec((1, t, d), lambda i, e: (i, 0, 0)),
                  pl.BlockSpec((1, 1, d), lambda i, e: (i, 0, 0)),
                  pl.BlockSpec((1, 1, d), lambda i, e: (i, 0, 0)),
                  pl.BlockSpec((d, 128), lambda i, e: (0, 0))],
        out_specs=[pl.BlockSpec((1, 1, cap, d), lambda i, e: (i, e, 0, 0)),
                   pl.BlockSpec((1, t, 128), lambda i, e: (i, 0, 0)),
                   pl.BlockSpec((1, t, 128), lambda i, e: (i, 0, 0))],
        out_shape=[jax.ShapeDtypeStruct((b, N_EXPERTS, cap, d), BF16),
                   jax.ShapeDtypeStruct((b, t, 128), F32),
                   jax.ShapeDtypeStruct((b, t, 128), F32)],
        scratch_shapes=[pltpu.VMEM((t, d), BF16), pltpu.VMEM((128, t), F32)],
        compiler_params=_params(("arbitrary", "arbitrary")),
        name="route",
    )(x1, sh2, sc2, router_pad)


def _ffn_kernel(nb, x_ref, wg_ref, wu_ref, wd_ref, y_ref, wg_s, wu_s, wd_s):
    p = pl.program_id(0)
    b = pl.program_id(1)
    ne = pl.num_programs(0) - 1
    rg = wg_s.shape[1] // nb
    rd = wd_s.shape[1] // nb
    og = pl.multiple_of(b * rg, rg)
    od = pl.multiple_of(b * rd, rd)

    for half in (0, 1):
        @pl.when((p < ne) & (p % 2 == half))
        def _():
            wg_s[half, pl.ds(og, rg), :] = wg_ref[0].astype(BF16)
            wu_s[half, pl.ds(og, rg), :] = wu_ref[0].astype(BF16)
            wd_s[half, pl.ds(od, rd), :] = wd_ref[0].astype(BF16)

        @pl.when((p > 0) & (p % 2 != half))
        def _():
            x = x_ref[0, 0]
            g = jnp.dot(x, wg_s[half], preferred_element_type=F32)
            u = jnp.dot(x, wu_s[half], preferred_element_type=F32)
            h = (g * _sigmoid(g) * u).astype(BF16)
            y_ref[0, 0] = jnp.dot(h, wd_s[half], preferred_element_type=F32).astype(BF16)

    @pl.when(p == 0)
    def _():
        y_ref[...] = jnp.zeros(y_ref.shape, BF16)


def _ffn_call(xg, wg, wu, wd):
    b, ne, cap, d = xg.shape
    f = wg.shape[2]
    assert d % (16 * b) == 0 and f % (16 * b) == 0, "weight chunks must be whole bf16 sublane tiles"
    wmap = lambda p, i: (jnp.minimum(p, ne - 1), jnp.where(p < ne, i, b - 1), 0)
    xmap = lambda p, i: (jnp.where(p > 0, i, 0), jnp.maximum(p - 1, 0), 0, 0)
    return pl.pallas_call(
        functools.partial(_ffn_kernel, b),
        grid=(ne + 1, b),
        in_specs=[pl.BlockSpec((1, 1, cap, d), xmap),
                  pl.BlockSpec((1, d // b, f), wmap),
                  pl.BlockSpec((1, d // b, f), wmap),
                  pl.BlockSpec((1, f // b, d), wmap)],
        out_specs=pl.BlockSpec((1, 1, cap, d), xmap),
        out_shape=jax.ShapeDtypeStruct((b, ne, cap, d), BF16),
        scratch_shapes=[pltpu.VMEM((2, d, f), BF16), pltpu.VMEM((2, d, f), BF16), pltpu.VMEM((2, f, d), BF16)],
        compiler_params=_params(("arbitrary", "arbitrary")),
        name="ffn",
    )(xg, wg, wu, wd)


def _combine_kernel(cap, y_ref, pos_ref, aff_ref, x_ref, g2_ref, l2g_ref, l2b_ref, out_ref):
    pos = pos_ref[0]
    gates = aff_ref[0]
    slot = _iota((1, cap), 1).astype(F32)
    acc = None
    for e in range(N_EXPERTS):
        oh = jnp.where(pos[:, e:e + 1] == slot, 1.0, 0.0).astype(BF16)
        part = jnp.dot(oh, y_ref[0, e], preferred_element_type=F32) * gates[:, e:e + 1]
        acc = part if acc is None else acc + part
    xs = DN_ALPHA * x_ref[0] + g2_ref[0] * acc
    out_ref[0] = _ln(xs) * l2g_ref[...] + l2b_ref[...]


def _combine_call(y, pos_t, aff_t, x1, g2, l2g, l2b, cap):
    b, t, d = x1.shape
    tm = min(t, 512)
    return pl.pallas_call(
        functools.partial(_combine_kernel, cap),
        grid=(b, t // tm),
        in_specs=[pl.BlockSpec((1, N_EXPERTS, cap, d), lambda i, j: (i, 0, 0, 0)),
                  pl.BlockSpec((1, tm, 128), lambda i, j: (i, j, 0)),
                  pl.BlockSpec((1, tm, 128), lambda i, j: (i, j, 0)),
                  pl.BlockSpec((1, tm, d), lambda i, j: (i, j, 0)),
                  pl.BlockSpec((1, 1, d), lambda i, j: (i, 0, 0)),
                  pl.BlockSpec((1, d), lambda i, j: (0, 0)),
                  pl.BlockSpec((1, d), lambda i, j: (0, 0))],
        out_specs=pl.BlockSpec((1, tm, d), lambda i, j: (i, j, 0)),
        out_shape=jax.ShapeDtypeStruct((b, t, d), F32),
        compiler_params=_params(("arbitrary", "arbitrary")),
        name="combine",
    )(y, pos_t, aff_t, x1, g2, l2g, l2b)


def _pad_cols(parts, total):
    rows = parts[0][0].shape[0]
    out = jnp.zeros((rows, total), parts[0][0].dtype)
    for a, off in parts:
        out = lax.dynamic_update_slice(out, a, (0, off))
    return out


def _block_ones(n, width):
    i = jnp.arange(n) // width
    return (i[:, None] == i[None, :]).astype(BF16)


def _layout_params(w_in, rw_mu, rw_w0, rw_w2, rw_a0, rw_a2, rw_g2, rw_k_k, rw_k_a, rw_r_k,
                   rw_gn_w, rw_gn_b, gla_a2, gla_a_b, gla_norm_w, w_out, ln1_g, ln1_b):
    rw_in = 1760
    segs = [(0, 1536, COL_R), (1536, 1664, COL_LORA), (1664, 1760, COL_GD),
            (rw_in, rw_in + 256, COL_Q), (rw_in + 256, rw_in + 512, COL_KG),
            (rw_in + 512, rw_in + 1024, COL_VG), (rw_in + 1024, rw_in + 1536, COL_GG),
            (rw_in + 1536, rw_in + 1568, COL_GA)]
    w_pad = _pad_cols([(w_in[:, a:b], off) for a, b, off in segs], IN_PAD).astype(BF16)
    mu = _pad_cols([(rw_mu[None, a:b], off) for a, b, off in segs[:3]], RW_PAD)
    w_lora = jnp.zeros((128, 2048), F32)
    for i, m in enumerate((rw_w2[0], rw_w2[1], rw_a2[0], rw_a2[1])):
        w_lora = lax.dynamic_update_slice(w_lora, m, (32 * i, 512 * i))
    g2 = jnp.zeros((128, 512), F32).at[:96].set(rw_g2)
    a2 = jnp.zeros((128, 512), F32).at[0:16, 0:256].set(gla_a2[0]).at[16:32, 256:512].set(gla_a2[1])
    row = lambda a: a.reshape(1, -1)
    return dict(
        w_in=w_pad, mu=mu, k_k=row(rw_k_k), k_a=row(rw_k_a), r_k=row(rw_r_k), w0=rw_w0, a0=rw_a0,
        ab=gla_a_b, w_lora=w_lora.astype(BF16), g2=g2.astype(BF16), a2=a2.astype(BF16),
        seg64=_block_ones(512, 64), seg128=_block_ones(512, 128),
        gn_w=row(rw_gn_w), gn_b=row(rw_gn_b), gla_nw=row(gla_norm_w), w_out=w_out.astype(BF16),
        ln1_g=row(ln1_g), ln1_b=row(ln1_b))


def kernel(x, c, ctx, c_ctx, ada_w, ada_b, w_in, rw_mu, rw_w0, rw_w2, rw_a0, rw_a2, rw_g2, rw_k_k, rw_k_a, rw_r_k, rw_gn_w, rw_gn_b, gla_a2, gla_a_b, gla_norm_w, w_out, ln1_g, ln1_b, router_w, ex_gate, ex_up, ex_down, ln2_g, ln2_b):
    assert ada_w.shape[0] == 1, "single-layer block"
    b, t, d = x.shape
    cap = CAPACITY_FACTOR * t // N_EXPERTS
    pp = _layout_params(w_in[0], rw_mu[0], rw_w0[0], rw_w2[0], rw_a0[0], rw_a2[0], rw_g2[0],
                        rw_k_k[0], rw_k_a[0], rw_r_k[0], rw_gn_w[0], rw_gn_b[0], gla_a2[0],
                        gla_a_b[0], gla_norm_w[0], w_out[0], ln1_g[0], ln1_b[0])

    rows = -(-(b + 1) // 8) * 8
    cs = jnp.zeros((rows, d), F32).at[:b].set(c).at[b].set(c_ctx)
    mod = _mod_call(cs, ada_w[0], ada_b[0][None])
    sh1, sc1, g1, sh2, sc2, g2 = [m[:, None, :] for m in jnp.split(mod[:b], 6, axis=-1)]
    sh1c, sc1c = [jnp.broadcast_to(m[None, None, :], (b, 1, d)) for m in jnp.split(mod[b], 6)[:2]]

    pc = _front_call(ctx, sh1c, sc1c, False, pp)
    zero_h = jnp.zeros((b, RW_WIDTH // 256, 2, 256, 256), F32)
    zero_s = jnp.zeros((b, 2, GLA_VAL, GLA_KEY), F32)
    _, _, h_ctx = _wkv_call(pc, zero_h, False)
    _, _, s_ctx = _gla_call(pc, zero_s, False)

    pz = _front_call(x, sh1, sc1, True, pp)
    yf, yb, _ = _wkv_call(pz, h_ctx, True)
    of, ob, _ = _gla_call(pz, s_ctx, True)
    x1 = _post_call(yf, yb, pz["bonus"], pz["gout"], of, ob, pz["gate"], x, g1, pp)

    router_pad = jnp.zeros((d, 128), F32).at[:, :N_EXPERTS].set(router_w[0])
    xg, aff_t, pos_t = _route_call(x1, sh2, sc2, router_pad, cap)
    ye = _ffn_call(xg, ex_gate[0], ex_up[0], ex_down[0])
    return _combine_call(ye, pos_t, aff_t, x1, g2, ln2_g[0][None], ln2_b[0][None], cap)
```

```python
import functools

import jax
import jax.numpy as jnp
from jax import lax
from jax.experimental import pallas as pl
from jax.experimental.pallas import tpu as pltpu

F32 = jnp.float32
BF16 = jnp.bfloat16
HIGHEST = lax.Precision.HIGHEST

D_MODEL = 1024
GRID_W = 64
RW_WIDTH = 512
RW_HEAD = 64
RW_GN_EPS = 64e-5
GLA_KEY = 256
GLA_VAL = 512
GLA_DV = 128
GLA_GATE_NORM = 16.0
GLA_NORM_EPS = 1e-5
N_EXPERTS = 16
CAPACITY_FACTOR = 2
DN_ALPHA = 2.0 ** 0.25
LN_EPS = 1e-5
CHUNK = 64
LOG_DECAY_SCALE = 0.6065306597126334

COL_R, COL_K, COL_V, COL_LORA, COL_GD = 0, 512, 1024, 1536, 1664
RW_PAD = 1792
COL_Q, COL_KG, COL_VG, COL_GG, COL_GA = 1792, 2048, 2304, 2816, 3328
IN_PAD = 3456

VMEM_LIMIT = 56 * 1024 * 1024


def _params(sem):
    return pltpu.CompilerParams(dimension_semantics=sem, vmem_limit_bytes=VMEM_LIMIT)


def _dot(a, b):
    return jnp.dot(a.astype(BF16), b.astype(BF16), preferred_element_type=F32)


def _dot_nt(a, b):
    return lax.dot_general(a.astype(BF16), b.astype(BF16), (((1,), (1,)), ((), ())),
                           preferred_element_type=F32)


def _dot_tn(a, b):
    return lax.dot_general(a.astype(BF16), b.astype(BF16), (((0,), (0,)), ((), ())),
                           preferred_element_type=F32)


def _sigmoid(x):
    return 1.0 / (1.0 + jnp.exp(-x))


def _softplus(x):
    return jnp.maximum(x, 0.0) + jnp.log(1.0 + jnp.exp(-jnp.abs(x)))


def _ln(x):
    mu = jnp.mean(x, axis=-1, keepdims=True)
    xc = x - mu
    var = jnp.mean(xc * xc, axis=-1, keepdims=True)
    return xc * lax.rsqrt(var + LN_EPS)


def _iota(shape, dim):
    return lax.broadcasted_iota(jnp.int32, shape, dim)


def _mod_kernel(c_ref, w_ref, b_ref, o_ref):
    c = c_ref[...]
    s = c * _sigmoid(c)
    o_ref[...] = jnp.dot(s, w_ref[...], precision=HIGHEST, preferred_element_type=F32) + b_ref[...]


def _mod_call(cs, ada_w, ada_b):
    rows, d = cs.shape
    n = ada_w.shape[1]
    bn = 1536
    return pl.pallas_call(
        _mod_kernel,
        grid=(n // bn,),
        in_specs=[pl.BlockSpec((rows, d), lambda j: (0, 0)),
                  pl.BlockSpec((d, bn), lambda j: (0, j)),
                  pl.BlockSpec((1, bn), lambda j: (0, j))],
        out_specs=pl.BlockSpec((rows, bn), lambda j: (0, j)),
        out_shape=jax.ShapeDtypeStruct((rows, n), F32),
        compiler_params=_params(("arbitrary",)),
        name="mod",
    )(cs, ada_w, ada_b)


_FRONT_OUT = (
    ("r", 512, BF16), ("kk", 512, BF16), ("v", 512, BF16),
    ("kh_f", 512, BF16), ("kh_b", 512, BF16), ("kka_f", 512, BF16), ("kka_b", 512, BF16),
    ("lw_f", 512, F32), ("lw_b", 512, F32), ("gout", 512, BF16), ("bonus", 512, BF16),
    ("q", 256, BF16), ("kg", 256, BF16), ("vg", 512, BF16),
    ("la_f", 256, F32), ("la_b", 256, F32), ("gate", 512, BF16),
)
_FRONT_PARAMS = ("w_in", "mu", "k_k", "k_a", "r_k", "w0", "a0", "ab", "w_lora", "g2", "a2", "seg64")


def _front_kernel(grid_shift, t_total, tr, *refs):
    if grid_shift:
        x_ref, xp_ref, xn_ref = refs[:3]
        refs = refs[3:]
    else:
        x_ref = refs[0]
        refs = refs[1:]
    sh_ref, sc_ref = refs[:2]
    (w_ref, mu_ref, kk_p, ka_p, rk_p, w0_ref, a0_ref, ab_ref, wl_ref, g2_ref, a2_ref, seg_ref) = refs[2:14]
    outs = dict(zip([n for n, _, _ in _FRONT_OUT], refs[14:14 + len(_FRONT_OUT)]))
    zs = refs[14 + len(_FRONT_OUT)]

    i = pl.program_id(1)
    modulate = lambda xx: _ln(xx) * (1.0 + sc_ref[0]) + sh_ref[0]
    hc = modulate(x_ref[0]).astype(BF16)
    row = _iota((tr, 1), 0) + i * tr
    if grid_shift:
        keep_p = jnp.where(i > 0, 1.0, 0.0)
        keep_n = jnp.where(i < t_total // tr - 1, 1.0, 0.0)
        hall = jnp.concatenate([(modulate(xp_ref[0]) * keep_p).astype(BF16), hc,
                                (modulate(xn_ref[0]) * keep_n).astype(BF16)], axis=0)
        zall = jnp.dot(hall, w_ref[:, 0:RW_PAD], preferred_element_type=F32)
        zc = zall[GRID_W:GRID_W + tr]
        up = zall[0:tr]
        down = zall[2 * GRID_W:2 * GRID_W + tr]
    else:
        zc = jnp.dot(hc, w_ref[:, 0:RW_PAD], preferred_element_type=F32)
    zg = jnp.dot(hc, w_ref[:, RW_PAD:IN_PAD], preferred_element_type=F32)
    zs[0:8, :] = jnp.zeros((8, RW_PAD), F32)
    zs[8 + tr:16 + tr, :] = jnp.zeros((8, RW_PAD), F32)
    zs[8:8 + tr, :] = zc
    left = zs[7:7 + tr, :]
    right = zs[9:9 + tr, :]
    if grid_shift:
        col = row & (GRID_W - 1)
        left = jnp.where(col > 0, left, 0.0)
        right = jnp.where(col < GRID_W - 1, right, 0.0)
        nb = 0.25 * (up + down + left + right)
    else:
        left = jnp.where(row > 0, left, 0.0)
        right = jnp.where(row < t_total - 1, right, 0.0)
        nb = 0.5 * (left + right)
    zr = zc + (nb - zc) * mu_ref[...]

    r = zr[:, COL_R:COL_R + 512]
    k = zr[:, COL_K:COL_K + 512]
    v = zr[:, COL_V:COL_V + 512]
    lo = zr[:, COL_LORA:COL_LORA + 128]
    gd = zr[:, COL_GD:COL_GD + 128]
    seg = seg_ref[...]

    lane = _iota((1, 128), 1)
    lor = _dot(jnp.where(lane < 64, jnp.tanh(lo), lo), wl_ref[...])
    kkr = k * kk_p[...]
    kk = kkr * lax.rsqrt(_dot(kkr * kkr, seg) + 1e-12)
    outs["r"][0] = r.astype(BF16)
    outs["kk"][0] = kk.astype(BF16)
    outs["v"][0] = v.astype(BF16)
    outs["gout"][0] = _dot(_sigmoid(gd), g2_ref[...]).astype(BF16)
    kh_sum = None
    for d, sfx in ((0, "_f"), (1, "_b")):
        u = lor[:, d * 512:(d + 1) * 512] + w0_ref[d:d + 1, :]
        outs["lw" + sfx][0] = (-LOG_DECAY_SCALE) * _sigmoid(u)
        a = _sigmoid(a0_ref[d:d + 1, :] + lor[:, 1024 + d * 512:1536 + d * 512])
        kh = k * (1.0 + (a - 1.0) * ka_p[...])
        outs["kh" + sfx][0] = kh.astype(BF16)
        outs["kka" + sfx][0] = (kk * a).astype(BF16)
        kh_sum = kh if kh_sum is None else kh_sum + kh
    outs["bonus"][0] = (_dot(r * rk_p[...] * kh_sum, seg) * v).astype(BF16)

    g0 = RW_PAD
    outs["q"][0] = (zg[:, COL_Q - g0:COL_Q - g0 + 256] * (RW_HEAD ** -0.5)).astype(BF16)
    outs["kg"][0] = zg[:, COL_KG - g0:COL_KG - g0 + 256].astype(BF16)
    outs["vg"][0] = zg[:, COL_VG - g0:COL_VG - g0 + 512].astype(BF16)
    gg = zg[:, COL_GG - g0:COL_GG - g0 + 512]
    outs["gate"][0] = (gg * _sigmoid(gg)).astype(BF16)
    gl = _dot(zg[:, COL_GA - g0:COL_GA - g0 + 128], a2_ref[...])
    for d, sfx in ((0, "_f"), (1, "_b")):
        xg = gl[:, d * 256:(d + 1) * 256] + ab_ref[d:d + 1, :]
        outs["la" + sfx][0] = -_softplus(-xg) * (1.0 / GLA_GATE_NORM)


def _front_call(x, sh, sc, grid_shift, pp):
    b, t, d = x.shape
    tr = min(t, 512)
    nt = t // tr
    hb = tr // GRID_W
    nh = t // GRID_W
    full = lambda shape: pl.BlockSpec(shape, lambda i, j: (0,) * len(shape))
    vec = pl.BlockSpec((1, 1, d), lambda i, j: (i, 0, 0))
    in_specs = [pl.BlockSpec((1, tr, d), lambda i, j: (i, j, 0))]
    args = [x]
    if grid_shift:
        in_specs += [
            pl.BlockSpec((1, GRID_W, d), lambda i, j: (i, jnp.maximum(j * hb - 1, 0), 0)),
            pl.BlockSpec((1, GRID_W, d), lambda i, j: (i, jnp.minimum((j + 1) * hb, nh - 1), 0)),
        ]
        args += [x, x]
    in_specs += [vec, vec]
    args += [sh, sc]
    for nm in _FRONT_PARAMS:
        in_specs.append(full(pp[nm].shape))
        args.append(pp[nm])
    out_specs = [pl.BlockSpec((1, tr, w), lambda i, j: (i, j, 0)) for _, w, _ in _FRONT_OUT]
    out_shape = [jax.ShapeDtypeStruct((b, t, w), dt) for _, w, dt in _FRONT_OUT]
    res = pl.pallas_call(
        functools.partial(_front_kernel, grid_shift, t, tr),
        grid=(b, nt),
        in_specs=in_specs,
        out_specs=out_specs,
        out_shape=out_shape,
        scratch_shapes=[pltpu.VMEM((tr + 16, RW_PAD), F32)],
        compiler_params=_params(("arbitrary", "arbitrary")),
        name="front_grid" if grid_shift else "front_seq",
    )(*args)
    return dict(zip([nm for nm, _, _ in _FRONT_OUT], res))


def _bd(x, blk):
    reps = blk.shape[0] // x.shape[0]
    return jnp.where(blk, jnp.concatenate([x] * reps, axis=0), jnp.zeros((), x.dtype))


def _scan_masks(rev, width):
    c = CHUNK
    ri = _iota((c, width), 0)
    jj = _iota((c, width), 1) & (c - 1)
    strict = (jj > ri) if rev else (jj < ri)
    incl = (jj >= ri) if rev else (jj <= ri)
    eye = jj == ri
    a = _iota((c, c), 0)
    bq = _iota((c, c), 1)
    mcum = jnp.where((bq >= a) if rev else (bq <= a), 1.0, 0.0).astype(BF16)
    levels = [strict & ((jj >> (l + 1)) == (ri >> (l + 1))) & ((jj >> l) != (ri >> l)) for l in range(6)]
    return strict, incl, eye, mcum, levels


def _each(f, *lists):
    return [f(*a) for a in zip(*lists)]


def _cumsum(x, mcum):
    hi = x.astype(BF16)
    lo = (x - hi.astype(F32)).astype(BF16)
    return (jnp.dot(mcum, hi, preferred_element_type=F32) + jnp.dot(mcum, lo, preferred_element_type=F32))


def _wkv_masks(rev):
    strict, incl, eye, mcum, levels = _scan_masks(rev, 256)
    one = lambda m: jnp.where(m, 1.0, 0.0)
    return one(strict), one(incl), one(eye), mcum, [one(m) for m in levels]


def _wkv_chains(need_y, ins, hts, masks, blk):
    c = CHUNK
    blk_f, blk_b = blk
    bd = lambda x: jnp.concatenate([x] * 4, axis=0) * blk_b
    revs = [i[0] for i in ins]
    mk = [masks[int(rv)] for rv in revs]
    g = _each(lambda i, m: _cumsum(i[6], m[3]), ins, mk)
    gl = [gg[0:1] if rv else gg[c - 1:c] for gg, rv in zip(g, revs)]
    x2 = _each(lambda i, gg: jnp.concatenate([(i[2] * jnp.exp(gg - i[6])).astype(BF16),
                                              (i[1] * jnp.exp(gg)).astype(BF16)], axis=0), ins, g)
    at = _each(lambda i, gg: (i[3] * jnp.exp(-gg)).astype(BF16), ins, g)
    kt = _each(lambda i, gg: (i[4] * jnp.exp(-gg)).astype(BF16), ins, g)
    sa = _each(lambda x, a: _dot_nt(x, bd(a)), x2, at)
    sk = _each(lambda x, k: _dot_nt(x, bd(k)), x2, kt)
    tm = _each(lambda s, m: m[2] - s[:c] * m[4][0], sa, mk)
    for l in range(1, 6):
        x = _each(lambda t, s, m: _dot(t, bd((s[:c] * m[4][l]).astype(BF16))), tm, sa, mk)
        tm = _each(lambda t, xx: t - _dot(xx, bd(t.astype(BF16))), tm, x)
    xh = _each(_dot, x2, hts)
    vb = [i[5].astype(BF16) for i in ins]
    vbd = [bd(v) for v in vb]
    if need_y:
        lky = _each(lambda s, m, vd: _dot(s * jnp.concatenate([m[0], m[1]], axis=0), vd), sk, mk, vbd)
        lkv = [x[:c] for x in lky]
    else:
        lkv = _each(lambda s, m, vd: _dot(s[:c] * m[0], vd), sk, mk, vbd)
    ub = _each(lambda t, h, lv: (-_dot(t, bd((h[:c] + lv).astype(BF16)))).astype(BF16), tm, xh, lkv)
    upd = _each(lambda u, v, i, gg, gle: _dot_tn(
        jnp.concatenate([(i[3] * jnp.exp(gle - gg)).astype(BF16), (i[4] * jnp.exp(gle - gg)).astype(BF16)], axis=0),
        jnp.concatenate([u, v], axis=0)),
        ub, vb, ins, g, gl)
    ht_new = _each(lambda h, gle, up: h * _column(jnp.exp(gle)) + up * blk_f, hts, gl, upd)
    ys = [None] * len(ins)
    if need_y:
        ya = _each(lambda s, m, u: _dot(s[c:] * m[1], bd(u)), sa, mk, ub)
        ys = _each(lambda h, a, k: h[c:] + a + k[c:], xh, ya, lky)
    return ys, ht_new


SCAN_TB = 256
WKV_NB = 2
GLA_NB = 4


def _wkv_kernel(need_y, nb, tb, *refs):
    fwd, bwd, h0_ref, rest = refs[:6], refs[6:12], refs[12], refs[13:]
    if need_y:
        yf_ref, yb_ref, hout_ref, hs = rest
    else:
        hout_ref, hs = rest
    c = CHUNK
    nc = tb // c
    w = 256
    j = pl.program_id(1)
    blk_f = jnp.where((_iota((w, w), 0) >> 6) == (_iota((w, w), 1) >> 6), 1.0, 0.0)
    blk = (blk_f, blk_f.astype(BF16))
    masks = (_wkv_masks(False), _wkv_masks(True))

    @pl.when(j == 0)
    def _():
        hs[...] = h0_ref[...]

    def body(ci, carry):
        where, ins, hts = [], [], []
        for ib in range(nb):
            for g in range(RW_WIDTH // w):
                lanes = slice(g * w, (g + 1) * w)
                for d, views in enumerate((fwd, bwd)):
                    cc = ci if d == 0 else nc - 1 - ci
                    rows = pl.ds(pl.multiple_of(cc * c, c), c)
                    r, kk, v, kh, kka = [ref[ib, rows, lanes].astype(F32) for ref in views[:5]]
                    ins.append((d == 1, r, kk, kka, kh, v, views[5][ib, rows, lanes]))
                    hts.append(hs[ib, g, d])
                    where.append((ib, g, d, rows, lanes))
        ys, hts = _wkv_chains(need_y, ins, hts, masks, blk)
        for (ib, g, d, rows, lanes), y, ht in zip(where, ys, hts):
            hs[ib, g, d] = ht
            if need_y:
                y_ref = yb_ref if d else yf_ref
                y_ref[ib, rows, lanes] = y.astype(BF16)
        return carry

    lax.fori_loop(0, nc, body, 0)

    @pl.when(j == pl.num_programs(1) - 1)
    def _():
        hout_ref[...] = hs[...]


def _wkv_call(p, h0, need_y):
    b, t, _ = p["r"].shape
    nb = min(WKV_NB, b)
    tb = min(SCAN_TB, t)
    nblk = t // tb
    ng = RW_WIDTH // 256
    fwd = pl.BlockSpec((nb, tb, RW_WIDTH), lambda i, j: (i, j, 0))
    bwd = pl.BlockSpec((nb, tb, RW_WIDTH), lambda i, j: (i, nblk - 1 - j, 0))
    st = pl.BlockSpec((nb, ng, 2, 256, 256), lambda i, j: (i, 0, 0, 0, 0))
    out_specs = [st]
    out_shape = [jax.ShapeDtypeStruct((b, ng, 2, 256, 256), F32)]
    if need_y:
        out_specs = [fwd, bwd] + out_specs
        out_shape = [jax.ShapeDtypeStruct((b, t, RW_WIDTH), BF16)] * 2 + out_shape
    res = pl.pallas_call(
        functools.partial(_wkv_kernel, need_y, nb, tb),
        grid=(b // nb, nblk),
        in_specs=[fwd] * 6 + [bwd] * 6 + [st],
        out_specs=out_specs,
        out_shape=out_shape,
        scratch_shapes=[pltpu.VMEM((nb, ng, 2, 256, 256), F32)],
        compiler_params=_params(("arbitrary", "arbitrary")),
        name="wkv_y" if need_y else "wkv_state",
    )(p["r"], p["kk"], p["v"], p["kh_f"], p["kka_f"], p["lw_f"],
      p["r"], p["kk"], p["v"], p["kh_b"], p["kka_b"], p["lw_b"], h0)
    return res if need_y else (None, None, res[0])


def _column(row):
    return jnp.transpose(jnp.broadcast_to(row, (8, row.shape[1])))[:, 0:1]


def _gla_chains(need_o, ins, sts, masks, blk_k, blk_v, blk_s):
    c = CHUNK
    revs = [i[0] for i in ins]
    mk = [masks[int(rv)] for rv in revs]
    b = _each(lambda i, m: _cumsum(i[4], m[3]), ins, mk)
    bl = [bb[0:1] if rv else bb[c - 1:c] for bb, rv in zip(b, revs)]
    vb = [i[3].astype(BF16) for i in ins]
    upd = _each(lambda v, i, bb, ble: _dot_tn(v, (i[2] * jnp.exp(ble - bb)).astype(BF16)), vb, ins, b, bl)
    st_new = _each(lambda s, ble, up: s * jnp.exp(ble) + jnp.where(blk_s, up, 0.0), sts, bl, upd)
    os_ = [None] * len(ins)
    if need_o:
        q_in = _each(lambda i, bb: (i[1] * jnp.exp(bb)).astype(BF16), ins, b)
        sc = _each(lambda q, i, bb, m: jnp.where(
            m[1], _dot_nt(q, _bd((i[2] * jnp.exp(-bb)).astype(BF16), blk_k)), 0.0), q_in, ins, b, mk)
        oi = _each(lambda s, v: _dot(s, _bd(v, blk_v)), sc, vb)
        ox = _each(_dot_nt, q_in, sts)
        os_ = _each(lambda a, x: a + x, oi, ox)
    return os_, st_new


def _gla_kernel(need_o, nb, tb, *refs):
    fwd, bwd, s0_ref, rest = refs[:4], refs[4:8], refs[8], refs[9:]
    if need_o:
        of_ref, ob_ref, sout_ref, ss = rest
    else:
        sout_ref, ss = rest
    c = CHUNK
    nc = tb // c
    j = pl.program_id(1)
    blk_k = (_iota((256, 256), 0) >> 6) == (_iota((256, 256), 1) >> 6)
    blk_v = (_iota((256, 512), 0) >> 6) == (_iota((256, 512), 1) >> 7)
    blk_s = (_iota((512, 256), 0) >> 7) == (_iota((512, 256), 1) >> 6)
    masks = (_scan_masks(False, 256), _scan_masks(True, 256))

    @pl.when(j == 0)
    def _():
        ss[...] = s0_ref[...]

    def body(ci, carry):
        where, ins, sts = [], [], []
        for ib in range(nb):
            for d, views in enumerate((fwd, bwd)):
                cc = ci if d == 0 else nc - 1 - ci
                rows = pl.ds(pl.multiple_of(cc * c, c), c)
                q, k, v = [ref[ib, rows, :].astype(F32) for ref in views[:3]]
                ins.append((d == 1, q, k, v, views[3][ib, rows, :]))
                sts.append(ss[ib, d])
                where.append((ib, d, rows))
        os_, sts = _gla_chains(need_o, ins, sts, masks, blk_k, blk_v, blk_s)
        for (ib, d, rows), o, st in zip(where, os_, sts):
            ss[ib, d] = st
            if need_o:
                o_ref = ob_ref if d else of_ref
                o_ref[ib, rows, :] = o.astype(BF16)
        return carry

    lax.fori_loop(0, nc, body, 0)

    @pl.when(j == pl.num_programs(1) - 1)
    def _():
        sout_ref[...] = ss[...]


def _gla_call(p, s0, need_o):
    b, t, _ = p["q"].shape
    nb = min(GLA_NB, b)
    tb = min(SCAN_TB, t)
    nblk = t // tb
    fwd = lambda w: pl.BlockSpec((nb, tb, w), lambda i, j: (i, j, 0))
    bwd = lambda w: pl.BlockSpec((nb, tb, w), lambda i, j: (i, nblk - 1 - j, 0))
    st = pl.BlockSpec((nb, 2, GLA_VAL, GLA_KEY), lambda i, j: (i, 0, 0, 0))
    out_specs = [st]
    out_shape = [jax.ShapeDtypeStruct((b, 2, GLA_VAL, GLA_KEY), F32)]
    if need_o:
        out_specs = [fwd(GLA_VAL), bwd(GLA_VAL)] + out_specs
        out_shape = [jax.ShapeDtypeStruct((b, t, GLA_VAL), BF16)] * 2 + out_shape
    res = pl.pallas_call(
        functools.partial(_gla_kernel, need_o, nb, tb),
        grid=(b // nb, nblk),
        in_specs=[fwd(256), fwd(256), fwd(512), fwd(256), bwd(256), bwd(256), bwd(512), bwd(256), st],
        out_specs=out_specs,
        out_shape=out_shape,
        scratch_shapes=[pltpu.VMEM((nb, 2, GLA_VAL, GLA_KEY), F32)],
        compiler_params=_params(("arbitrary", "arbitrary")),
        name="gla_o" if need_o else "gla_state",
    )(p["q"], p["kg"], p["vg"], p["la_f"], p["q"], p["kg"], p["vg"], p["la_b"], s0)
    return res if need_o else (None, None, res[0])


def _post_kernel(yf_ref, yb_ref, bon_ref, gout_ref, of_ref, ob_ref, gate_ref, x_ref, g1_ref, gnw_ref,
                 gnb_ref, nw_ref, wout_ref, l1g_ref, l1b_ref, seg64_ref, seg128_ref, out_ref):
    y = yf_ref[0].astype(F32) + yb_ref[0].astype(F32)
    seg64 = seg64_ref[...]
    mu = _dot(y, seg64) * (1.0 / RW_HEAD)
    yc = y - mu
    var = _dot(yc * yc, seg64) * (1.0 / RW_HEAD)
    y_n = yc * lax.rsqrt(var + RW_GN_EPS) * gnw_ref[...] + gnb_ref[...]
    rw_out = (y_n + bon_ref[0].astype(F32)) * gout_ref[0].astype(F32)
    o = of_ref[0].astype(F32) + ob_ref[0].astype(F32)
    ms = _dot(o * o, seg128_ref[...]) * (1.0 / GLA_DV)
    gla_out = o * lax.rsqrt(ms + GLA_NORM_EPS) * nw_ref[...] * gate_ref[0].astype(F32)
    mix = jnp.concatenate([rw_out.astype(BF16), gla_out.astype(BF16)], axis=-1)
    proj = jnp.dot(mix, wout_ref[...], preferred_element_type=F32)
    xs = DN_ALPHA * x_ref[0] + g1_ref[0] * proj
    out_ref[0] = _ln(xs) * l1g_ref[...] + l1b_ref[...]


def _post_call(yf, yb, bonus, gout, of, ob, gate, x, g1, pp):
    b, t, d = x.shape
    tm = min(t, 512)
    tile = lambda w: pl.BlockSpec((1, tm, w), lambda i, j: (i, j, 0))
    full = lambda a: pl.BlockSpec(a.shape, lambda i, j: (0,) * a.ndim)
    consts = [pp["gn_w"], pp["gn_b"], pp["gla_nw"], pp["w_out"], pp["ln1_g"], pp["ln1_b"],
              pp["seg64"], pp["seg128"]]
    return pl.pallas_call(
        _post_kernel,
        grid=(b, t // tm),
        in_specs=[tile(512)] * 7 + [tile(d), pl.BlockSpec((1, 1, d), lambda i, j: (i, 0, 0))]
        + [full(a) for a in consts],
        out_specs=tile(d),
        out_shape=jax.ShapeDtypeStruct((b, t, d), F32),
        compiler_params=_params(("arbitrary", "arbitrary")),
        name="post",
    )(yf, yb, bonus, gout, of, ob, gate, x, g1, *consts)


def _prefix_excl(m, su):
    e, t = m.shape
    off = jnp.zeros((e, 1), F32)
    parts = []
    for j in range(t // 128):
        blk = m[:, j * 128:(j + 1) * 128]
        parts.append(_dot(blk, su) + off)
        off = off + jnp.sum(blk, axis=1, keepdims=True)
    return jnp.concatenate(parts, axis=1)


ROUTE_TBK = 256
GATHER_W = 96
COMBINE_W = 128
META_ROWS = 2 * N_EXPERTS + 8


def _select_kernel(cap, gw, cw, x_ref, sh_ref, sc_ref, rw_ref, h2_ref, aff_ref, post_ref, pose_ref, meta_ref):
    t = x_ref.shape[1]
    ne = N_EXPERTS
    h2 = _ln(x_ref[0]) * (1.0 + sc_ref[0]) + sh_ref[0]
    mm = lambda p, q: jnp.dot(p, q, preferred_element_type=F32)
    h_hi = h2.astype(BF16)
    h2_ref[0] = h_hi
    h_lo = (h2 - h_hi.astype(F32)).astype(BF16)
    w_hi = rw_ref[...].astype(BF16)
    w_lo = (rw_ref[...] - w_hi.astype(F32)).astype(BF16)
    logits = mm(h_hi, w_hi) + (mm(h_lo, w_hi) + mm(h_hi, w_lo))
    lane = _iota((1, 128), 1)
    logits = jnp.where(lane < ne, logits, -1e30)
    m = jnp.max(logits, axis=-1, keepdims=True)
    ex = jnp.exp(logits - m)
    aff = ex / jnp.sum(ex, axis=-1, keepdims=True)
    aff_ref[0] = aff
    aff_t = aff.T[0:ne, :]

    def bs(_, lohi):
        lo, hi = lohi
        mid = lo + ((hi - lo) >> 1)
        mid_f = pltpu.bitcast(mid, F32)[:, 0:1]
        cnt = jnp.sum(jnp.where(aff_t >= mid_f, 1.0, 0.0), axis=1, keepdims=True)
        ok = cnt >= cap
        return jnp.where(ok, mid, lo), jnp.where(ok, hi, mid)

    lo0 = jnp.zeros((ne, 128), jnp.int32)
    hi0 = jnp.full((ne, 128), 0x3F800001, jnp.int32)
    thr_bits, _ = lax.fori_loop(0, 31, bs, (lo0, hi0))
    thr = pltpu.bitcast(thr_bits, F32)[:, 0:1]
    gt = jnp.where(aff_t > thr, 1.0, 0.0)
    eq = jnp.where(aff_t == thr, 1.0, 0.0)
    need = cap - jnp.sum(gt, axis=1, keepdims=True)
    su = jnp.where(_iota((128, 128), 0) < _iota((128, 128), 1), 1.0, 0.0).astype(BF16)
    sel = gt + eq * jnp.where(_prefix_excl(eq, su) < need, 1.0, 0.0)
    pos = jnp.where(sel > 0.0, _prefix_excl(sel, su), -1.0)
    pose_ref[0, 0:ne, :] = pos
    pose_ref[0, ne:, :] = jnp.full((128 - ne, t), -1.0, F32)
    post_ref[0] = pose_ref[0].T

    lanes = _iota((ne, 128), 1)
    first = jnp.zeros((ne, 1), F32)
    og = jnp.zeros((ne, 128), F32)
    oc = jnp.zeros((ne, 128), F32)
    fits = jnp.ones((ne, 1), F32)
    for j in range(t // ROUTE_TBK):
        n = jnp.sum(sel[:, j * ROUTE_TBK:(j + 1) * ROUTE_TBK], axis=1, keepdims=True)
        a16 = jnp.floor(first * (1.0 / 16.0)) * 16.0
        sg = jnp.minimum(a16, float(cap - gw))
        sc_ = jnp.minimum(a16, float(cap - cw))
        fits = fits * jnp.where(first - sg + n <= gw, 1.0, 0.0) * jnp.where(first - sc_ + n <= cw, 1.0, 0.0)
        og = jnp.where(lanes == j, sg, og)
        oc = jnp.where(lanes == j, sc_, oc)
        first = first + n
    meta_ref[0, 0:ne, :] = og.astype(jnp.int32)
    meta_ref[0, ne:2 * ne, :] = oc.astype(jnp.int32)
    meta_ref[0, 2 * ne:, :] = jnp.broadcast_to(jnp.min(fits, axis=0, keepdims=True), (8, 128)).astype(jnp.int32)


def _select_call(x1, sh2, sc2, router_pad, cap, gw, cw):
    b, t, d = x1.shape
    row = lambda w: pl.BlockSpec((1, t, w), lambda i: (i, 0, 0))
    vec = pl.BlockSpec((1, 1, d), lambda i: (i, 0, 0))
    return pl.pallas_call(
        functools.partial(_select_kernel, cap, gw, cw),
        grid=(b,),
        in_specs=[row(d), vec, vec, pl.BlockSpec((d, 128), lambda i: (0, 0))],
        out_specs=[row(d), row(128), row(128), pl.BlockSpec((1, 128, t), lambda i: (i, 0, 0)),
                   pl.BlockSpec((1, META_ROWS, 128), lambda i: (i, 0, 0))],
        out_shape=[jax.ShapeDtypeStruct((b, t, d), BF16),
                   jax.ShapeDtypeStruct((b, t, 128), F32),
                   jax.ShapeDtypeStruct((b, t, 128), F32),
                   jax.ShapeDtypeStruct((b, 128, t), F32),
                   jax.ShapeDtypeStruct((b, META_ROWS, 128), jnp.int32)],
        compiler_params=_params(("arbitrary",)),
        name="select",
    )(x1, sh2, sc2, router_pad)


def _gather_kernel(cap, gw, nblk, og_ref, ok_ref, h2_ref, pose_ref, xg_ref):
    b = pl.program_id(0)
    t = h2_ref.shape[1]
    ne = N_EXPERTS

    @pl.when(ok_ref[b] != 0)
    def _():
        xg_ref[...] = jnp.zeros(xg_ref.shape, BF16)
        srow = _iota((gw, 1), 0).astype(F32)
        for j in range(nblk):
            tok = slice(j * ROUTE_TBK, (j + 1) * ROUTE_TBK)
            starts = [pl.multiple_of(og_ref[(b * ne + e) * nblk + j], 16) for e in range(ne)]
            oh = jnp.concatenate(
                [jnp.where(pose_ref[0, e:e + 1, tok] == srow + starts[e].astype(F32), 1.0, 0.0).astype(BF16)
                 for e in range(ne)], axis=0)
            res = jnp.dot(oh, h2_ref[0, tok, :], preferred_element_type=F32).astype(BF16)
            for e in range(ne):
                rows = pl.ds(starts[e], gw)
                xg_ref[0, e, rows, :] = xg_ref[0, e, rows, :] + res[e * gw:(e + 1) * gw]

    @pl.when(ok_ref[b] == 0)
    def _():
        slot = _iota((cap, t), 0).astype(F32)
        for e in range(ne):
            onehot = jnp.where(pose_ref[0, e:e + 1, :] == slot, 1.0, 0.0).astype(BF16)
            xg_ref[0, e] = jnp.dot(onehot, h2_ref[0], preferred_element_type=F32).astype(BF16)


def _gather_call(h2, pos_e, og, ok, cap, gw):
    b, t, d = h2.shape
    nblk = t // ROUTE_TBK
    grid_spec = pltpu.PrefetchScalarGridSpec(
        num_scalar_prefetch=2,
        grid=(b,),
        in_specs=[pl.BlockSpec((1, t, d), lambda i, *_: (i, 0, 0)),
                  pl.BlockSpec((1, 128, t), lambda i, *_: (i, 0, 0))],
        out_specs=pl.BlockSpec((1, N_EXPERTS, cap, d), lambda i, *_: (i, 0, 0, 0)),
    )
    return pl.pallas_call(
        functools.partial(_gather_kernel, cap, gw, nblk),
        grid_spec=grid_spec,
        out_shape=jax.ShapeDtypeStruct((b, N_EXPERTS, cap, d), BF16),
        compiler_params=_params(("arbitrary",)),
        name="gather",
    )(og, ok, h2, pos_e)


def _ffn_kernel(nb, x_ref, wg_ref, wu_ref, wd_ref, y_ref, wg_s, wu_s, wd_s):
    p = pl.program_id(0)
    b = pl.program_id(1)
    ne = pl.num_programs(0) - 1
    rg = wg_s.shape[1] // nb
    rd = wd_s.shape[1] // nb
    og = pl.multiple_of(b * rg, rg)
    od = pl.multiple_of(b * rd, rd)

    for half in (0, 1):
        @pl.when((p < ne) & (p % 2 == half))
        def _():
            wg_s[half, pl.ds(og, rg), :] = wg_ref[0].astype(BF16)
            wu_s[half, pl.ds(og, rg), :] = wu_ref[0].astype(BF16)
            wd_s[half, pl.ds(od, rd), :] = wd_ref[0].astype(BF16)

        @pl.when((p > 0) & (p % 2 != half))
        def _():
            x = x_ref[0, 0]
            g = jnp.dot(x, wg_s[half], preferred_element_type=F32)
            u = jnp.dot(x, wu_s[half], preferred_element_type=F32)
            h = (g * _sigmoid(g) * u).astype(BF16)
            y_ref[0, 0] = jnp.dot(h, wd_s[half], preferred_element_type=F32).astype(BF16)

    @pl.when(p == 0)
    def _():
        y_ref[...] = jnp.zeros(y_ref.shape, BF16)


def _ffn_call(xg, wg, wu, wd):
    b, ne, cap, d = xg.shape
    f = wg.shape[2]
    assert d % (16 * b) == 0 and f % (16 * b) == 0, "weight chunks must be whole bf16 sublane tiles"
    wmap = lambda p, i: (jnp.minimum(p, ne - 1), jnp.where(p < ne, i, b - 1), 0)
    xmap = lambda p, i: (jnp.where(p > 0, i, 0), jnp.maximum(p - 1, 0), 0, 0)
    return pl.pallas_call(
        functools.partial(_ffn_kernel, b),
        grid=(ne + 1, b),
        in_specs=[pl.BlockSpec((1, 1, cap, d), xmap),
                  pl.BlockSpec((1, d // b, f), wmap),
                  pl.BlockSpec((1, d // b, f), wmap),
                  pl.BlockSpec((1, f // b, d), wmap)],
        out_specs=pl.BlockSpec((1, 1, cap, d), xmap),
        out_shape=jax.ShapeDtypeStruct((b, ne, cap, d), BF16),
        scratch_shapes=[pltpu.VMEM((2, d, f), BF16), pltpu.VMEM((2, d, f), BF16), pltpu.VMEM((2, f, d), BF16)],
        compiler_params=_params(("arbitrary", "arbitrary")),
        name="ffn",
    )(xg, wg, wu, wd)


def _combine_kernel(cap, cw, nblk, oc_ref, ok_ref, y_ref, pos_ref, aff_ref, x_ref, g2_ref, l2g_ref, l2b_ref,
                    out_ref):
    b = pl.program_id(0)
    j = pl.program_id(1)
    ne = N_EXPERTS
    pos = pos_ref[0]
    gates = aff_ref[0]

    def finish(acc):
        xs = DN_ALPHA * x_ref[0] + g2_ref[0] * acc
        out_ref[0] = _ln(xs) * l2g_ref[...] + l2b_ref[...]

    @pl.when(ok_ref[b] != 0)
    def _():
        lane = _iota((1, cw), 1).astype(F32)
        ohs, ys = [], []
        for e in range(ne):
            a = pl.multiple_of(oc_ref[(b * ne + e) * nblk + j], 16)
            ohs.append(jnp.where(pos[:, e:e + 1] == lane + a.astype(F32), gates[:, e:e + 1], 0.0).astype(BF16))
            ys.append(y_ref[0, e, pl.ds(a, cw), :])
        finish(jnp.dot(jnp.concatenate(ohs, axis=1), jnp.concatenate(ys, axis=0), preferred_element_type=F32))

    @pl.when(ok_ref[b] == 0)
    def _():
        slot = _iota((1, cap), 1).astype(F32)
        acc = None
        for e in range(ne):
            oh = jnp.where(pos[:, e:e + 1] == slot, 1.0, 0.0).astype(BF16)
            part = jnp.dot(oh, y_ref[0, e], preferred_element_type=F32) * gates[:, e:e + 1]
            acc = part if acc is None else acc + part
        finish(acc)


def _combine_call(y, pos_t, aff_t, x1, g2, l2g, l2b, oc, ok, cap, cw):
    b, t, d = x1.shape
    tm = ROUTE_TBK
    nblk = t // tm
    grid_spec = pltpu.PrefetchScalarGridSpec(
        num_scalar_prefetch=2,
        grid=(b, nblk),
        in_specs=[pl.BlockSpec((1, N_EXPERTS, cap, d), lambda i, j, *_: (i, 0, 0, 0)),
                  pl.BlockSpec((1, tm, 128), lambda i, j, *_: (i, j, 0)),
                  pl.BlockSpec((1, tm, 128), lambda i, j, *_: (i, j, 0)),
                  pl.BlockSpec((1, tm, d), lambda i, j, *_: (i, j, 0)),
                  pl.BlockSpec((1, 1, d), lambda i, j, *_: (i, 0, 0)),
                  pl.BlockSpec((1, d), lambda i, j, *_: (0, 0)),
                  pl.BlockSpec((1, d), lambda i, j, *_: (0, 0))],
        out_specs=pl.BlockSpec((1, tm, d), lambda i, j, *_: (i, j, 0)),
    )
    return pl.pallas_call(
        functools.partial(_combine_kernel, cap, cw, nblk),
        grid_spec=grid_spec,
        out_shape=jax.ShapeDtypeStruct((b, t, d), F32),
        compiler_params=_params(("arbitrary", "arbitrary")),
        name="combine",
    )(oc, ok, y, pos_t, aff_t, x1, g2, l2g, l2b)


def _pad_cols(parts, total):
    rows = parts[0][0].shape[0]
    out = jnp.zeros((rows, total), parts[0][0].dtype)
    for a, off in parts:
        out = lax.dynamic_update_slice(out, a, (0, off))
    return out


def _block_ones(n, width):
    i = jnp.arange(n) // width
    return (i[:, None] == i[None, :]).astype(BF16)


def _layout_params(w_in, rw_mu, rw_w0, rw_w2, rw_a0, rw_a2, rw_g2, rw_k_k, rw_k_a, rw_r_k,
                   rw_gn_w, rw_gn_b, gla_a2, gla_a_b, gla_norm_w, w_out, ln1_g, ln1_b):
    rw_in = 1760
    segs = [(0, 1536, COL_R), (1536, 1664, COL_LORA), (1664, 1760, COL_GD),
            (rw_in, rw_in + 256, COL_Q), (rw_in + 256, rw_in + 512, COL_KG),
            (rw_in + 512, rw_in + 1024, COL_VG), (rw_in + 1024, rw_in + 1536, COL_GG),
            (rw_in + 1536, rw_in + 1568, COL_GA)]
    w_pad = _pad_cols([(w_in[:, a:b], off) for a, b, off in segs], IN_PAD).astype(BF16)
    mu = _pad_cols([(rw_mu[None, a:b], off) for a, b, off in segs[:3]], RW_PAD)
    w_lora = jnp.zeros((128, 2048), F32)
    for i, m in enumerate((rw_w2[0], rw_w2[1], rw_a2[0], rw_a2[1])):
        w_lora = lax.dynamic_update_slice(w_lora, m, (32 * i, 512 * i))
    g2 = jnp.zeros((128, 512), F32).at[:96].set(rw_g2)
    a2 = jnp.zeros((128, 512), F32).at[0:16, 0:256].set(gla_a2[0]).at[16:32, 256:512].set(gla_a2[1])
    row = lambda a: a.reshape(1, -1)
    return dict(
        w_in=w_pad, mu=mu, k_k=row(rw_k_k), k_a=row(rw_k_a), r_k=row(rw_r_k), w0=rw_w0, a0=rw_a0,
        ab=gla_a_b, w_lora=w_lora.astype(BF16), g2=g2.astype(BF16), a2=a2.astype(BF16),
        seg64=_block_ones(512, 64), seg128=_block_ones(512, 128),
        gn_w=row(rw_gn_w), gn_b=row(rw_gn_b), gla_nw=row(gla_norm_w), w_out=w_out.astype(BF16),
        ln1_g=row(ln1_g), ln1_b=row(ln1_b))


def kernel(x, c, ctx, c_ctx, ada_w, ada_b, w_in, rw_mu, rw_w0, rw_w2, rw_a0, rw_a2, rw_g2, rw_k_k, rw_k_a, rw_r_k, rw_gn_w, rw_gn_b, gla_a2, gla_a_b, gla_norm_w, w_out, ln1_g, ln1_b, router_w, ex_gate, ex_up, ex_down, ln2_g, ln2_b):
    assert ada_w.shape[0] == 1, "single-layer block"
    b, t, d = x.shape
    cap = CAPACITY_FACTOR * t // N_EXPERTS
    pp = _layout_params(w_in[0], rw_mu[0], rw_w0[0], rw_w2[0], rw_a0[0], rw_a2[0], rw_g2[0],
                        rw_k_k[0], rw_k_a[0], rw_r_k[0], rw_gn_w[0], rw_gn_b[0], gla_a2[0],
                        gla_a_b[0], gla_norm_w[0], w_out[0], ln1_g[0], ln1_b[0])

    rows = -(-(b + 1) // 8) * 8
    cs = jnp.zeros((rows, d), F32).at[:b].set(c).at[b].set(c_ctx)
    mod = _mod_call(cs, ada_w[0], ada_b[0][None])
    sh1, sc1, g1, sh2, sc2, g2 = [m[:, None, :] for m in jnp.split(mod[:b], 6, axis=-1)]
    sh1c, sc1c = [jnp.broadcast_to(m[None, None, :], (b, 1, d)) for m in jnp.split(mod[b], 6)[:2]]

    pc = _front_call(ctx, sh1c, sc1c, False, pp)
    zero_h = jnp.zeros((b, RW_WIDTH // 256, 2, 256, 256), F32)
    zero_s = jnp.zeros((b, 2, GLA_VAL, GLA_KEY), F32)
    _, _, h_ctx = _wkv_call(pc, zero_h, False)
    _, _, s_ctx = _gla_call(pc, zero_s, False)

    pz = _front_call(x, sh1, sc1, True, pp)
    yf, yb, _ = _wkv_call(pz, h_ctx, True)
    of, ob, _ = _gla_call(pz, s_ctx, True)
    x1 = _post_call(yf, yb, pz["bonus"], pz["gout"], of, ob, pz["gate"], x, g1, pp)

    router_pad = jnp.zeros((d, 128), F32).at[:, :N_EXPERTS].set(router_w[0])
    gw, cw = min(GATHER_W, cap), min(COMBINE_W, cap)
    nblk = t // ROUTE_TBK
    h2, aff_t, pos_t, pos_e, meta = _select_call(x1, sh2, sc2, router_pad, cap, gw, cw)
    og = meta[:, 0:N_EXPERTS, 0:nblk].reshape(-1)
    oc = meta[:, N_EXPERTS:2 * N_EXPERTS, 0:nblk].reshape(-1)
    ok = meta[:, 2 * N_EXPERTS, 0]
    xg = _gather_call(h2, pos_e, og, ok, cap, gw)
    ye = _ffn_call(xg, ex_gate[0], ex_up[0], ex_down[0])
    return _combine_call(ye, pos_t, aff_t, x1, g2, ln2_g[0][None], ln2_b[0][None], oc, ok, cap, cw)
```

```python
import functools

import jax
import jax.numpy as jnp
from jax import lax
from jax.experimental import pallas as pl
from jax.experimental.pallas import tpu as pltpu

F32 = jnp.float32
BF16 = jnp.bfloat16
HIGHEST = lax.Precision.HIGHEST

D_MODEL = 1024
GRID_W = 64
RW_WIDTH = 512
RW_HEAD = 64
RW_GN_EPS = 64e-5
GLA_KEY = 256
GLA_VAL = 512
GLA_DV = 128
GLA_GATE_NORM = 16.0
GLA_NORM_EPS = 1e-5
N_EXPERTS = 16
CAPACITY_FACTOR = 2
DN_ALPHA = 2.0 ** 0.25
LN_EPS = 1e-5
CHUNK = 64
LOG_DECAY_SCALE = 0.6065306597126334

COL_R, COL_K, COL_V, COL_LORA, COL_GD = 0, 512, 1024, 1536, 1664
RW_PAD = 1792
COL_Q, COL_KG, COL_VG, COL_GG, COL_GA = 1792, 2048, 2304, 2816, 3328
IN_PAD = 3456

VMEM_LIMIT = 56 * 1024 * 1024


def _params(sem):
    return pltpu.CompilerParams(dimension_semantics=sem, vmem_limit_bytes=VMEM_LIMIT)


def _dot(a, b):
    return jnp.dot(a.astype(BF16), b.astype(BF16), preferred_element_type=F32)


def _dot_nt(a, b):
    return lax.dot_general(a.astype(BF16), b.astype(BF16), (((1,), (1,)), ((), ())),
                           preferred_element_type=F32)


def _dot_tn(a, b):
    return lax.dot_general(a.astype(BF16), b.astype(BF16), (((0,), (0,)), ((), ())),
                           preferred_element_type=F32)


def _sigmoid(x):
    return 1.0 / (1.0 + jnp.exp(-x))


def _softplus(x):
    return jnp.maximum(x, 0.0) + jnp.log(1.0 + jnp.exp(-jnp.abs(x)))


def _ln(x):
    mu = jnp.mean(x, axis=-1, keepdims=True)
    xc = x - mu
    var = jnp.mean(xc * xc, axis=-1, keepdims=True)
    return xc * lax.rsqrt(var + LN_EPS)


def _iota(shape, dim):
    return lax.broadcasted_iota(jnp.int32, shape, dim)


def _mod_kernel(c_ref, w_ref, b_ref, o_ref):
    c = c_ref[...]
    s = c * _sigmoid(c)
    o_ref[...] = jnp.dot(s, w_ref[...], precision=HIGHEST, preferred_element_type=F32) + b_ref[...]


def _mod_call(cs, ada_w, ada_b):
    rows, d = cs.shape
    n = ada_w.shape[1]
    bn = 1536
    return pl.pallas_call(
        _mod_kernel,
        grid=(n // bn,),
        in_specs=[pl.BlockSpec((rows, d), lambda j: (0, 0)),
                  pl.BlockSpec((d, bn), lambda j: (0, j)),
                  pl.BlockSpec((1, bn), lambda j: (0, j))],
        out_specs=pl.BlockSpec((rows, bn), lambda j: (0, j)),
        out_shape=jax.ShapeDtypeStruct((rows, n), F32),
        compiler_params=_params(("arbitrary",)),
        name="mod",
    )(cs, ada_w, ada_b)


_FRONT_OUT = (
    ("r", 512, BF16), ("kk", 512, BF16), ("v", 512, BF16),
    ("kh_f", 512, BF16), ("kh_b", 512, BF16), ("kka_f", 512, BF16), ("kka_b", 512, BF16),
    ("lw_f", 512, F32), ("lw_b", 512, F32), ("gout", 512, BF16), ("bonus", 512, BF16),
    ("q", 256, BF16), ("kg", 256, BF16), ("vg", 512, BF16),
    ("la_f", 256, F32), ("la_b", 256, F32), ("gate", 512, BF16),
)
_FRONT_PARAMS = ("w_in", "mu", "k_k", "k_a", "r_k", "w0", "a0", "ab", "w_lora", "g2", "a2", "seg64")


def _front_kernel(grid_shift, t_total, tr, *refs):
    if grid_shift:
        x_ref, xp_ref, xn_ref = refs[:3]
        refs = refs[3:]
    else:
        x_ref = refs[0]
        refs = refs[1:]
    sh_ref, sc_ref = refs[:2]
    (w_ref, mu_ref, kk_p, ka_p, rk_p, w0_ref, a0_ref, ab_ref, wl_ref, g2_ref, a2_ref, seg_ref) = refs[2:14]
    outs = dict(zip([n for n, _, _ in _FRONT_OUT], refs[14:14 + len(_FRONT_OUT)]))
    zs = refs[14 + len(_FRONT_OUT)]

    i = pl.program_id(1)
    modulate = lambda xx: _ln(xx) * (1.0 + sc_ref[0]) + sh_ref[0]
    hc = modulate(x_ref[0]).astype(BF16)
    row = _iota((tr, 1), 0) + i * tr
    if grid_shift:
        keep_p = jnp.where(i > 0, 1.0, 0.0)
        keep_n = jnp.where(i < t_total // tr - 1, 1.0, 0.0)
        hall = jnp.concatenate([(modulate(xp_ref[0]) * keep_p).astype(BF16), hc,
                                (modulate(xn_ref[0]) * keep_n).astype(BF16)], axis=0)
        zall = jnp.dot(hall, w_ref[:, 0:RW_PAD], preferred_element_type=F32)
        zc = zall[GRID_W:GRID_W + tr]
        up = zall[0:tr]
        down = zall[2 * GRID_W:2 * GRID_W + tr]
    else:
        zc = jnp.dot(hc, w_ref[:, 0:RW_PAD], preferred_element_type=F32)
    zg = jnp.dot(hc, w_ref[:, RW_PAD:IN_PAD], preferred_element_type=F32)
    zs[0:8, :] = jnp.zeros((8, RW_PAD), F32)
    zs[8 + tr:16 + tr, :] = jnp.zeros((8, RW_PAD), F32)
    zs[8:8 + tr, :] = zc
    left = zs[7:7 + tr, :]
    right = zs[9:9 + tr, :]
    if grid_shift:
        col = row & (GRID_W - 1)
        left = jnp.where(col > 0, left, 0.0)
        right = jnp.where(col < GRID_W - 1, right, 0.0)
        nb = 0.25 * (up + down + left + right)
    else:
        left = jnp.where(row > 0, left, 0.0)
        right = jnp.where(row < t_total - 1, right, 0.0)
        nb = 0.5 * (left + right)
    zr = zc + (nb - zc) * mu_ref[...]

    r = zr[:, COL_R:COL_R + 512]
    k = zr[:, COL_K:COL_K + 512]
    v = zr[:, COL_V:COL_V + 512]
    lo = zr[:, COL_LORA:COL_LORA + 128]
    gd = zr[:, COL_GD:COL_GD + 128]
    seg = seg_ref[...]

    lane = _iota((1, 128), 1)
    lor = _dot(jnp.where(lane < 64, jnp.tanh(lo), lo), wl_ref[...])
    kkr = k * kk_p[...]
    kk = kkr * lax.rsqrt(_dot(kkr * kkr, seg) + 1e-12)
    outs["r"][0] = r.astype(BF16)
    outs["kk"][0] = kk.astype(BF16)
    outs["v"][0] = v.astype(BF16)
    outs["gout"][0] = _dot(_sigmoid(gd), g2_ref[...]).astype(BF16)
    kh_sum = None
    for d, sfx in ((0, "_f"), (1, "_b")):
        u = lor[:, d * 512:(d + 1) * 512] + w0_ref[d:d + 1, :]
        outs["lw" + sfx][0] = (-LOG_DECAY_SCALE) * _sigmoid(u)
        a = _sigmoid(a0_ref[d:d + 1, :] + lor[:, 1024 + d * 512:1536 + d * 512])
        kh = k * (1.0 + (a - 1.0) * ka_p[...])
        outs["kh" + sfx][0] = kh.astype(BF16)
        outs["kka" + sfx][0] = (kk * a).astype(BF16)
        kh_sum = kh if kh_sum is None else kh_sum + kh
    outs["bonus"][0] = (_dot(r * rk_p[...] * kh_sum, seg) * v).astype(BF16)

    g0 = RW_PAD
    outs["q"][0] = (zg[:, COL_Q - g0:COL_Q - g0 + 256] * (RW_HEAD ** -0.5)).astype(BF16)
    outs["kg"][0] = zg[:, COL_KG - g0:COL_KG - g0 + 256].astype(BF16)
    outs["vg"][0] = zg[:, COL_VG - g0:COL_VG - g0 + 512].astype(BF16)
    gg = zg[:, COL_GG - g0:COL_GG - g0 + 512]
    outs["gate"][0] = (gg * _sigmoid(gg)).astype(BF16)
    gl = _dot(zg[:, COL_GA - g0:COL_GA - g0 + 128], a2_ref[...])
    for d, sfx in ((0, "_f"), (1, "_b")):
        xg = gl[:, d * 256:(d + 1) * 256] + ab_ref[d:d + 1, :]
        outs["la" + sfx][0] = -_softplus(-xg) * (1.0 / GLA_GATE_NORM)


def _front_call(x, sh, sc, grid_shift, pp):
    b, t, d = x.shape
    tr = min(t, 512)
    nt = t // tr
    hb = tr // GRID_W
    nh = t // GRID_W
    full = lambda shape: pl.BlockSpec(shape, lambda i, j: (0,) * len(shape))
    vec = pl.BlockSpec((1, 1, d), lambda i, j: (i, 0, 0))
    in_specs = [pl.BlockSpec((1, tr, d), lambda i, j: (i, j, 0))]
    args = [x]
    if grid_shift:
        in_specs += [
            pl.BlockSpec((1, GRID_W, d), lambda i, j: (i, jnp.maximum(j * hb - 1, 0), 0)),
            pl.BlockSpec((1, GRID_W, d), lambda i, j: (i, jnp.minimum((j + 1) * hb, nh - 1), 0)),
        ]
        args += [x, x]
    in_specs += [vec, vec]
    args += [sh, sc]
    for nm in _FRONT_PARAMS:
        in_specs.append(full(pp[nm].shape))
        args.append(pp[nm])
    out_specs = [pl.BlockSpec((1, tr, w), lambda i, j: (i, j, 0)) for _, w, _ in _FRONT_OUT]
    out_shape = [jax.ShapeDtypeStruct((b, t, w), dt) for _, w, dt in _FRONT_OUT]
    res = pl.pallas_call(
        functools.partial(_front_kernel, grid_shift, t, tr),
        grid=(b, nt),
        in_specs=in_specs,
        out_specs=out_specs,
        out_shape=out_shape,
        scratch_shapes=[pltpu.VMEM((tr + 16, RW_PAD), F32)],
        compiler_params=_params(("arbitrary", "arbitrary")),
        name="front_grid" if grid_shift else "front_seq",
    )(*args)
    return dict(zip([nm for nm, _, _ in _FRONT_OUT], res))


def _bd(x, blk):
    reps = blk.shape[0] // x.shape[0]
    return jnp.where(blk, jnp.concatenate([x] * reps, axis=0), jnp.zeros((), x.dtype))


def _scan_masks(rev, width):
    c = CHUNK
    ri = _iota((c, width), 0)
    jj = _iota((c, width), 1) & (c - 1)
    strict = (jj > ri) if rev else (jj < ri)
    incl = (jj >= ri) if rev else (jj <= ri)
    eye = jj == ri
    a = _iota((c, c), 0)
    bq = _iota((c, c), 1)
    mcum = jnp.where((bq >= a) if rev else (bq <= a), 1.0, 0.0).astype(BF16)
    levels = [strict & ((jj >> (l + 1)) == (ri >> (l + 1))) & ((jj >> l) != (ri >> l)) for l in range(6)]
    return strict, incl, eye, mcum, levels


def _each(f, *lists):
    return [f(*a) for a in zip(*lists)]


def _cumsum(x, mcum):
    hi = x.astype(BF16)
    lo = (x - hi.astype(F32)).astype(BF16)
    return (jnp.dot(mcum, hi, preferred_element_type=F32) + jnp.dot(mcum, lo, preferred_element_type=F32))


def _wkv_masks(rev):
    strict, incl, eye, mcum, levels = _scan_masks(rev, 256)
    one = lambda m: jnp.where(m, 1.0, 0.0)
    return one(strict), one(incl), one(eye), mcum, [one(m) for m in levels]


def _wkv_chains(need_y, ins, hts, masks, blk):
    c = CHUNK
    blk_f, blk_b = blk
    bd = lambda x: jnp.concatenate([x] * 4, axis=0) * blk_b
    revs = [i[0] for i in ins]
    mk = [masks[int(rv)] for rv in revs]
    g = _each(lambda i, m: _cumsum(i[6], m[3]), ins, mk)
    gl = [gg[0:1] if rv else gg[c - 1:c] for gg, rv in zip(g, revs)]
    x2 = _each(lambda i, gg: jnp.concatenate([(i[2] * jnp.exp(gg - i[6])).astype(BF16),
                                              (i[1] * jnp.exp(gg)).astype(BF16)], axis=0), ins, g)
    at = _each(lambda i, gg: (i[3] * jnp.exp(-gg)).astype(BF16), ins, g)
    kt = _each(lambda i, gg: (i[4] * jnp.exp(-gg)).astype(BF16), ins, g)
    sa = _each(lambda x, a: _dot_nt(x, bd(a)), x2, at)
    sk = _each(lambda x, k: _dot_nt(x, bd(k)), x2, kt)
    tm = _each(lambda s, m: m[2] - s[:c] * m[4][0], sa, mk)
    for l in range(1, 6):
        x = _each(lambda t, s, m: _dot(t, bd((s[:c] * m[4][l]).astype(BF16))), tm, sa, mk)
        tm = _each(lambda t, xx: t - _dot(xx, bd(t.astype(BF16))), tm, x)
    xh = _each(_dot, x2, hts)
    vb = [i[5].astype(BF16) for i in ins]
    vbd = [bd(v) for v in vb]
    if need_y:
        lky = _each(lambda s, m, vd: _dot(s * jnp.concatenate([m[0], m[1]], axis=0), vd), sk, mk, vbd)
        lkv = [x[:c] for x in lky]
    else:
        lkv = _each(lambda s, m, vd: _dot(s[:c] * m[0], vd), sk, mk, vbd)
    ub = _each(lambda t, h, lv: (-_dot(t, bd((h[:c] + lv).astype(BF16)))).astype(BF16), tm, xh, lkv)
    upd = _each(lambda u, v, i, gg, gle: _dot_tn(
        jnp.concatenate([(i[3] * jnp.exp(gle - gg)).astype(BF16), (i[4] * jnp.exp(gle - gg)).astype(BF16)], axis=0),
        jnp.concatenate([u, v], axis=0)),
        ub, vb, ins, g, gl)
    ht_new = _each(lambda h, gle, up: h * _column(jnp.exp(gle)) + up * blk_f, hts, gl, upd)
    ys = [None] * len(ins)
    if need_y:
        ya = _each(lambda s, m, u: _dot(s[c:] * m[1], bd(u)), sa, mk, ub)
        ys = _each(lambda h, a, k: h[c:] + a + k[c:], xh, ya, lky)
    return ys, ht_new


SCAN_TB = 512
WKV_NB = 2
GLA_NB = 4


def _wkv_kernel(need_y, nb, tb, *refs):
    fwd, bwd, rest = refs[:6], refs[6:12], refs[12:]
    if need_y:
        h0_ref, yf_ref, yb_ref, hs = rest
    else:
        hout_ref, hs = rest
    c = CHUNK
    nc = tb // c
    w = 256
    j = pl.program_id(1)
    blk_f = jnp.where((_iota((w, w), 0) >> 6) == (_iota((w, w), 1) >> 6), 1.0, 0.0)
    blk = (blk_f, blk_f.astype(BF16))
    masks = (_wkv_masks(False), _wkv_masks(True))

    @pl.when(j == 0)
    def _():
        hs[...] = h0_ref[...] if need_y else jnp.zeros(hs.shape, F32)

    def body(ci, carry):
        where, ins, hts = [], [], []
        for ib in range(nb):
            for g in range(RW_WIDTH // w):
                lanes = slice(g * w, (g + 1) * w)
                for d, views in enumerate((fwd, bwd)):
                    cc = ci if d == 0 else nc - 1 - ci
                    rows = pl.ds(pl.multiple_of(cc * c, c), c)
                    r, kk, v, kh, kka = [ref[ib, rows, lanes].astype(F32) for ref in views[:5]]
                    ins.append((d == 1, r, kk, kka, kh, v, views[5][ib, rows, lanes]))
                    hts.append(hs[ib, g, d])
                    where.append((ib, g, d, rows, lanes))
        ys, hts = _wkv_chains(need_y, ins, hts, masks, blk)
        for (ib, g, d, rows, lanes), y, ht in zip(where, ys, hts):
            hs[ib, g, d] = ht
            if need_y:
                y_ref = yb_ref if d else yf_ref
                y_ref[ib, rows, lanes] = y.astype(BF16)
        return carry

    lax.fori_loop(0, nc, body, 0)

    if not need_y:
        @pl.when(j == pl.num_programs(1) - 1)
        def _():
            hout_ref[...] = hs[...]


def _wkv_call(p, h0):
    need_y = h0 is not None
    b, t, _ = p["r"].shape
    nb = min(WKV_NB, b)
    tb = min(SCAN_TB, t)
    nblk = t // tb
    ng = RW_WIDTH // 256
    fwd = pl.BlockSpec((nb, tb, RW_WIDTH), lambda i, j: (i, j, 0))
    bwd = pl.BlockSpec((nb, tb, RW_WIDTH), lambda i, j: (i, nblk - 1 - j, 0))
    st = pl.BlockSpec((nb, ng, 2, 256, 256), lambda i, j: (i, 0, 0, 0, 0))
    args = [p["r"], p["kk"], p["v"], p["kh_f"], p["kka_f"], p["lw_f"],
            p["r"], p["kk"], p["v"], p["kh_b"], p["kka_b"], p["lw_b"]]
    if need_y:
        in_specs, args = [fwd] * 6 + [bwd] * 6 + [st], args + [h0]
        out_specs = [fwd, bwd]
        out_shape = [jax.ShapeDtypeStruct((b, t, RW_WIDTH), BF16)] * 2
    else:
        in_specs = [fwd] * 6 + [bwd] * 6
        out_specs = st
        out_shape = jax.ShapeDtypeStruct((b, ng, 2, 256, 256), F32)
    return pl.pallas_call(
        functools.partial(_wkv_kernel, need_y, nb, tb),
        grid=(b // nb, nblk),
        in_specs=in_specs,
        out_specs=out_specs,
        out_shape=out_shape,
        scratch_shapes=[pltpu.VMEM((nb, ng, 2, 256, 256), F32)],
        compiler_params=_params(("arbitrary", "arbitrary")),
        name="wkv_y" if need_y else "wkv_state",
    )(*args)


def _column(row):
    return jnp.transpose(jnp.broadcast_to(row, (8, row.shape[1])))[:, 0:1]


def _gla_chains(need_o, ins, sts, masks, blk_k, blk_v, blk_s):
    c = CHUNK
    revs = [i[0] for i in ins]
    mk = [masks[int(rv)] for rv in revs]
    b = _each(lambda i, m: _cumsum(i[4], m[3]), ins, mk)
    bl = [bb[0:1] if rv else bb[c - 1:c] for bb, rv in zip(b, revs)]
    vb = [i[3].astype(BF16) for i in ins]
    upd = _each(lambda v, i, bb, ble: _dot_tn(v, (i[2] * jnp.exp(ble - bb)).astype(BF16)), vb, ins, b, bl)
    st_new = _each(lambda s, ble, up: s * jnp.exp(ble) + jnp.where(blk_s, up, 0.0), sts, bl, upd)
    os_ = [None] * len(ins)
    if need_o:
        q_in = _each(lambda i, bb: (i[1] * jnp.exp(bb)).astype(BF16), ins, b)
        sc = _each(lambda q, i, bb, m: jnp.where(
            m[1], _dot_nt(q, _bd((i[2] * jnp.exp(-bb)).astype(BF16), blk_k)), 0.0), q_in, ins, b, mk)
        oi = _each(lambda s, v: _dot(s, _bd(v, blk_v)), sc, vb)
        ox = _each(_dot_nt, q_in, sts)
        os_ = _each(lambda a, x: a + x, oi, ox)
    return os_, st_new


def _gla_kernel(need_o, nb, tb, *refs):
    fwd, bwd, rest = refs[:4], refs[4:8], refs[8:]
    if need_o:
        s0_ref, of_ref, ob_ref, ss = rest
    else:
        sout_ref, ss = rest
    c = CHUNK
    nc = tb // c
    j = pl.program_id(1)
    blk_k = (_iota((256, 256), 0) >> 6) == (_iota((256, 256), 1) >> 6)
    blk_v = (_iota((256, 512), 0) >> 6) == (_iota((256, 512), 1) >> 7)
    blk_s = (_iota((512, 256), 0) >> 7) == (_iota((512, 256), 1) >> 6)
    masks = (_scan_masks(False, 256), _scan_masks(True, 256))

    @pl.when(j == 0)
    def _():
        ss[...] = s0_ref[...] if need_o else jnp.zeros(ss.shape, F32)

    def body(ci, carry):
        where, ins, sts = [], [], []
        for ib in range(nb):
            for d, views in enumerate((fwd, bwd)):
                cc = ci if d == 0 else nc - 1 - ci
                rows = pl.ds(pl.multiple_of(cc * c, c), c)
                q, k, v = [ref[ib, rows, :].astype(F32) for ref in views[:3]]
                ins.append((d == 1, q, k, v, views[3][ib, rows, :]))
                sts.append(ss[ib, d])
                where.append((ib, d, rows))
        os_, sts = _gla_chains(need_o, ins, sts, masks, blk_k, blk_v, blk_s)
        for (ib, d, rows), o, st in zip(where, os_, sts):
            ss[ib, d] = st
            if need_o:
                o_ref = ob_ref if d else of_ref
                o_ref[ib, rows, :] = o.astype(BF16)
        return carry

    lax.fori_loop(0, nc, body, 0)

    if not need_o:
        @pl.when(j == pl.num_programs(1) - 1)
        def _():
            sout_ref[...] = ss[...]


def _gla_call(p, s0):
    need_o = s0 is not None
    b, t, _ = p["q"].shape
    nb = min(GLA_NB, b)
    tb = min(SCAN_TB, t)
    nblk = t // tb
    fwd = lambda w: pl.BlockSpec((nb, tb, w), lambda i, j: (i, j, 0))
    bwd = lambda w: pl.BlockSpec((nb, tb, w), lambda i, j: (i, nblk - 1 - j, 0))
    st = pl.BlockSpec((nb, 2, GLA_VAL, GLA_KEY), lambda i, j: (i, 0, 0, 0))
    in_specs = [fwd(256), fwd(256), fwd(512), fwd(256), bwd(256), bwd(256), bwd(512), bwd(256)]
    args = [p["q"], p["kg"], p["vg"], p["la_f"], p["q"], p["kg"], p["vg"], p["la_b"]]
    if need_o:
        in_specs, args = in_specs + [st], args + [s0]
        out_specs = [fwd(GLA_VAL), bwd(GLA_VAL)]
        out_shape = [jax.ShapeDtypeStruct((b, t, GLA_VAL), BF16)] * 2
    else:
        out_specs = st
        out_shape = jax.ShapeDtypeStruct((b, 2, GLA_VAL, GLA_KEY), F32)
    return pl.pallas_call(
        functools.partial(_gla_kernel, need_o, nb, tb),
        grid=(b // nb, nblk),
        in_specs=in_specs,
        out_specs=out_specs,
        out_shape=out_shape,
        scratch_shapes=[pltpu.VMEM((nb, 2, GLA_VAL, GLA_KEY), F32)],
        compiler_params=_params(("arbitrary", "arbitrary")),
        name="gla_o" if need_o else "gla_state",
    )(*args)


def _post_kernel(yf_ref, yb_ref, bon_ref, gout_ref, of_ref, ob_ref, gate_ref, x_ref, g1_ref, gnw_ref,
                 gnb_ref, nw_ref, wout_ref, l1g_ref, l1b_ref, seg64_ref, seg128_ref, out_ref):
    y = yf_ref[0].astype(F32) + yb_ref[0].astype(F32)
    seg64 = seg64_ref[...]
    mu = _dot(y, seg64) * (1.0 / RW_HEAD)
    yc = y - mu
    var = _dot(yc * yc, seg64) * (1.0 / RW_HEAD)
    y_n = yc * lax.rsqrt(var + RW_GN_EPS) * gnw_ref[...] + gnb_ref[...]
    rw_out = (y_n + bon_ref[0].astype(F32)) * gout_ref[0].astype(F32)
    o = of_ref[0].astype(F32) + ob_ref[0].astype(F32)
    ms = _dot(o * o, seg128_ref[...]) * (1.0 / GLA_DV)
    gla_out = o * lax.rsqrt(ms + GLA_NORM_EPS) * nw_ref[...] * gate_ref[0].astype(F32)
    mix = jnp.concatenate([rw_out.astype(BF16), gla_out.astype(BF16)], axis=-1)
    proj = jnp.dot(mix, wout_ref[...], preferred_element_type=F32)
    xs = DN_ALPHA * x_ref[0] + g1_ref[0] * proj
    out_ref[0] = _ln(xs) * l1g_ref[...] + l1b_ref[...]


def _post_call(yf, yb, bonus, gout, of, ob, gate, x, g1, pp):
    b, t, d = x.shape
    tm = min(t, 512)
    tile = lambda w: pl.BlockSpec((1, tm, w), lambda i, j: (i, j, 0))
    full = lambda a: pl.BlockSpec(a.shape, lambda i, j: (0,) * a.ndim)
    consts = [pp["gn_w"], pp["gn_b"], pp["gla_nw"], pp["w_out"], pp["ln1_g"], pp["ln1_b"],
              pp["seg64"], pp["seg128"]]
    return pl.pallas_call(
        _post_kernel,
        grid=(b, t // tm),
        in_specs=[tile(512)] * 7 + [tile(d), pl.BlockSpec((1, 1, d), lambda i, j: (i, 0, 0))]
        + [full(a) for a in consts],
        out_specs=tile(d),
        out_shape=jax.ShapeDtypeStruct((b, t, d), F32),
        compiler_params=_params(("arbitrary", "arbitrary")),
        name="post",
    )(yf, yb, bonus, gout, of, ob, gate, x, g1, *consts)


def _prefix_excl(m, su):
    e, t = m.shape
    off = jnp.zeros((e, 1), F32)
    parts = []
    for j in range(t // 128):
        blk = m[:, j * 128:(j + 1) * 128]
        parts.append(_dot(blk, su) + off)
        off = off + jnp.sum(blk, axis=1, keepdims=True)
    return jnp.concatenate(parts, axis=1)


ROUTE_TBK = 256
GATHER_W = 96
COMBINE_W = 128
META_ROWS = 2 * N_EXPERTS + 8


def _select_kernel(cap, gw, cw, x_ref, sh_ref, sc_ref, rw_ref, h2_ref, aff_ref, post_ref, pose_ref, meta_ref):
    t = x_ref.shape[1]
    ne = N_EXPERTS
    h2 = _ln(x_ref[0]) * (1.0 + sc_ref[0]) + sh_ref[0]
    mm = lambda p, q: jnp.dot(p, q, preferred_element_type=F32)
    h_hi = h2.astype(BF16)
    h2_ref[0] = h_hi
    h_lo = (h2 - h_hi.astype(F32)).astype(BF16)
    w_hi = rw_ref[...].astype(BF16)
    w_lo = (rw_ref[...] - w_hi.astype(F32)).astype(BF16)
    logits = mm(h_hi, w_hi) + (mm(h_lo, w_hi) + mm(h_hi, w_lo))
    lane = _iota((1, 128), 1)
    logits = jnp.where(lane < ne, logits, -1e30)
    m = jnp.max(logits, axis=-1, keepdims=True)
    ex = jnp.exp(logits - m)
    aff = ex / jnp.sum(ex, axis=-1, keepdims=True)
    aff_ref[0] = aff
    aff_t = aff.T[0:ne, :]

    def bs(_, lohi):
        lo, hi = lohi
        mid = lo + ((hi - lo) >> 1)
        mid_f = pltpu.bitcast(mid, F32)[:, 0:1]
        cnt = jnp.sum(jnp.where(aff_t >= mid_f, 1.0, 0.0), axis=1, keepdims=True)
        ok = cnt >= cap
        return jnp.where(ok, mid, lo), jnp.where(ok, hi, mid)

    lo0 = jnp.zeros((ne, 128), jnp.int32)
    hi0 = jnp.full((ne, 128), 0x3F800001, jnp.int32)
    thr_bits, _ = lax.fori_loop(0, 31, bs, (lo0, hi0))
    thr = pltpu.bitcast(thr_bits, F32)[:, 0:1]
    gt = jnp.where(aff_t > thr, 1.0, 0.0)
    eq = jnp.where(aff_t == thr, 1.0, 0.0)
    need = cap - jnp.sum(gt, axis=1, keepdims=True)
    su = jnp.where(_iota((128, 128), 0) < _iota((128, 128), 1), 1.0, 0.0).astype(BF16)
    sel = gt + eq * jnp.where(_prefix_excl(eq, su) < need, 1.0, 0.0)
    pos = jnp.where(sel > 0.0, _prefix_excl(sel, su), -1.0)
    pose_ref[0, 0:ne, :] = pos
    pose_ref[0, ne:, :] = jnp.full((128 - ne, t), -1.0, F32)
    post_ref[0] = pose_ref[0].T

    lanes = _iota((ne, 128), 1)
    first = jnp.zeros((ne, 1), F32)
    og = jnp.zeros((ne, 128), F32)
    oc = jnp.zeros((ne, 128), F32)
    fits = jnp.ones((ne, 1), F32)
    for j in range(t // ROUTE_TBK):
        n = jnp.sum(sel[:, j * ROUTE_TBK:(j + 1) * ROUTE_TBK], axis=1, keepdims=True)
        a16 = jnp.floor(first * (1.0 / 16.0)) * 16.0
        sg = jnp.minimum(a16, float(cap - gw))
        sc_ = jnp.minimum(a16, float(cap - cw))
        fits = fits * jnp.where(first - sg + n <= gw, 1.0, 0.0) * jnp.where(first - sc_ + n <= cw, 1.0, 0.0)
        og = jnp.where(lanes == j, sg, og)
        oc = jnp.where(lanes == j, sc_, oc)
        first = first + n
    meta_ref[0, 0:ne, :] = og.astype(jnp.int32)
    meta_ref[0, ne:2 * ne, :] = oc.astype(jnp.int32)
    meta_ref[0, 2 * ne:, :] = jnp.broadcast_to(jnp.min(fits, axis=0, keepdims=True), (8, 128)).astype(jnp.int32)


def _select_call(x1, sh2, sc2, router_pad, cap, gw, cw):
    b, t, d = x1.shape
    row = lambda w: pl.BlockSpec((1, t, w), lambda i: (i, 0, 0))
    vec = pl.BlockSpec((1, 1, d), lambda i: (i, 0, 0))
    return pl.pallas_call(
        functools.partial(_select_kernel, cap, gw, cw),
        grid=(b,),
        in_specs=[row(d), vec, vec, pl.BlockSpec((d, 128), lambda i: (0, 0))],
        out_specs=[row(d), row(128), row(128), pl.BlockSpec((1, 128, t), lambda i: (i, 0, 0)),
                   pl.BlockSpec((1, META_ROWS, 128), lambda i: (i, 0, 0))],
        out_shape=[jax.ShapeDtypeStruct((b, t, d), BF16),
                   jax.ShapeDtypeStruct((b, t, 128), F32),
                   jax.ShapeDtypeStruct((b, t, 128), F32),
                   jax.ShapeDtypeStruct((b, 128, t), F32),
                   jax.ShapeDtypeStruct((b, META_ROWS, 128), jnp.int32)],
        compiler_params=_params(("arbitrary",)),
        name="select",
    )(x1, sh2, sc2, router_pad)


def _gather_kernel(cap, gw, nblk, og_ref, ok_ref, h2_ref, pose_ref, xg_ref):
    b = pl.program_id(0)
    t = h2_ref.shape[1]
    ne = N_EXPERTS

    @pl.when(ok_ref[b] != 0)
    def _():
        xg_ref[...] = jnp.zeros(xg_ref.shape, BF16)
        srow = _iota((gw, 1), 0).astype(F32)
        for j in range(nblk):
            tok = slice(j * ROUTE_TBK, (j + 1) * ROUTE_TBK)
            starts = [pl.multiple_of(og_ref[(b * ne + e) * nblk + j], 16) for e in range(ne)]
            oh = jnp.concatenate(
                [jnp.where(pose_ref[0, e:e + 1, tok] == srow + starts[e].astype(F32), 1.0, 0.0).astype(BF16)
                 for e in range(ne)], axis=0)
            res = jnp.dot(oh, h2_ref[0, tok, :], preferred_element_type=F32).astype(BF16)
            for e in range(ne):
                rows = pl.ds(starts[e], gw)
                xg_ref[0, e, rows, :] = xg_ref[0, e, rows, :] + res[e * gw:(e + 1) * gw]

    @pl.when(ok_ref[b] == 0)
    def _():
        slot = _iota((cap, t), 0).astype(F32)
        for e in range(ne):
            onehot = jnp.where(pose_ref[0, e:e + 1, :] == slot, 1.0, 0.0).astype(BF16)
            xg_ref[0, e] = jnp.dot(onehot, h2_ref[0], preferred_element_type=F32).astype(BF16)


def _gather_call(h2, pos_e, og, ok, cap, gw):
    b, t, d = h2.shape
    nblk = t // ROUTE_TBK
    grid_spec = pltpu.PrefetchScalarGridSpec(
        num_scalar_prefetch=2,
        grid=(b,),
        in_specs=[pl.BlockSpec((1, t, d), lambda i, *_: (i, 0, 0)),
                  pl.BlockSpec((1, 128, t), lambda i, *_: (i, 0, 0))],
        out_specs=pl.BlockSpec((1, N_EXPERTS, cap, d), lambda i, *_: (i, 0, 0, 0)),
    )
    return pl.pallas_call(
        functools.partial(_gather_kernel, cap, gw, nblk),
        grid_spec=grid_spec,
        out_shape=jax.ShapeDtypeStruct((b, N_EXPERTS, cap, d), BF16),
        compiler_params=_params(("arbitrary",)),
        name="gather",
    )(og, ok, h2, pos_e)


def _ffn_kernel(nb, x_ref, wg_ref, wu_ref, wd_ref, y_ref, wg_s, wu_s, wd_s):
    p = pl.program_id(0)
    b = pl.program_id(1)
    ne = pl.num_programs(0) - 1
    rg = wg_s.shape[1] // nb
    rd = wd_s.shape[1] // nb
    og = pl.multiple_of(b * rg, rg)
    od = pl.multiple_of(b * rd, rd)

    for half in (0, 1):
        @pl.when((p < ne) & (p % 2 == half))
        def _():
            wg_s[half, pl.ds(og, rg), :] = wg_ref[0].astype(BF16)
            wu_s[half, pl.ds(og, rg), :] = wu_ref[0].astype(BF16)
            wd_s[half, pl.ds(od, rd), :] = wd_ref[0].astype(BF16)

        @pl.when((p > 0) & (p % 2 != half))
        def _():
            x = x_ref[0, 0]
            g = jnp.dot(x, wg_s[half], preferred_element_type=F32)
            u = jnp.dot(x, wu_s[half], preferred_element_type=F32)
            h = (g * _sigmoid(g) * u).astype(BF16)
            y_ref[0, 0] = jnp.dot(h, wd_s[half], preferred_element_type=F32).astype(BF16)

    @pl.when(p == 0)
    def _():
        y_ref[...] = jnp.zeros(y_ref.shape, BF16)


def _ffn_call(xg, wg, wu, wd):
    b, ne, cap, d = xg.shape
    f = wg.shape[2]
    assert d % (16 * b) == 0 and f % (16 * b) == 0, "weight chunks must be whole bf16 sublane tiles"
    wmap = lambda p, i: (jnp.minimum(p, ne - 1), jnp.where(p < ne, i, b - 1), 0)
    xmap = lambda p, i: (jnp.where(p > 0, i, 0), jnp.maximum(p - 1, 0), 0, 0)
    return pl.pallas_call(
        functools.partial(_ffn_kernel, b),
        grid=(ne + 1, b),
        in_specs=[pl.BlockSpec((1, 1, cap, d), xmap),
                  pl.BlockSpec((1, d // b, f), wmap),
                  pl.BlockSpec((1, d // b, f), wmap),
                  pl.BlockSpec((1, f // b, d), wmap)],
        out_specs=pl.BlockSpec((1, 1, cap, d), xmap),
        out_shape=jax.ShapeDtypeStruct((b, ne, cap, d), BF16),
        scratch_shapes=[pltpu.VMEM((2, d, f), BF16), pltpu.VMEM((2, d, f), BF16), pltpu.VMEM((2, f, d), BF16)],
        compiler_params=_params(("arbitrary", "arbitrary")),
        name="ffn",
    )(xg, wg, wu, wd)


def _combine_kernel(cap, cw, nblk, nsub, oc_ref, ok_ref, y_ref, pos_ref, aff_ref, x_ref, g2_ref, l2g_ref,
                    l2b_ref, out_ref):
    b = pl.program_id(0)
    j = pl.program_id(1)
    ne = N_EXPERTS
    pos = pos_ref[0]
    gates = aff_ref[0]

    def finish(acc, rows):
        xs = DN_ALPHA * x_ref[0, rows, :] + g2_ref[0] * acc
        out_ref[0, rows, :] = _ln(xs) * l2g_ref[...] + l2b_ref[...]

    @pl.when(ok_ref[b] != 0)
    def _():
        lane = _iota((1, cw), 1).astype(F32)
        for jj in range(nsub):
            rows = slice(jj * ROUTE_TBK, (jj + 1) * ROUTE_TBK)
            ohs, ys = [], []
            for e in range(ne):
                a = pl.multiple_of(oc_ref[(b * ne + e) * nblk + j * nsub + jj], 16)
                ohs.append(jnp.where(pos[rows, e:e + 1] == lane + a.astype(F32),
                                     gates[rows, e:e + 1], 0.0).astype(BF16))
                ys.append(y_ref[0, e, pl.ds(a, cw), :])
            finish(jnp.dot(jnp.concatenate(ohs, axis=1), jnp.concatenate(ys, axis=0),
                           preferred_element_type=F32), rows)

    @pl.when(ok_ref[b] == 0)
    def _():
        slot = _iota((1, cap), 1).astype(F32)
        acc = None
        for e in range(ne):
            oh = jnp.where(pos[:, e:e + 1] == slot, 1.0, 0.0).astype(BF16)
            part = jnp.dot(oh, y_ref[0, e], preferred_element_type=F32) * gates[:, e:e + 1]
            acc = part if acc is None else acc + part
        finish(acc, slice(None))


def _combine_call(y, pos_t, aff_t, x1, g2, l2g, l2b, oc, ok, cap, cw):
    b, t, d = x1.shape
    tm = min(t, 2 * ROUTE_TBK)
    nblk = t // ROUTE_TBK
    nsub = tm // ROUTE_TBK
    grid_spec = pltpu.PrefetchScalarGridSpec(
        num_scalar_prefetch=2,
        grid=(b, t // tm),
        in_specs=[pl.BlockSpec((1, N_EXPERTS, cap, d), lambda i, j, *_: (i, 0, 0, 0)),
                  pl.BlockSpec((1, tm, 128), lambda i, j, *_: (i, j, 0)),
                  pl.BlockSpec((1, tm, 128), lambda i, j, *_: (i, j, 0)),
                  pl.BlockSpec((1, tm, d), lambda i, j, *_: (i, j, 0)),
                  pl.BlockSpec((1, 1, d), lambda i, j, *_: (i, 0, 0)),
                  pl.BlockSpec((1, d), lambda i, j, *_: (0, 0)),
                  pl.BlockSpec((1, d), lambda i, j, *_: (0, 0))],
        out_specs=pl.BlockSpec((1, tm, d), lambda i, j, *_: (i, j, 0)),
    )
    return pl.pallas_call(
        functools.partial(_combine_kernel, cap, cw, nblk, nsub),
        grid_spec=grid_spec,
        out_shape=jax.ShapeDtypeStruct((b, t, d), F32),
        compiler_params=_params(("arbitrary", "arbitrary")),
        name="combine",
    )(oc, ok, y, pos_t, aff_t, x1, g2, l2g, l2b)


def _pad_cols(parts, total):
    rows = parts[0][0].shape[0]
    out = jnp.zeros((rows, total), parts[0][0].dtype)
    for a, off in parts:
        out = lax.dynamic_update_slice(out, a, (0, off))
    return out


def _block_ones(n, width):
    i = jnp.arange(n) // width
    return (i[:, None] == i[None, :]).astype(BF16)


def _layout_params(w_in, rw_mu, rw_w0, rw_w2, rw_a0, rw_a2, rw_g2, rw_k_k, rw_k_a, rw_r_k,
                   rw_gn_w, rw_gn_b, gla_a2, gla_a_b, gla_norm_w, w_out, ln1_g, ln1_b):
    rw_in = 1760
    segs = [(0, 1536, COL_R), (1536, 1664, COL_LORA), (1664, 1760, COL_GD),
            (rw_in, rw_in + 256, COL_Q), (rw_in + 256, rw_in + 512, COL_KG),
            (rw_in + 512, rw_in + 1024, COL_VG), (rw_in + 1024, rw_in + 1536, COL_GG),
            (rw_in + 1536, rw_in + 1568, COL_GA)]
    w_pad = _pad_cols([(w_in[:, a:b], off) for a, b, off in segs], IN_PAD).astype(BF16)
    mu = _pad_cols([(rw_mu[None, a:b], off) for a, b, off in segs[:3]], RW_PAD)
    w_lora = jnp.zeros((128, 2048), F32)
    for i, m in enumerate((rw_w2[0], rw_w2[1], rw_a2[0], rw_a2[1])):
        w_lora = lax.dynamic_update_slice(w_lora, m, (32 * i, 512 * i))
    g2 = jnp.zeros((128, 512), F32).at[:96].set(rw_g2)
    a2 = jnp.zeros((128, 512), F32).at[0:16, 0:256].set(gla_a2[0]).at[16:32, 256:512].set(gla_a2[1])
    row = lambda a: a.reshape(1, -1)
    return dict(
        w_in=w_pad, mu=mu, k_k=row(rw_k_k), k_a=row(rw_k_a), r_k=row(rw_r_k), w0=rw_w0, a0=rw_a0,
        ab=gla_a_b, w_lora=w_lora.astype(BF16), g2=g2.astype(BF16), a2=a2.astype(BF16),
        seg64=_block_ones(512, 64), seg128=_block_ones(512, 128),
        gn_w=row(rw_gn_w), gn_b=row(rw_gn_b), gla_nw=row(gla_norm_w), w_out=w_out.astype(BF16),
        ln1_g=row(ln1_g), ln1_b=row(ln1_b))


def kernel(x, c, ctx, c_ctx, ada_w, ada_b, w_in, rw_mu, rw_w0, rw_w2, rw_a0, rw_a2, rw_g2, rw_k_k, rw_k_a, rw_r_k, rw_gn_w, rw_gn_b, gla_a2, gla_a_b, gla_norm_w, w_out, ln1_g, ln1_b, router_w, ex_gate, ex_up, ex_down, ln2_g, ln2_b):
    assert ada_w.shape[0] == 1, "single-layer block"
    b, t, d = x.shape
    cap = CAPACITY_FACTOR * t // N_EXPERTS
    pp = _layout_params(w_in[0], rw_mu[0], rw_w0[0], rw_w2[0], rw_a0[0], rw_a2[0], rw_g2[0],
                        rw_k_k[0], rw_k_a[0], rw_r_k[0], rw_gn_w[0], rw_gn_b[0], gla_a2[0],
                        gla_a_b[0], gla_norm_w[0], w_out[0], ln1_g[0], ln1_b[0])

    rows = -(-(b + 1) // 8) * 8
    cs = jnp.zeros((rows, d), F32).at[:b].set(c).at[b].set(c_ctx)
    mod = _mod_call(cs, ada_w[0], ada_b[0][None])
    sh1, sc1, g1, sh2, sc2, g2 = [m[:, None, :] for m in jnp.split(mod[:b], 6, axis=-1)]
    sh1c, sc1c = [jnp.broadcast_to(m[None, None, :], (b, 1, d)) for m in jnp.split(mod[b], 6)[:2]]

    pc = _front_call(ctx, sh1c, sc1c, False, pp)
    h_ctx = _wkv_call(pc, None)
    s_ctx = _gla_call(pc, None)

    pz = _front_call(x, sh1, sc1, True, pp)
    yf, yb = _wkv_call(pz, h_ctx)
    of, ob = _gla_call(pz, s_ctx)
    x1 = _post_call(yf, yb, pz["bonus"], pz["gout"], of, ob, pz["gate"], x, g1, pp)

    router_pad = jnp.zeros((d, 128), F32).at[:, :N_EXPERTS].set(router_w[0])
    gw, cw = min(GATHER_W, cap), min(COMBINE_W, cap)
    nblk = t // ROUTE_TBK
    h2, aff_t, pos_t, pos_e, meta = _select_call(x1, sh2, sc2, router_pad, cap, gw, cw)
    og = meta[:, 0:N_EXPERTS, 0:nblk].reshape(-1)
    oc = meta[:, N_EXPERTS:2 * N_EXPERTS, 0:nblk].reshape(-1)
    ok = meta[:, 2 * N_EXPERTS, 0]
    xg = _gather_call(h2, pos_e, og, ok, cap, gw)
    ye = _ffn_call(xg, ex_gate[0], ex_up[0], ex_down[0])
    return _combine_call(ye, pos_t, aff_t, x1, g2, ln2_g[0][None], ln2_b[0][None], oc, ok, cap, cw)
```

```python
import functools

import jax
import jax.numpy as jnp
from jax import lax
from jax.experimental import pallas as pl
from jax.experimental.pallas import tpu as pltpu

F32 = jnp.float32
BF16 = jnp.bfloat16
HIGHEST = lax.Precision.HIGHEST

D_MODEL = 1024
GRID_W = 64
RW_WIDTH = 512
RW_HEAD = 64
RW_GN_EPS = 64e-5
GLA_KEY = 256
GLA_VAL = 512
GLA_DV = 128
GLA_GATE_NORM = 16.0
GLA_NORM_EPS = 1e-5
N_EXPERTS = 16
CAPACITY_FACTOR = 2
DN_ALPHA = 2.0 ** 0.25
LN_EPS = 1e-5
CHUNK = 64
LOG_DECAY_SCALE = 0.6065306597126334

COL_R, COL_K, COL_V, COL_LORA, COL_GD = 0, 512, 1024, 1536, 1664
RW_PAD = 1792
COL_Q, COL_KG, COL_VG, COL_GG, COL_GA = 1792, 2048, 2304, 2816, 3328
IN_PAD = 3456

VMEM_LIMIT = 56 * 1024 * 1024


def _params(sem):
    return pltpu.CompilerParams(dimension_semantics=sem, vmem_limit_bytes=VMEM_LIMIT)


def _dot(a, b):
    return jnp.dot(a.astype(BF16), b.astype(BF16), preferred_element_type=F32)


def _dot_nt(a, b):
    return lax.dot_general(a.astype(BF16), b.astype(BF16), (((1,), (1,)), ((), ())),
                           preferred_element_type=F32)


def _dot_tn(a, b):
    return lax.dot_general(a.astype(BF16), b.astype(BF16), (((0,), (0,)), ((), ())),
                           preferred_element_type=F32)


def _sigmoid(x):
    return 1.0 / (1.0 + jnp.exp(-x))


def _softplus(x):
    return jnp.maximum(x, 0.0) + jnp.log(1.0 + jnp.exp(-jnp.abs(x)))


def _ln(x):
    mu = jnp.mean(x, axis=-1, keepdims=True)
    xc = x - mu
    var = jnp.mean(xc * xc, axis=-1, keepdims=True)
    return xc * lax.rsqrt(var + LN_EPS)


def _iota(shape, dim):
    return lax.broadcasted_iota(jnp.int32, shape, dim)


def _mod_kernel(c_ref, w_ref, b_ref, o_ref):
    c = c_ref[...]
    s = c * _sigmoid(c)
    o_ref[...] = jnp.dot(s, w_ref[...], precision=HIGHEST, preferred_element_type=F32) + b_ref[...]


def _mod_call(cs, ada_w, ada_b):
    rows, d = cs.shape
    n = ada_w.shape[1]
    bn = 1536
    return pl.pallas_call(
        _mod_kernel,
        grid=(n // bn,),
        in_specs=[pl.BlockSpec((rows, d), lambda j: (0, 0)),
                  pl.BlockSpec((d, bn), lambda j: (0, j)),
                  pl.BlockSpec((1, bn), lambda j: (0, j))],
        out_specs=pl.BlockSpec((rows, bn), lambda j: (0, j)),
        out_shape=jax.ShapeDtypeStruct((rows, n), F32),
        compiler_params=_params(("arbitrary",)),
        name="mod",
    )(cs, ada_w, ada_b)


_FRONT_OUT = (
    ("r", 512, BF16), ("kk", 512, BF16), ("v", 512, BF16),
    ("kh_f", 512, BF16), ("kh_b", 512, BF16), ("kka_f", 512, BF16), ("kka_b", 512, BF16),
    ("lw_f", 512, F32), ("lw_b", 512, F32), ("gout", 512, BF16), ("bonus", 512, BF16),
    ("q", 256, BF16), ("kg", 256, BF16), ("vg", 512, BF16),
    ("la_f", 256, F32), ("la_b", 256, F32), ("gate", 512, BF16),
)
_FRONT_PARAMS = ("w_in", "mu", "k_k", "k_a", "r_k", "w0", "a0", "ab", "w_lora", "g2", "a2", "seg64")


def _front_kernel(grid_shift, t_total, tr, *refs):
    if grid_shift:
        x_ref, xp_ref, xn_ref = refs[:3]
        refs = refs[3:]
    else:
        x_ref = refs[0]
        refs = refs[1:]
    sh_ref, sc_ref = refs[:2]
    (w_ref, mu_ref, kk_p, ka_p, rk_p, w0_ref, a0_ref, ab_ref, wl_ref, g2_ref, a2_ref, seg_ref) = refs[2:14]
    outs = dict(zip([n for n, _, _ in _FRONT_OUT], refs[14:14 + len(_FRONT_OUT)]))
    zs = refs[14 + len(_FRONT_OUT)]

    i = pl.program_id(1)
    modulate = lambda xx: _ln(xx) * (1.0 + sc_ref[0]) + sh_ref[0]
    hc = modulate(x_ref[0]).astype(BF16)
    row = _iota((tr, 1), 0) + i * tr
    if grid_shift:
        keep_p = jnp.where(i > 0, 1.0, 0.0)
        keep_n = jnp.where(i < t_total // tr - 1, 1.0, 0.0)
        hall = jnp.concatenate([(modulate(xp_ref[0]) * keep_p).astype(BF16), hc,
                                (modulate(xn_ref[0]) * keep_n).astype(BF16)], axis=0)
        zall = jnp.dot(hall, w_ref[:, 0:RW_PAD], preferred_element_type=F32)
        zc = zall[GRID_W:GRID_W + tr]
        up = zall[0:tr]
        down = zall[2 * GRID_W:2 * GRID_W + tr]
    else:
        zc = jnp.dot(hc, w_ref[:, 0:RW_PAD], preferred_element_type=F32)
    zg = jnp.dot(hc, w_ref[:, RW_PAD:IN_PAD], preferred_element_type=F32)
    zs[0:8, :] = jnp.zeros((8, RW_PAD), F32)
    zs[8 + tr:16 + tr, :] = jnp.zeros((8, RW_PAD), F32)
    zs[8:8 + tr, :] = zc
    left = zs[7:7 + tr, :]
    right = zs[9:9 + tr, :]
    if grid_shift:
        col = row & (GRID_W - 1)
        left = jnp.where(col > 0, left, 0.0)
        right = jnp.where(col < GRID_W - 1, right, 0.0)
        nb = 0.25 * (up + down + left + right)
    else:
        left = jnp.where(row > 0, left, 0.0)
        right = jnp.where(row < t_total - 1, right, 0.0)
        nb = 0.5 * (left + right)
    zr = zc + (nb - zc) * mu_ref[...]

    r = zr[:, COL_R:COL_R + 512]
    k = zr[:, COL_K:COL_K + 512]
    v = zr[:, COL_V:COL_V + 512]
    lo = zr[:, COL_LORA:COL_LORA + 128]
    gd = zr[:, COL_GD:COL_GD + 128]
    seg = seg_ref[...]

    lane = _iota((1, 128), 1)
    lor = _dot(jnp.where(lane < 64, jnp.tanh(lo), lo), wl_ref[...])
    kkr = k * kk_p[...]
    kk = kkr * lax.rsqrt(_dot(kkr * kkr, seg) + 1e-12)
    outs["r"][0] = r.astype(BF16)
    outs["kk"][0] = kk.astype(BF16)
    outs["v"][0] = v.astype(BF16)
    outs["gout"][0] = _dot(_sigmoid(gd), g2_ref[...]).astype(BF16)
    kh_sum = None
    for d, sfx in ((0, "_f"), (1, "_b")):
        u = lor[:, d * 512:(d + 1) * 512] + w0_ref[d:d + 1, :]
        outs["lw" + sfx][0] = (-LOG_DECAY_SCALE) * _sigmoid(u)
        a = _sigmoid(a0_ref[d:d + 1, :] + lor[:, 1024 + d * 512:1536 + d * 512])
        kh = k * (1.0 + (a - 1.0) * ka_p[...])
        outs["kh" + sfx][0] = kh.astype(BF16)
        outs["kka" + sfx][0] = (kk * a).astype(BF16)
        kh_sum = kh if kh_sum is None else kh_sum + kh
    outs["bonus"][0] = (_dot(r * rk_p[...] * kh_sum, seg) * v).astype(BF16)

    g0 = RW_PAD
    outs["q"][0] = (zg[:, COL_Q - g0:COL_Q - g0 + 256] * (RW_HEAD ** -0.5)).astype(BF16)
    outs["kg"][0] = zg[:, COL_KG - g0:COL_KG - g0 + 256].astype(BF16)
    outs["vg"][0] = zg[:, COL_VG - g0:COL_VG - g0 + 512].astype(BF16)
    gg = zg[:, COL_GG - g0:COL_GG - g0 + 512]
    outs["gate"][0] = (gg * _sigmoid(gg)).astype(BF16)
    gl = _dot(zg[:, COL_GA - g0:COL_GA - g0 + 128], a2_ref[...])
    for d, sfx in ((0, "_f"), (1, "_b")):
        xg = gl[:, d * 256:(d + 1) * 256] + ab_ref[d:d + 1, :]
        outs["la" + sfx][0] = -_softplus(-xg) * (1.0 / GLA_GATE_NORM)


def _front_call(x, sh, sc, grid_shift, pp):
    b, t, d = x.shape
    tr = min(t, 512)
    nt = t // tr
    hb = tr // GRID_W
    nh = t // GRID_W
    full = lambda shape: pl.BlockSpec(shape, lambda i, j: (0,) * len(shape))
    vec = pl.BlockSpec((1, 1, d), lambda i, j: (i, 0, 0))
    in_specs = [pl.BlockSpec((1, tr, d), lambda i, j: (i, j, 0))]
    args = [x]
    if grid_shift:
        in_specs += [
            pl.BlockSpec((1, GRID_W, d), lambda i, j: (i, jnp.maximum(j * hb - 1, 0), 0)),
            pl.BlockSpec((1, GRID_W, d), lambda i, j: (i, jnp.minimum((j + 1) * hb, nh - 1), 0)),
        ]
        args += [x, x]
    in_specs += [vec, vec]
    args += [sh, sc]
    for nm in _FRONT_PARAMS:
        in_specs.append(full(pp[nm].shape))
        args.append(pp[nm])
    out_specs = [pl.BlockSpec((1, tr, w), lambda i, j: (i, j, 0)) for _, w, _ in _FRONT_OUT]
    out_shape = [jax.ShapeDtypeStruct((b, t, w), dt) for _, w, dt in _FRONT_OUT]
    res = pl.pallas_call(
        functools.partial(_front_kernel, grid_shift, t, tr),
        grid=(b, nt),
        in_specs=in_specs,
        out_specs=out_specs,
        out_shape=out_shape,
        scratch_shapes=[pltpu.VMEM((tr + 16, RW_PAD), F32)],
        compiler_params=_params(("arbitrary", "arbitrary")),
        name="front_grid" if grid_shift else "front_seq",
    )(*args)
    return dict(zip([nm for nm, _, _ in _FRONT_OUT], res))


def _bd(x, blk):
    reps = blk.shape[0] // x.shape[0]
    return jnp.where(blk, jnp.concatenate([x] * reps, axis=0), jnp.zeros((), x.dtype))


def _scan_masks(rev, width):
    c = CHUNK
    ri = _iota((c, width), 0)
    jj = _iota((c, width), 1) & (c - 1)
    strict = (jj > ri) if rev else (jj < ri)
    incl = (jj >= ri) if rev else (jj <= ri)
    eye = jj == ri
    a = _iota((c, c), 0)
    bq = _iota((c, c), 1)
    mcum = jnp.where((bq >= a) if rev else (bq <= a), 1.0, 0.0).astype(BF16)
    levels = [strict & ((jj >> (l + 1)) == (ri >> (l + 1))) & ((jj >> l) != (ri >> l)) for l in range(6)]
    return strict, incl, eye, mcum, levels


def _each(f, *lists):
    return [f(*a) for a in zip(*lists)]


def _cumsum(x, mcum):
    hi = x.astype(BF16)
    lo = (x - hi.astype(F32)).astype(BF16)
    return (jnp.dot(mcum, hi, preferred_element_type=F32) + jnp.dot(mcum, lo, preferred_element_type=F32))


def _wkv_masks(rev):
    strict, incl, eye, mcum, levels = _scan_masks(rev, 256)
    one = lambda m: jnp.where(m, 1.0, 0.0)
    return one(strict), one(incl), one(eye), mcum, [one(m) for m in levels]


def _wkv_chains(need_y, ins, hts, masks, blk):
    c = CHUNK
    blk_f, blk_b = blk
    bd = lambda x: jnp.concatenate([x] * 4, axis=0) * blk_b
    revs = [i[0] for i in ins]
    mk = [masks[int(rv)] for rv in revs]
    g = _each(lambda i, m: _cumsum(i[6], m[3]), ins, mk)
    gl = [gg[0:1] if rv else gg[c - 1:c] for gg, rv in zip(g, revs)]
    x2 = _each(lambda i, gg: jnp.concatenate([(i[2] * jnp.exp(gg - i[6])).astype(BF16),
                                              (i[1] * jnp.exp(gg)).astype(BF16)], axis=0), ins, g)
    at = _each(lambda i, gg: (i[3] * jnp.exp(-gg)).astype(BF16), ins, g)
    kt = _each(lambda i, gg: (i[4] * jnp.exp(-gg)).astype(BF16), ins, g)
    sa = _each(lambda x, a: _dot_nt(x, bd(a)), x2, at)
    sk = _each(lambda x, k: _dot_nt(x, bd(k)), x2, kt)
    tm = _each(lambda s, m: m[2] - s[:c] * m[4][0], sa, mk)
    for l in range(1, 6):
        x = _each(lambda t, s, m: _dot(t, bd((s[:c] * m[4][l]).astype(BF16))), tm, sa, mk)
        tm = _each(lambda t, xx: t - _dot(xx, bd(t.astype(BF16))), tm, x)
    xh = _each(_dot, x2, hts)
    vb = [i[5].astype(BF16) for i in ins]
    vbd = [bd(v) for v in vb]
    if need_y:
        lky = _each(lambda s, m, vd: _dot(s * jnp.concatenate([m[0], m[1]], axis=0), vd), sk, mk, vbd)
        lkv = [x[:c] for x in lky]
    else:
        lkv = _each(lambda s, m, vd: _dot(s[:c] * m[0], vd), sk, mk, vbd)
    ub = _each(lambda t, h, lv: (-_dot(t, bd((h[:c] + lv).astype(BF16)))).astype(BF16), tm, xh, lkv)
    upd = _each(lambda u, v, i, gg, gle: _dot_tn(
        jnp.concatenate([(i[3] * jnp.exp(gle - gg)).astype(BF16), (i[4] * jnp.exp(gle - gg)).astype(BF16)], axis=0),
        jnp.concatenate([u, v], axis=0)),
        ub, vb, ins, g, gl)
    ht_new = _each(lambda h, gle, up: h * _column(jnp.exp(gle)) + up * blk_f, hts, gl, upd)
    ys = [None] * len(ins)
    if need_y:
        ya = _each(lambda s, m, u: _dot(s[c:] * m[1], bd(u)), sa, mk, ub)
        ys = _each(lambda h, a, k: h[c:] + a + k[c:], xh, ya, lky)
    return ys, ht_new


SCAN_TB = 512
WKV_NB = 2
GLA_NB = 4


def _wkv_kernel(need_y, nb, tb, *refs):
    fwd, bwd, rest = refs[:6], refs[6:12], refs[12:]
    if need_y:
        h0_ref, yf_ref, yb_ref, hs = rest
    else:
        hout_ref, hs = rest
    c = CHUNK
    nc = tb // c
    w = 256
    j = pl.program_id(1)
    blk_f = jnp.where((_iota((w, w), 0) >> 6) == (_iota((w, w), 1) >> 6), 1.0, 0.0)
    blk = (blk_f, blk_f.astype(BF16))
    masks = (_wkv_masks(False), _wkv_masks(True))

    @pl.when(j == 0)
    def _():
        hs[...] = h0_ref[...] if need_y else jnp.zeros(hs.shape, F32)

    def body(ci, carry):
        where, ins, hts = [], [], []
        for ib in range(nb):
            for g in range(RW_WIDTH // w):
                lanes = slice(g * w, (g + 1) * w)
                for d, views in enumerate((fwd, bwd)):
                    cc = ci if d == 0 else nc - 1 - ci
                    rows = pl.ds(pl.multiple_of(cc * c, c), c)
                    r, kk, v, kh, kka = [ref[ib, rows, lanes].astype(F32) for ref in views[:5]]
                    ins.append((d == 1, r, kk, kka, kh, v, views[5][ib, rows, lanes]))
                    hts.append(hs[ib, g, d])
                    where.append((ib, g, d, rows, lanes))
        ys, hts = _wkv_chains(need_y, ins, hts, masks, blk)
        for (ib, g, d, rows, lanes), y, ht in zip(where, ys, hts):
            hs[ib, g, d] = ht
            if need_y:
                y_ref = yb_ref if d else yf_ref
                y_ref[ib, rows, lanes] = y.astype(BF16)
        return carry

    lax.fori_loop(0, nc, body, 0)

    if not need_y:
        @pl.when(j == pl.num_programs(1) - 1)
        def _():
            hout_ref[...] = hs[...]


def _wkv_call(p, h0):
    need_y = h0 is not None
    b, t, _ = p["r"].shape
    nb = min(WKV_NB, b)
    tb = min(SCAN_TB, t)
    nblk = t // tb
    ng = RW_WIDTH // 256
    fwd = pl.BlockSpec((nb, tb, RW_WIDTH), lambda i, j: (i, j, 0))
    bwd = pl.BlockSpec((nb, tb, RW_WIDTH), lambda i, j: (i, nblk - 1 - j, 0))
    st = pl.BlockSpec((nb, ng, 2, 256, 256), lambda i, j: (i, 0, 0, 0, 0))
    args = [p["r"], p["kk"], p["v"], p["kh_f"], p["kka_f"], p["lw_f"],
            p["r"], p["kk"], p["v"], p["kh_b"], p["kka_b"], p["lw_b"]]
    if need_y:
        in_specs, args = [fwd] * 6 + [bwd] * 6 + [st], args + [h0]
        out_specs = [fwd, bwd]
        out_shape = [jax.ShapeDtypeStruct((b, t, RW_WIDTH), BF16)] * 2
    else:
        in_specs = [fwd] * 6 + [bwd] * 6
        out_specs = st
        out_shape = jax.ShapeDtypeStruct((b, ng, 2, 256, 256), F32)
    return pl.pallas_call(
        functools.partial(_wkv_kernel, need_y, nb, tb),
        grid=(b // nb, nblk),
        in_specs=in_specs,
        out_specs=out_specs,
        out_shape=out_shape,
        scratch_shapes=[pltpu.VMEM((nb, ng, 2, 256, 256), F32)],
        compiler_params=_params(("arbitrary", "arbitrary")),
        name="wkv_y" if need_y else "wkv_state",
    )(*args)


def _column(row):
    return jnp.transpose(jnp.broadcast_to(row, (8, row.shape[1])))[:, 0:1]


def _gla_chains(need_o, ins, sts, masks, blk_k, blk_v, blk_s):
    c = CHUNK
    revs = [i[0] for i in ins]
    mk = [masks[int(rv)] for rv in revs]
    b = _each(lambda i, m: _cumsum(i[4], m[3]), ins, mk)
    bl = [bb[0:1] if rv else bb[c - 1:c] for bb, rv in zip(b, revs)]
    vb = [i[3].astype(BF16) for i in ins]
    upd = _each(lambda v, i, bb, ble: _dot_tn(v, (i[2] * jnp.exp(ble - bb)).astype(BF16)), vb, ins, b, bl)
    st_new = _each(lambda s, ble, up: s * jnp.exp(ble) + jnp.where(blk_s, up, 0.0), sts, bl, upd)
    os_ = [None] * len(ins)
    if need_o:
        q_in = _each(lambda i, bb: (i[1] * jnp.exp(bb)).astype(BF16), ins, b)
        sc = _each(lambda q, i, bb, m: jnp.where(
            m[1], _dot_nt(q, _bd((i[2] * jnp.exp(-bb)).astype(BF16), blk_k)), 0.0), q_in, ins, b, mk)
        oi = _each(lambda s, v: _dot(s, _bd(v, blk_v)), sc, vb)
        ox = _each(_dot_nt, q_in, sts)
        os_ = _each(lambda a, x: a + x, oi, ox)
    return os_, st_new


def _gla_kernel(need_o, nb, tb, *refs):
    fwd, bwd, rest = refs[:4], refs[4:8], refs[8:]
    if need_o:
        s0_ref, of_ref, ob_ref, ss = rest
    else:
        sout_ref, ss = rest
    c = CHUNK
    nc = tb // c
    j = pl.program_id(1)
    blk_k = (_iota((256, 256), 0) >> 6) == (_iota((256, 256), 1) >> 6)
    blk_v = (_iota((256, 512), 0) >> 6) == (_iota((256, 512), 1) >> 7)
    blk_s = (_iota((512, 256), 0) >> 7) == (_iota((512, 256), 1) >> 6)
    masks = (_scan_masks(False, 256), _scan_masks(True, 256))

    @pl.when(j == 0)
    def _():
        ss[...] = s0_ref[...] if need_o else jnp.zeros(ss.shape, F32)

    def body(ci, carry):
        where, ins, sts = [], [], []
        for ib in range(nb):
            for d, views in enumerate((fwd, bwd)):
                cc = ci if d == 0 else nc - 1 - ci
                rows = pl.ds(pl.multiple_of(cc * c, c), c)
                q, k, v = [ref[ib, rows, :].astype(F32) for ref in views[:3]]
                ins.append((d == 1, q, k, v, views[3][ib, rows, :]))
                sts.append(ss[ib, d])
                where.append((ib, d, rows))
        os_, sts = _gla_chains(need_o, ins, sts, masks, blk_k, blk_v, blk_s)
        for (ib, d, rows), o, st in zip(where, os_, sts):
            ss[ib, d] = st
            if need_o:
                o_ref = ob_ref if d else of_ref
                o_ref[ib, rows, :] = o.astype(BF16)
        return carry

    lax.fori_loop(0, nc, body, 0)

    if not need_o:
        @pl.when(j == pl.num_programs(1) - 1)
        def _():
            sout_ref[...] = ss[...]


def _gla_call(p, s0):
    need_o = s0 is not None
    b, t, _ = p["q"].shape
    nb = min(GLA_NB, b)
    tb = min(SCAN_TB, t)
    nblk = t // tb
    fwd = lambda w: pl.BlockSpec((nb, tb, w), lambda i, j: (i, j, 0))
    bwd = lambda w: pl.BlockSpec((nb, tb, w), lambda i, j: (i, nblk - 1 - j, 0))
    st = pl.BlockSpec((nb, 2, GLA_VAL, GLA_KEY), lambda i, j: (i, 0, 0, 0))
    in_specs = [fwd(256), fwd(256), fwd(512), fwd(256), bwd(256), bwd(256), bwd(512), bwd(256)]
    args = [p["q"], p["kg"], p["vg"], p["la_f"], p["q"], p["kg"], p["vg"], p["la_b"]]
    if need_o:
        in_specs, args = in_specs + [st], args + [s0]
        out_specs = [fwd(GLA_VAL), bwd(GLA_VAL)]
        out_shape = [jax.ShapeDtypeStruct((b, t, GLA_VAL), BF16)] * 2
    else:
        out_specs = st
        out_shape = jax.ShapeDtypeStruct((b, 2, GLA_VAL, GLA_KEY), F32)
    return pl.pallas_call(
        functools.partial(_gla_kernel, need_o, nb, tb),
        grid=(b // nb, nblk),
        in_specs=in_specs,
        out_specs=out_specs,
        out_shape=out_shape,
        scratch_shapes=[pltpu.VMEM((nb, 2, GLA_VAL, GLA_KEY), F32)],
        compiler_params=_params(("arbitrary", "arbitrary")),
        name="gla_o" if need_o else "gla_state",
    )(*args)


def _post_kernel(yf_ref, yb_ref, bon_ref, gout_ref, of_ref, ob_ref, gate_ref, x_ref, g1_ref, gnw_ref,
                 gnb_ref, nw_ref, wout_ref, l1g_ref, l1b_ref, seg64_ref, seg128_ref, out_ref):
    y = yf_ref[0].astype(F32) + yb_ref[0].astype(F32)
    seg64 = seg64_ref[...]
    mu = _dot(y, seg64) * (1.0 / RW_HEAD)
    yc = y - mu
    var = _dot(yc * yc, seg64) * (1.0 / RW_HEAD)
    y_n = yc * lax.rsqrt(var + RW_GN_EPS) * gnw_ref[...] + gnb_ref[...]
    rw_out = (y_n + bon_ref[0].astype(F32)) * gout_ref[0].astype(F32)
    o = of_ref[0].astype(F32) + ob_ref[0].astype(F32)
    ms = _dot(o * o, seg128_ref[...]) * (1.0 / GLA_DV)
    gla_out = o * lax.rsqrt(ms + GLA_NORM_EPS) * nw_ref[...] * gate_ref[0].astype(F32)
    mix = jnp.concatenate([rw_out.astype(BF16), gla_out.astype(BF16)], axis=-1)
    proj = jnp.dot(mix, wout_ref[...], preferred_element_type=F32)
    xs = DN_ALPHA * x_ref[0] + g1_ref[0] * proj
    out_ref[0] = _ln(xs) * l1g_ref[...] + l1b_ref[...]


def _post_call(yf, yb, bonus, gout, of, ob, gate, x, g1, pp):
    b, t, d = x.shape
    tm = min(t, 512)
    tile = lambda w: pl.BlockSpec((1, tm, w), lambda i, j: (i, j, 0))
    full = lambda a: pl.BlockSpec(a.shape, lambda i, j: (0,) * a.ndim)
    consts = [pp["gn_w"], pp["gn_b"], pp["gla_nw"], pp["w_out"], pp["ln1_g"], pp["ln1_b"],
              pp["seg64"], pp["seg128"]]
    return pl.pallas_call(
        _post_kernel,
        grid=(b, t // tm),
        in_specs=[tile(512)] * 7 + [tile(d), pl.BlockSpec((1, 1, d), lambda i, j: (i, 0, 0))]
        + [full(a) for a in consts],
        out_specs=tile(d),
        out_shape=jax.ShapeDtypeStruct((b, t, d), F32),
        compiler_params=_params(("arbitrary", "arbitrary")),
        name="post",
    )(yf, yb, bonus, gout, of, ob, gate, x, g1, *consts)


def _prefix_excl(m, su):
    e, t = m.shape
    off = jnp.zeros((e, 1), F32)
    parts = []
    for j in range(t // 128):
        blk = m[:, j * 128:(j + 1) * 128]
        parts.append(_dot(blk, su) + off)
        off = off + jnp.sum(blk, axis=1, keepdims=True)
    return jnp.concatenate(parts, axis=1)


ROUTE_TBK = 256
GATHER_W = 96
COMBINE_W = 128
META_ROWS = 2 * N_EXPERTS + 8


def _select_kernel(cap, gw, cw, x_ref, sh_ref, sc_ref, rw_ref, h2_ref, aff_ref, post_ref, pose_ref, meta_ref):
    t = x_ref.shape[1]
    ne = N_EXPERTS
    h2 = _ln(x_ref[0]) * (1.0 + sc_ref[0]) + sh_ref[0]
    mm = lambda p, q: jnp.dot(p, q, preferred_element_type=F32)
    h_hi = h2.astype(BF16)
    h2_ref[0] = h_hi
    h_lo = (h2 - h_hi.astype(F32)).astype(BF16)
    w_hi = rw_ref[...].astype(BF16)
    w_lo = (rw_ref[...] - w_hi.astype(F32)).astype(BF16)
    logits = mm(h_hi, w_hi) + (mm(h_lo, w_hi) + mm(h_hi, w_lo))
    lane = _iota((1, 128), 1)
    logits = jnp.where(lane < ne, logits, -1e30)
    m = jnp.max(logits, axis=-1, keepdims=True)
    ex = jnp.exp(logits - m)
    aff = ex / jnp.sum(ex, axis=-1, keepdims=True)
    aff_ref[0] = aff
    aff_t = aff.T[0:ne, :]

    def bs(_, lohi):
        lo, hi = lohi
        mid = lo + ((hi - lo) >> 1)
        mid_f = pltpu.bitcast(mid, F32)[:, 0:1]
        cnt = jnp.sum(jnp.where(aff_t >= mid_f, 1.0, 0.0), axis=1, keepdims=True)
        ok = cnt >= cap
        return jnp.where(ok, mid, lo), jnp.where(ok, hi, mid)

    lo0 = jnp.zeros((ne, 128), jnp.int32)
    hi0 = jnp.full((ne, 128), 0x3F800001, jnp.int32)
    thr_bits, _ = lax.fori_loop(0, 31, bs, (lo0, hi0))
    thr = pltpu.bitcast(thr_bits, F32)[:, 0:1]
    gt = jnp.where(aff_t > thr, 1.0, 0.0)
    eq = jnp.where(aff_t == thr, 1.0, 0.0)
    need = cap - jnp.sum(gt, axis=1, keepdims=True)
    su = jnp.where(_iota((128, 128), 0) < _iota((128, 128), 1), 1.0, 0.0).astype(BF16)
    sel = gt + eq * jnp.where(_prefix_excl(eq, su) < need, 1.0, 0.0)
    pos = jnp.where(sel > 0.0, _prefix_excl(sel, su), -1.0)
    pose_ref[0, 0:ne, :] = pos
    pose_ref[0, ne:, :] = jnp.full((128 - ne, t), -1.0, F32)
    post_ref[0] = pose_ref[0].T

    lanes = _iota((ne, 128), 1)
    first = jnp.zeros((ne, 1), F32)
    og = jnp.zeros((ne, 128), F32)
    oc = jnp.zeros((ne, 128), F32)
    fits = jnp.ones((ne, 1), F32)
    for j in range(t // ROUTE_TBK):
        n = jnp.sum(sel[:, j * ROUTE_TBK:(j + 1) * ROUTE_TBK], axis=1, keepdims=True)
        a16 = jnp.floor(first * (1.0 / 16.0)) * 16.0
        sg = jnp.minimum(a16, float(cap - gw))
        sc_ = jnp.minimum(a16, float(cap - cw))
        fits = fits * jnp.where(first - sg + n <= gw, 1.0, 0.0) * jnp.where(first - sc_ + n <= cw, 1.0, 0.0)
        og = jnp.where(lanes == j, sg, og)
        oc = jnp.where(lanes == j, sc_, oc)
        first = first + n
    meta_ref[0, 0:ne, :] = og.astype(jnp.int32)
    meta_ref[0, ne:2 * ne, :] = oc.astype(jnp.int32)
    meta_ref[0, 2 * ne:, :] = jnp.broadcast_to(jnp.min(fits, axis=0, keepdims=True), (8, 128)).astype(jnp.int32)


def _select_call(x1, sh2, sc2, router_pad, cap, gw, cw):
    b, t, d = x1.shape
    row = lambda w: pl.BlockSpec((1, t, w), lambda i: (i, 0, 0))
    vec = pl.BlockSpec((1, 1, d), lambda i: (i, 0, 0))
    return pl.pallas_call(
        functools.partial(_select_kernel, cap, gw, cw),
        grid=(b,),
        in_specs=[row(d), vec, vec, pl.BlockSpec((d, 128), lambda i: (0, 0))],
        out_specs=[row(d), row(128), row(128), pl.BlockSpec((1, 128, t), lambda i: (i, 0, 0)),
                   pl.BlockSpec((1, META_ROWS, 128), lambda i: (i, 0, 0))],
        out_shape=[jax.ShapeDtypeStruct((b, t, d), BF16),
                   jax.ShapeDtypeStruct((b, t, 128), F32),
                   jax.ShapeDtypeStruct((b, t, 128), F32),
                   jax.ShapeDtypeStruct((b, 128, t), F32),
                   jax.ShapeDtypeStruct((b, META_ROWS, 128), jnp.int32)],
        compiler_params=_params(("arbitrary",)),
        name="select",
    )(x1, sh2, sc2, router_pad)


def _gather_kernel(cap, gw, nblk, og_ref, ok_ref, h2_ref, pose_ref, xg_ref):
    b = pl.program_id(0)
    t = h2_ref.shape[1]
    ne = N_EXPERTS

    @pl.when(ok_ref[b] != 0)
    def _():
        xg_ref[...] = jnp.zeros(xg_ref.shape, BF16)
        srow = _iota((gw, 1), 0).astype(F32)
        for j in range(nblk):
            tok = slice(j * ROUTE_TBK, (j + 1) * ROUTE_TBK)
            starts = [pl.multiple_of(og_ref[(b * ne + e) * nblk + j], 16) for e in range(ne)]
            oh = jnp.concatenate(
                [jnp.where(pose_ref[0, e:e + 1, tok] == srow + starts[e].astype(F32), 1.0, 0.0).astype(BF16)
                 for e in range(ne)], axis=0)
            res = jnp.dot(oh, h2_ref[0, tok, :], preferred_element_type=F32).astype(BF16)
            for e in range(ne):
                rows = pl.ds(starts[e], gw)
                xg_ref[0, e, rows, :] = xg_ref[0, e, rows, :] + res[e * gw:(e + 1) * gw]

    @pl.when(ok_ref[b] == 0)
    def _():
        slot = _iota((cap, t), 0).astype(F32)
        for e in range(ne):
            onehot = jnp.where(pose_ref[0, e:e + 1, :] == slot, 1.0, 0.0).astype(BF16)
            xg_ref[0, e] = jnp.dot(onehot, h2_ref[0], preferred_element_type=F32).astype(BF16)


def _gather_call(h2, pos_e, og, ok, cap, gw):
    b, t, d = h2.shape
    nblk = t // ROUTE_TBK
    grid_spec = pltpu.PrefetchScalarGridSpec(
        num_scalar_prefetch=2,
        grid=(b,),
        in_specs=[pl.BlockSpec((1, t, d), lambda i, *_: (i, 0, 0)),
                  pl.BlockSpec((1, 128, t), lambda i, *_: (i, 0, 0))],
        out_specs=pl.BlockSpec((1, N_EXPERTS, cap, d), lambda i, *_: (i, 0, 0, 0)),
    )
    return pl.pallas_call(
        functools.partial(_gather_kernel, cap, gw, nblk),
        grid_spec=grid_spec,
        out_shape=jax.ShapeDtypeStruct((b, N_EXPERTS, cap, d), BF16),
        compiler_params=_params(("arbitrary",)),
        name="gather",
    )(og, ok, h2, pos_e)


def _ffn_kernel(nb, x_ref, wg_ref, wu_ref, wd_ref, y_ref, wg_s, wu_s, wd_s):
    p = pl.program_id(0)
    b = pl.program_id(1)
    ne = pl.num_programs(0) - 1
    rg = wg_s.shape[1] // nb
    rd = wd_s.shape[1] // nb
    og = pl.multiple_of(b * rg, rg)
    od = pl.multiple_of(b * rd, rd)

    def cast_chunk(half):
        wg_s[half, pl.ds(og, rg), :] = wg_ref[0].astype(BF16)
        wu_s[half, pl.ds(og, rg), :] = wu_ref[0].astype(BF16)
        wd_s[half, pl.ds(od, rd), :] = wd_ref[0].astype(BF16)

    for half in (0, 1):
        @pl.when((p > 0) & (p % 2 != half))
        def _():
            cast_chunk(1 - half)
            x = x_ref[0, 0]
            g = jnp.dot(x, wg_s[half], preferred_element_type=F32)
            u = jnp.dot(x, wu_s[half], preferred_element_type=F32)
            h = (g * _sigmoid(g) * u).astype(BF16)
            y_ref[0, 0] = jnp.dot(h, wd_s[half], preferred_element_type=F32).astype(BF16)

    @pl.when(p == 0)
    def _():
        cast_chunk(0)
        y_ref[...] = jnp.zeros(y_ref.shape, BF16)


def _ffn_call(xg, wg, wu, wd):
    b, ne, cap, d = xg.shape
    f = wg.shape[2]
    assert d % (16 * b) == 0 and f % (16 * b) == 0, "weight chunks must be whole bf16 sublane tiles"
    wmap = lambda p, i: (jnp.minimum(p, ne - 1), jnp.where(p < ne, i, b - 1), 0)
    xmap = lambda p, i: (jnp.where(p > 0, i, 0), jnp.maximum(p - 1, 0), 0, 0)
    return pl.pallas_call(
        functools.partial(_ffn_kernel, b),
        grid=(ne + 1, b),
        in_specs=[pl.BlockSpec((1, 1, cap, d), xmap),
                  pl.BlockSpec((1, d // b, f), wmap),
                  pl.BlockSpec((1, d // b, f), wmap),
                  pl.BlockSpec((1, f // b, d), wmap)],
        out_specs=pl.BlockSpec((1, 1, cap, d), xmap),
        out_shape=jax.ShapeDtypeStruct((b, ne, cap, d), BF16),
        scratch_shapes=[pltpu.VMEM((2, d, f), BF16), pltpu.VMEM((2, d, f), BF16), pltpu.VMEM((2, f, d), BF16)],
        compiler_params=_params(("arbitrary", "arbitrary")),
        name="ffn",
    )(xg, wg, wu, wd)


def _combine_kernel(cap, cw, nblk, nsub, oc_ref, ok_ref, y_ref, pos_ref, aff_ref, x_ref, g2_ref, l2g_ref,
                    l2b_ref, out_ref):
    b = pl.program_id(0)
    j = pl.program_id(1)
    ne = N_EXPERTS
    pos = pos_ref[0]
    gates = aff_ref[0]

    def finish(acc, rows):
        xs = DN_ALPHA * x_ref[0, rows, :] + g2_ref[0] * acc
        out_ref[0, rows, :] = _ln(xs) * l2g_ref[...] + l2b_ref[...]

    @pl.when(ok_ref[b] != 0)
    def _():
        lane = _iota((1, cw), 1).astype(F32)
        for jj in range(nsub):
            rows = slice(jj * ROUTE_TBK, (jj + 1) * ROUTE_TBK)
            ohs, ys = [], []
            for e in range(ne):
                a = pl.multiple_of(oc_ref[(b * ne + e) * nblk + j * nsub + jj], 16)
                ohs.append(jnp.where(pos[rows, e:e + 1] == lane + a.astype(F32),
                                     gates[rows, e:e + 1], 0.0).astype(BF16))
                ys.append(y_ref[0, e, pl.ds(a, cw), :])
            finish(jnp.dot(jnp.concatenate(ohs, axis=1), jnp.concatenate(ys, axis=0),
                           preferred_element_type=F32), rows)

    @pl.when(ok_ref[b] == 0)
    def _():
        slot = _iota((1, cap), 1).astype(F32)
        acc = None
        for e in range(ne):
            oh = jnp.where(pos[:, e:e + 1] == slot, 1.0, 0.0).astype(BF16)
            part = jnp.dot(oh, y_ref[0, e], preferred_element_type=F32) * gates[:, e:e + 1]
            acc = part if acc is None else acc + part
        finish(acc, slice(None))


def _combine_call(y, pos_t, aff_t, x1, g2, l2g, l2b, oc, ok, cap, cw):
    b, t, d = x1.shape
    tm = min(t, 2 * ROUTE_TBK)
    nblk = t // ROUTE_TBK
    nsub = tm // ROUTE_TBK
    grid_spec = pltpu.PrefetchScalarGridSpec(
        num_scalar_prefetch=2,
        grid=(b, t // tm),
        in_specs=[pl.BlockSpec((1, N_EXPERTS, cap, d), lambda i, j, *_: (i, 0, 0, 0)),
                  pl.BlockSpec((1, tm, 128), lambda i, j, *_: (i, j, 0)),
                  pl.BlockSpec((1, tm, 128), lambda i, j, *_: (i, j, 0)),
                  pl.BlockSpec((1, tm, d), lambda i, j, *_: (i, j, 0)),
                  pl.BlockSpec((1, 1, d), lambda i, j, *_: (i, 0, 0)),
                  pl.BlockSpec((1, d), lambda i, j, *_: (0, 0)),
                  pl.BlockSpec((1, d), lambda i, j, *_: (0, 0))],
        out_specs=pl.BlockSpec((1, tm, d), lambda i, j, *_: (i, j, 0)),
    )
    return pl.pallas_call(
        functools.partial(_combine_kernel, cap, cw, nblk, nsub),
        grid_spec=grid_spec,
        out_shape=jax.ShapeDtypeStruct((b, t, d), F32),
        compiler_params=_params(("arbitrary", "arbitrary")),
        name="combine",
    )(oc, ok, y, pos_t, aff_t, x1, g2, l2g, l2b)


def _pad_cols(parts, total):
    rows, dt = parts[0][0].shape[0], parts[0][0].dtype
    cols, pos = [], 0
    for a, off in parts:
        if off > pos:
            cols.append(jnp.zeros((rows, off - pos), dt))
        cols.append(a)
        pos = off + a.shape[1]
    if pos < total:
        cols.append(jnp.zeros((rows, total - pos), dt))
    return jnp.concatenate(cols, axis=1)


def _block_ones(n, width):
    i = jnp.arange(n) // width
    return (i[:, None] == i[None, :]).astype(BF16)


def _layout_params(w_in, rw_mu, rw_w0, rw_w2, rw_a0, rw_a2, rw_g2, rw_k_k, rw_k_a, rw_r_k,
                   rw_gn_w, rw_gn_b, gla_a2, gla_a_b, gla_norm_w, w_out, ln1_g, ln1_b):
    rw_in = 1760
    segs = [(0, 1536, COL_R), (1536, 1664, COL_LORA), (1664, 1760, COL_GD),
            (rw_in, rw_in + 256, COL_Q), (rw_in + 256, rw_in + 512, COL_KG),
            (rw_in + 512, rw_in + 1024, COL_VG), (rw_in + 1024, rw_in + 1536, COL_GG),
            (rw_in + 1536, rw_in + 1568, COL_GA)]
    w_pad = _pad_cols([(w_in[:, a:b], off) for a, b, off in segs], IN_PAD).astype(BF16)
    mu = _pad_cols([(rw_mu[None, a:b], off) for a, b, off in segs[:3]], RW_PAD)
    w_lora = jnp.zeros((128, 2048), F32)
    for i, m in enumerate((rw_w2[0], rw_w2[1], rw_a2[0], rw_a2[1])):
        w_lora = lax.dynamic_update_slice(w_lora, m, (32 * i, 512 * i))
    g2 = jnp.zeros((128, 512), F32).at[:96].set(rw_g2)
    a2 = jnp.zeros((128, 512), F32).at[0:16, 0:256].set(gla_a2[0]).at[16:32, 256:512].set(gla_a2[1])
    row = lambda a: a.reshape(1, -1)
    return dict(
        w_in=w_pad, mu=mu, k_k=row(rw_k_k), k_a=row(rw_k_a), r_k=row(rw_r_k), w0=rw_w0, a0=rw_a0,
        ab=gla_a_b, w_lora=w_lora.astype(BF16), g2=g2.astype(BF16), a2=a2.astype(BF16),
        seg64=_block_ones(512, 64), seg128=_block_ones(512, 128),
        gn_w=row(rw_gn_w), gn_b=row(rw_gn_b), gla_nw=row(gla_norm_w), w_out=w_out.astype(BF16),
        ln1_g=row(ln1_g), ln1_b=row(ln1_b))


def kernel(x, c, ctx, c_ctx, ada_w, ada_b, w_in, rw_mu, rw_w0, rw_w2, rw_a0, rw_a2, rw_g2, rw_k_k, rw_k_a, rw_r_k, rw_gn_w, rw_gn_b, gla_a2, gla_a_b, gla_norm_w, w_out, ln1_g, ln1_b, router_w, ex_gate, ex_up, ex_down, ln2_g, ln2_b):
    assert ada_w.shape[0] == 1, "single-layer block"
    b, t, d = x.shape
    cap = CAPACITY_FACTOR * t // N_EXPERTS
    pp = _layout_params(w_in[0], rw_mu[0], rw_w0[0], rw_w2[0], rw_a0[0], rw_a2[0], rw_g2[0],
                        rw_k_k[0], rw_k_a[0], rw_r_k[0], rw_gn_w[0], rw_gn_b[0], gla_a2[0],
                        gla_a_b[0], gla_norm_w[0], w_out[0], ln1_g[0], ln1_b[0])

    rows = -(-(b + 1) // 8) * 8
    cs = jnp.zeros((rows, d), F32).at[:b].set(c).at[b].set(c_ctx)
    mod = _mod_call(cs, ada_w[0], ada_b[0][None])
    sh1, sc1, g1, sh2, sc2, g2 = [m[:, None, :] for m in jnp.split(mod[:b], 6, axis=-1)]
    sh1c, sc1c = [jnp.broadcast_to(m[None, None, :], (b, 1, d)) for m in jnp.split(mod[b], 6)[:2]]

    pc = _front_call(ctx, sh1c, sc1c, False, pp)
    h_ctx = _wkv_call(pc, None)
    s_ctx = _gla_call(pc, None)

    pz = _front_call(x, sh1, sc1, True, pp)
    yf, yb = _wkv_call(pz, h_ctx)
    of, ob = _gla_call(pz, s_ctx)
    x1 = _post_call(yf, yb, pz["bonus"], pz["gout"], of, ob, pz["gate"], x, g1, pp)

    router_pad = jnp.zeros((d, 128), F32).at[:, :N_EXPERTS].set(router_w[0])
    gw, cw = min(GATHER_W, cap), min(COMBINE_W, cap)
    nblk = t // ROUTE_TBK
    h2, aff_t, pos_t, pos_e, meta = _select_call(x1, sh2, sc2, router_pad, cap, gw, cw)
    og = meta[:, 0:N_EXPERTS, 0:nblk].reshape(-1)
    oc = meta[:, N_EXPERTS:2 * N_EXPERTS, 0:nblk].reshape(-1)
    ok = meta[:, 2 * N_EXPERTS, 0]
    xg = _gather_call(h2, pos_e, og, ok, cap, gw)
    ye = _ffn_call(xg, ex_gate[0], ex_up[0], ex_down[0])
    return _combine_call(ye, pos_t, aff_t, x1, g2, ln2_g[0][None], ln2_b[0][None], oc, ok, cap, cw)
```

```python
import functools

import jax
import jax.numpy as jnp
from jax import lax
from jax.experimental import pallas as pl
from jax.experimental.pallas import tpu as pltpu

F32 = jnp.float32
BF16 = jnp.bfloat16
HIGHEST = lax.Precision.HIGHEST

D_MODEL = 1024
GRID_W = 64
RW_WIDTH = 512
RW_HEAD = 64
RW_GN_EPS = 64e-5
GLA_KEY = 256
GLA_VAL = 512
GLA_DV = 128
GLA_GATE_NORM = 16.0
GLA_NORM_EPS = 1e-5
N_EXPERTS = 16
CAPACITY_FACTOR = 2
DN_ALPHA = 2.0 ** 0.25
LN_EPS = 1e-5
CHUNK = 64
LOG_DECAY_SCALE = 0.6065306597126334

COL_R, COL_K, COL_V, COL_LORA, COL_GD = 0, 512, 1024, 1536, 1664
RW_PAD = 1792
COL_Q, COL_KG, COL_VG, COL_GG, COL_GA = 1792, 2048, 2304, 2816, 3328
IN_PAD = 3456

VMEM_LIMIT = 56 * 1024 * 1024


def _params(sem):
    return pltpu.CompilerParams(dimension_semantics=sem, vmem_limit_bytes=VMEM_LIMIT)


def _dot(a, b):
    return jnp.dot(a.astype(BF16), b.astype(BF16), preferred_element_type=F32)


def _dot_nt(a, b):
    return lax.dot_general(a.astype(BF16), b.astype(BF16), (((1,), (1,)), ((), ())),
                           preferred_element_type=F32)


def _dot_tn(a, b):
    return lax.dot_general(a.astype(BF16), b.astype(BF16), (((0,), (0,)), ((), ())),
                           preferred_element_type=F32)


def _sigmoid(x):
    return 1.0 / (1.0 + jnp.exp(-x))


def _softplus(x):
    return jnp.maximum(x, 0.0) + jnp.log(1.0 + jnp.exp(-jnp.abs(x)))


def _ln(x):
    mu = jnp.mean(x, axis=-1, keepdims=True)
    xc = x - mu
    var = jnp.mean(xc * xc, axis=-1, keepdims=True)
    return xc * lax.rsqrt(var + LN_EPS)


def _iota(shape, dim):
    return lax.broadcasted_iota(jnp.int32, shape, dim)


def _mod_kernel(c_ref, w_ref, b_ref, o_ref):
    c = c_ref[...]
    s = c * _sigmoid(c)
    o_ref[...] = jnp.dot(s, w_ref[...], precision=HIGHEST, preferred_element_type=F32) + b_ref[...]


def _mod_call(cs, ada_w, ada_b):
    rows, d = cs.shape
    n = ada_w.shape[1]
    bn = 1536
    return pl.pallas_call(
        _mod_kernel,
        grid=(n // bn,),
        in_specs=[pl.BlockSpec((rows, d), lambda j: (0, 0)),
                  pl.BlockSpec((d, bn), lambda j: (0, j)),
                  pl.BlockSpec((1, bn), lambda j: (0, j))],
        out_specs=pl.BlockSpec((rows, bn), lambda j: (0, j)),
        out_shape=jax.ShapeDtypeStruct((rows, n), F32),
        compiler_params=_params(("arbitrary",)),
        name="mod",
    )(cs, ada_w, ada_b)


_FRONT_OUT = (
    ("r", 512, BF16), ("kk", 512, BF16), ("v", 512, BF16),
    ("kh_f", 512, BF16), ("kh_b", 512, BF16), ("kka_f", 512, BF16), ("kka_b", 512, BF16),
    ("lw_f", 512, F32), ("lw_b", 512, F32), ("gout", 512, BF16), ("bonus", 512, BF16),
    ("q", 256, BF16), ("kg", 256, BF16), ("vg", 512, BF16),
    ("la_f", 256, F32), ("la_b", 256, F32), ("gate", 512, BF16),
)
_FRONT_PARAMS = ("w_in", "mu", "k_k", "k_a", "r_k", "w0", "a0", "ab", "w_lora", "g2", "a2", "seg64")


def _front_kernel(grid_shift, t_total, tr, *refs):
    if grid_shift:
        x_ref, xp_ref, xn_ref = refs[:3]
        refs = refs[3:]
    else:
        x_ref = refs[0]
        refs = refs[1:]
    sh_ref, sc_ref = refs[:2]
    (w_ref, mu_ref, kk_p, ka_p, rk_p, w0_ref, a0_ref, ab_ref, wl_ref, g2_ref, a2_ref, seg_ref) = refs[2:14]
    outs = dict(zip([n for n, _, _ in _FRONT_OUT], refs[14:14 + len(_FRONT_OUT)]))
    zs = refs[14 + len(_FRONT_OUT)]

    i = pl.program_id(1)
    modulate = lambda xx: _ln(xx) * (1.0 + sc_ref[0]) + sh_ref[0]
    hc = modulate(x_ref[0]).astype(BF16)
    row = _iota((tr, 1), 0) + i * tr
    if grid_shift:
        keep_p = jnp.where(i > 0, 1.0, 0.0)
        keep_n = jnp.where(i < t_total // tr - 1, 1.0, 0.0)
        hall = jnp.concatenate([(modulate(xp_ref[0]) * keep_p).astype(BF16), hc,
                                (modulate(xn_ref[0]) * keep_n).astype(BF16)], axis=0)
        zall = jnp.dot(hall, w_ref[:, 0:RW_PAD], preferred_element_type=F32)
        zc = zall[GRID_W:GRID_W + tr]
        up = zall[0:tr]
        down = zall[2 * GRID_W:2 * GRID_W + tr]
    else:
        zc = jnp.dot(hc, w_ref[:, 0:RW_PAD], preferred_element_type=F32)
    zg = jnp.dot(hc, w_ref[:, RW_PAD:IN_PAD], preferred_element_type=F32)
    zs[0:8, :] = jnp.zeros((8, RW_PAD), F32)
    zs[8 + tr:16 + tr, :] = jnp.zeros((8, RW_PAD), F32)
    zs[8:8 + tr, :] = zc
    left = zs[7:7 + tr, :]
    right = zs[9:9 + tr, :]
    if grid_shift:
        col = row & (GRID_W - 1)
        left = jnp.where(col > 0, left, 0.0)
        right = jnp.where(col < GRID_W - 1, right, 0.0)
        nb = 0.25 * (up + down + left + right)
    else:
        left = jnp.where(row > 0, left, 0.0)
        right = jnp.where(row < t_total - 1, right, 0.0)
        nb = 0.5 * (left + right)
    zr = zc + (nb - zc) * mu_ref[...]

    r = zr[:, COL_R:COL_R + 512]
    k = zr[:, COL_K:COL_K + 512]
    v = zr[:, COL_V:COL_V + 512]
    lo = zr[:, COL_LORA:COL_LORA + 128]
    gd = zr[:, COL_GD:COL_GD + 128]
    seg = seg_ref[...]

    lane = _iota((1, 128), 1)
    lor = _dot(jnp.where(lane < 64, jnp.tanh(lo), lo), wl_ref[...])
    kkr = k * kk_p[...]
    kk = kkr * lax.rsqrt(_dot(kkr * kkr, seg) + 1e-12)
    outs["r"][0] = r.astype(BF16)
    outs["kk"][0] = kk.astype(BF16)
    outs["v"][0] = v.astype(BF16)
    outs["gout"][0] = _dot(_sigmoid(gd), g2_ref[...]).astype(BF16)
    kh_sum = None
    for d, sfx in ((0, "_f"), (1, "_b")):
        u = lor[:, d * 512:(d + 1) * 512] + w0_ref[d:d + 1, :]
        outs["lw" + sfx][0] = (-LOG_DECAY_SCALE) * _sigmoid(u)
        a = _sigmoid(a0_ref[d:d + 1, :] + lor[:, 1024 + d * 512:1536 + d * 512])
        kh = k * (1.0 + (a - 1.0) * ka_p[...])
        outs["kh" + sfx][0] = kh.astype(BF16)
        outs["kka" + sfx][0] = (kk * a).astype(BF16)
        kh_sum = kh if kh_sum is None else kh_sum + kh
    outs["bonus"][0] = (_dot(r * rk_p[...] * kh_sum, seg) * v).astype(BF16)

    g0 = RW_PAD
    outs["q"][0] = (zg[:, COL_Q - g0:COL_Q - g0 + 256] * (RW_HEAD ** -0.5)).astype(BF16)
    outs["kg"][0] = zg[:, COL_KG - g0:COL_KG - g0 + 256].astype(BF16)
    outs["vg"][0] = zg[:, COL_VG - g0:COL_VG - g0 + 512].astype(BF16)
    gg = zg[:, COL_GG - g0:COL_GG - g0 + 512]
    outs["gate"][0] = (gg * _sigmoid(gg)).astype(BF16)
    gl = _dot(zg[:, COL_GA - g0:COL_GA - g0 + 128], a2_ref[...])
    for d, sfx in ((0, "_f"), (1, "_b")):
        xg = gl[:, d * 256:(d + 1) * 256] + ab_ref[d:d + 1, :]
        outs["la" + sfx][0] = -_softplus(-xg) * (1.0 / GLA_GATE_NORM)


def _front_call(x, sh, sc, grid_shift, pp):
    b, t, d = x.shape
    tr = min(t, 512)
    nt = t // tr
    hb = tr // GRID_W
    nh = t // GRID_W
    full = lambda shape: pl.BlockSpec(shape, lambda i, j: (0,) * len(shape))
    vec = pl.BlockSpec((1, 1, d), lambda i, j: (i, 0, 0))
    in_specs = [pl.BlockSpec((1, tr, d), lambda i, j: (i, j, 0))]
    args = [x]
    if grid_shift:
        in_specs += [
            pl.BlockSpec((1, GRID_W, d), lambda i, j: (i, jnp.maximum(j * hb - 1, 0), 0)),
            pl.BlockSpec((1, GRID_W, d), lambda i, j: (i, jnp.minimum((j + 1) * hb, nh - 1), 0)),
        ]
        args += [x, x]
    in_specs += [vec, vec]
    args += [sh, sc]
    for nm in _FRONT_PARAMS:
        in_specs.append(full(pp[nm].shape))
        args.append(pp[nm])
    out_specs = [pl.BlockSpec((1, tr, w), lambda i, j: (i, j, 0)) for _, w, _ in _FRONT_OUT]
    out_shape = [jax.ShapeDtypeStruct((b, t, w), dt) for _, w, dt in _FRONT_OUT]
    res = pl.pallas_call(
        functools.partial(_front_kernel, grid_shift, t, tr),
        grid=(b, nt),
        in_specs=in_specs,
        out_specs=out_specs,
        out_shape=out_shape,
        scratch_shapes=[pltpu.VMEM((tr + 16, RW_PAD), F32)],
        compiler_params=_params(("arbitrary", "arbitrary")),
        name="front_grid" if grid_shift else "front_seq",
    )(*args)
    return dict(zip([nm for nm, _, _ in _FRONT_OUT], res))


def _bd(x, blk):
    reps = blk.shape[0] // x.shape[0]
    return jnp.where(blk, jnp.concatenate([x] * reps, axis=0), jnp.zeros((), x.dtype))


def _scan_masks(rev, width):
    c = CHUNK
    ri = _iota((c, width), 0)
    jj = _iota((c, width), 1) & (c - 1)
    strict = (jj > ri) if rev else (jj < ri)
    incl = (jj >= ri) if rev else (jj <= ri)
    eye = jj == ri
    a = _iota((c, c), 0)
    bq = _iota((c, c), 1)
    mcum = jnp.where((bq >= a) if rev else (bq <= a), 1.0, 0.0).astype(BF16)
    levels = [strict & ((jj >> (l + 1)) == (ri >> (l + 1))) & ((jj >> l) != (ri >> l)) for l in range(6)]
    return strict, incl, eye, mcum, levels


def _each(f, *lists):
    return [f(*a) for a in zip(*lists)]


def _cumsum(x, mcum):
    hi = x.astype(BF16)
    lo = (x - hi.astype(F32)).astype(BF16)
    return (jnp.dot(mcum, hi, preferred_element_type=F32) + jnp.dot(mcum, lo, preferred_element_type=F32))


def _wkv_masks(rev):
    strict, incl, eye, mcum, levels = _scan_masks(rev, 256)
    one = lambda m: jnp.where(m, 1.0, 0.0).astype(BF16)
    both = jnp.concatenate([one(strict), one(incl)], axis=0)
    return both, one(incl), jnp.where(eye, 1.0, 0.0), mcum, [one(m) for m in levels]


def _wkv_chains(need_y, ins, hts, masks, blk):
    c = CHUNK
    blk_f, blk_b = blk
    bd = lambda x: jnp.concatenate([x] * 4, axis=0) * blk_b
    revs = [i[0] for i in ins]
    mk = [masks[int(rv)] for rv in revs]
    g = _each(lambda i, m: _cumsum(i[6], m[3]), ins, mk)
    gl = [gg[0:1] if rv else gg[c - 1:c] for gg, rv in zip(g, revs)]
    x2 = _each(lambda i, gg: jnp.concatenate([(i[2] * jnp.exp(gg - i[6])).astype(BF16),
                                              (i[1] * jnp.exp(gg)).astype(BF16)], axis=0), ins, g)
    at = _each(lambda i, gg: (i[3] * jnp.exp(-gg)).astype(BF16), ins, g)
    kt = _each(lambda i, gg: (i[4] * jnp.exp(-gg)).astype(BF16), ins, g)
    sa = _each(lambda x, a: _dot_nt(x, bd(a)).astype(BF16), x2, at)
    sk = _each(lambda x, k: _dot_nt(x, bd(k)).astype(BF16), x2, kt)
    tm = _each(lambda s, m: m[2] - (s[:c] * m[4][0]).astype(F32), sa, mk)
    for l in range(1, 6):
        x = _each(lambda t, s, m: _dot(t, bd(s[:c] * m[4][l])), tm, sa, mk)
        tm = _each(lambda t, xx: t - _dot(xx, bd(t.astype(BF16))), tm, x)
    xh = _each(_dot, x2, hts)
    vb = [i[5].astype(BF16) for i in ins]
    vbd = [bd(v) for v in vb]
    if need_y:
        lky = _each(lambda s, m, vd: _dot(s * m[0], vd), sk, mk, vbd)
        lkv = [x[:c] for x in lky]
    else:
        lkv = _each(lambda s, m, vd: _dot(s[:c] * m[0][:c], vd), sk, mk, vbd)
    ub = _each(lambda t, h, lv: (-_dot(t, bd((h[:c] + lv).astype(BF16)))).astype(BF16), tm, xh, lkv)
    upd = _each(lambda u, v, i, gg, gle: _dot_tn(
        jnp.concatenate([(i[3] * jnp.exp(gle - gg)).astype(BF16), (i[4] * jnp.exp(gle - gg)).astype(BF16)], axis=0),
        jnp.concatenate([u, v], axis=0)),
        ub, vb, ins, g, gl)
    ht_new = _each(lambda h, gle, up: h * _column(jnp.exp(gle)) + up * blk_f, hts, gl, upd)
    ys = [None] * len(ins)
    if need_y:
        ya = _each(lambda s, m, u: _dot(s[c:] * m[1], bd(u)), sa, mk, ub)
        ys = _each(lambda h, a, k: h[c:] + a + k[c:], xh, ya, lky)
    return ys, ht_new


SCAN_TB = 512
WKV_NB = 2
GLA_NB = 4


def _wkv_kernel(need_y, nb, tb, *refs):
    fwd, bwd, rest = refs[:6], refs[6:12], refs[12:]
    if need_y:
        h0_ref, yf_ref, yb_ref, hs = rest
    else:
        hout_ref, hs = rest
    c = CHUNK
    nc = tb // c
    w = 256
    j = pl.program_id(1)
    blk_f = jnp.where((_iota((w, w), 0) >> 6) == (_iota((w, w), 1) >> 6), 1.0, 0.0)
    blk = (blk_f, blk_f.astype(BF16))
    masks = (_wkv_masks(False), _wkv_masks(True))

    @pl.when(j == 0)
    def _():
        hs[...] = h0_ref[...] if need_y else jnp.zeros(hs.shape, F32)

    def body(ci, carry):
        where, ins, hts = [], [], []
        for ib in range(nb):
            for g in range(RW_WIDTH // w):
                lanes = slice(g * w, (g + 1) * w)
                for d, views in enumerate((fwd, bwd)):
                    cc = ci if d == 0 else nc - 1 - ci
                    rows = pl.ds(pl.multiple_of(cc * c, c), c)
                    r, kk, v, kh, kka = [ref[ib, rows, lanes].astype(F32) for ref in views[:5]]
                    ins.append((d == 1, r, kk, kka, kh, v, views[5][ib, rows, lanes]))
                    hts.append(hs[ib, g, d])
                    where.append((ib, g, d, rows, lanes))
        ys, hts = _wkv_chains(need_y, ins, hts, masks, blk)
        for (ib, g, d, rows, lanes), y, ht in zip(where, ys, hts):
            hs[ib, g, d] = ht
            if need_y:
                y_ref = yb_ref if d else yf_ref
                y_ref[ib, rows, lanes] = y.astype(BF16)
        return carry

    lax.fori_loop(0, nc, body, 0)

    if not need_y:
        @pl.when(j == pl.num_programs(1) - 1)
        def _():
            hout_ref[...] = hs[...]


def _wkv_call(p, h0):
    need_y = h0 is not None
    b, t, _ = p["r"].shape
    nb = min(WKV_NB, b)
    tb = min(SCAN_TB, t)
    nblk = t // tb
    ng = RW_WIDTH // 256
    fwd = pl.BlockSpec((nb, tb, RW_WIDTH), lambda i, j: (i, j, 0))
    bwd = pl.BlockSpec((nb, tb, RW_WIDTH), lambda i, j: (i, nblk - 1 - j, 0))
    st = pl.BlockSpec((nb, ng, 2, 256, 256), lambda i, j: (i, 0, 0, 0, 0))
    args = [p["r"], p["kk"], p["v"], p["kh_f"], p["kka_f"], p["lw_f"],
            p["r"], p["kk"], p["v"], p["kh_b"], p["kka_b"], p["lw_b"]]
    if need_y:
        in_specs, args = [fwd] * 6 + [bwd] * 6 + [st], args + [h0]
        out_specs = [fwd, bwd]
        out_shape = [jax.ShapeDtypeStruct((b, t, RW_WIDTH), BF16)] * 2
    else:
        in_specs = [fwd] * 6 + [bwd] * 6
        out_specs = st
        out_shape = jax.ShapeDtypeStruct((b, ng, 2, 256, 256), F32)
    return pl.pallas_call(
        functools.partial(_wkv_kernel, need_y, nb, tb),
        grid=(b // nb, nblk),
        in_specs=in_specs,
        out_specs=out_specs,
        out_shape=out_shape,
        scratch_shapes=[pltpu.VMEM((nb, ng, 2, 256, 256), F32)],
        compiler_params=_params(("arbitrary", "arbitrary")),
        name="wkv_y" if need_y else "wkv_state",
    )(*args)


def _column(row):
    return jnp.transpose(jnp.broadcast_to(row, (8, row.shape[1])))[:, 0:1]


def _gla_chains(need_o, ins, sts, masks, blk_k, blk_v, blk_s):
    c = CHUNK
    revs = [i[0] for i in ins]
    mk = [masks[int(rv)] for rv in revs]
    b = _each(lambda i, m: _cumsum(i[4], m[3]), ins, mk)
    bl = [bb[0:1] if rv else bb[c - 1:c] for bb, rv in zip(b, revs)]
    vb = [i[3].astype(BF16) for i in ins]
    upd = _each(lambda v, i, bb, ble: _dot_tn(v, (i[2] * jnp.exp(ble - bb)).astype(BF16)), vb, ins, b, bl)
    st_new = _each(lambda s, ble, up: s * jnp.exp(ble) + jnp.where(blk_s, up, 0.0), sts, bl, upd)
    os_ = [None] * len(ins)
    if need_o:
        q_in = _each(lambda i, bb: (i[1] * jnp.exp(bb)).astype(BF16), ins, b)
        sc = _each(lambda q, i, bb, m: jnp.where(
            m[1], _dot_nt(q, _bd((i[2] * jnp.exp(-bb)).astype(BF16), blk_k)), 0.0), q_in, ins, b, mk)
        oi = _each(lambda s, v: _dot(s, _bd(v, blk_v)), sc, vb)
        ox = _each(_dot_nt, q_in, sts)
        os_ = _each(lambda a, x: a + x, oi, ox)
    return os_, st_new


def _gla_kernel(need_o, nb, tb, *refs):
    fwd, bwd, rest = refs[:4], refs[4:8], refs[8:]
    if need_o:
        s0_ref, of_ref, ob_ref, ss = rest
    else:
        sout_ref, ss = rest
    c = CHUNK
    nc = tb // c
    j = pl.program_id(1)
    blk_k = (_iota((256, 256), 0) >> 6) == (_iota((256, 256), 1) >> 6)
    blk_v = (_iota((256, 512), 0) >> 6) == (_iota((256, 512), 1) >> 7)
    blk_s = (_iota((512, 256), 0) >> 7) == (_iota((512, 256), 1) >> 6)
    masks = (_scan_masks(False, 256), _scan_masks(True, 256))

    @pl.when(j == 0)
    def _():
        ss[...] = s0_ref[...] if need_o else jnp.zeros(ss.shape, F32)

    def body(ci, carry):
        where, ins, sts = [], [], []
        for ib in range(nb):
            for d, views in enumerate((fwd, bwd)):
                cc = ci if d == 0 else nc - 1 - ci
                rows = pl.ds(pl.multiple_of(cc * c, c), c)
                q, k, v = [ref[ib, rows, :].astype(F32) for ref in views[:3]]
                ins.append((d == 1, q, k, v, views[3][ib, rows, :]))
                sts.append(ss[ib, d])
                where.append((ib, d, rows))
        os_, sts = _gla_chains(need_o, ins, sts, masks, blk_k, blk_v, blk_s)
        for (ib, d, rows), o, st in zip(where, os_, sts):
            ss[ib, d] = st
            if need_o:
                o_ref = ob_ref if d else of_ref
                o_ref[ib, rows, :] = o.astype(BF16)
        return carry

    lax.fori_loop(0, nc, body, 0)

    if not need_o:
        @pl.when(j == pl.num_programs(1) - 1)
        def _():
            sout_ref[...] = ss[...]


def _gla_call(p, s0):
    need_o = s0 is not None
    b, t, _ = p["q"].shape
    nb = min(GLA_NB, b)
    tb = min(SCAN_TB, t)
    nblk = t // tb
    fwd = lambda w: pl.BlockSpec((nb, tb, w), lambda i, j: (i, j, 0))
    bwd = lambda w: pl.BlockSpec((nb, tb, w), lambda i, j: (i, nblk - 1 - j, 0))
    st = pl.BlockSpec((nb, 2, GLA_VAL, GLA_KEY), lambda i, j: (i, 0, 0, 0))
    in_specs = [fwd(256), fwd(256), fwd(512), fwd(256), bwd(256), bwd(256), bwd(512), bwd(256)]
    args = [p["q"], p["kg"], p["vg"], p["la_f"], p["q"], p["kg"], p["vg"], p["la_b"]]
    if need_o:
        in_specs, args = in_specs + [st], args + [s0]
        out_specs = [fwd(GLA_VAL), bwd(GLA_VAL)]
        out_shape = [jax.ShapeDtypeStruct((b, t, GLA_VAL), BF16)] * 2
    else:
        out_specs = st
        out_shape = jax.ShapeDtypeStruct((b, 2, GLA_VAL, GLA_KEY), F32)
    return pl.pallas_call(
        functools.partial(_gla_kernel, need_o, nb, tb),
        grid=(b // nb, nblk),
        in_specs=in_specs,
        out_specs=out_specs,
        out_shape=out_shape,
        scratch_shapes=[pltpu.VMEM((nb, 2, GLA_VAL, GLA_KEY), F32)],
        compiler_params=_params(("arbitrary", "arbitrary")),
        name="gla_o" if need_o else "gla_state",
    )(*args)


def _post_kernel(yf_ref, yb_ref, bon_ref, gout_ref, of_ref, ob_ref, gate_ref, x_ref, g1_ref, gnw_ref,
                 gnb_ref, nw_ref, wout_ref, l1g_ref, l1b_ref, seg64_ref, seg128_ref, out_ref):
    y = yf_ref[0].astype(F32) + yb_ref[0].astype(F32)
    seg64 = seg64_ref[...]
    mu = _dot(y, seg64) * (1.0 / RW_HEAD)
    yc = y - mu
    var = _dot(yc * yc, seg64) * (1.0 / RW_HEAD)
    y_n = yc * lax.rsqrt(var + RW_GN_EPS) * gnw_ref[...] + gnb_ref[...]
    rw_out = (y_n + bon_ref[0].astype(F32)) * gout_ref[0].astype(F32)
    o = of_ref[0].astype(F32) + ob_ref[0].astype(F32)
    ms = _dot(o * o, seg128_ref[...]) * (1.0 / GLA_DV)
    gla_out = o * lax.rsqrt(ms + GLA_NORM_EPS) * nw_ref[...] * gate_ref[0].astype(F32)
    mix = jnp.concatenate([rw_out.astype(BF16), gla_out.astype(BF16)], axis=-1)
    proj = jnp.dot(mix, wout_ref[...], preferred_element_type=F32)
    xs = DN_ALPHA * x_ref[0] + g1_ref[0] * proj
    out_ref[0] = _ln(xs) * l1g_ref[...] + l1b_ref[...]


def _post_call(yf, yb, bonus, gout, of, ob, gate, x, g1, pp):
    b, t, d = x.shape
    tm = min(t, 512)
    tile = lambda w: pl.BlockSpec((1, tm, w), lambda i, j: (i, j, 0))
    full = lambda a: pl.BlockSpec(a.shape, lambda i, j: (0,) * a.ndim)
    consts = [pp["gn_w"], pp["gn_b"], pp["gla_nw"], pp["w_out"], pp["ln1_g"], pp["ln1_b"],
              pp["seg64"], pp["seg128"]]
    return pl.pallas_call(
        _post_kernel,
        grid=(b, t // tm),
        in_specs=[tile(512)] * 7 + [tile(d), pl.BlockSpec((1, 1, d), lambda i, j: (i, 0, 0))]
        + [full(a) for a in consts],
        out_specs=tile(d),
        out_shape=jax.ShapeDtypeStruct((b, t, d), F32),
        compiler_params=_params(("arbitrary", "arbitrary")),
        name="post",
    )(yf, yb, bonus, gout, of, ob, gate, x, g1, *consts)


def _prefix_excl(m, su):
    e, t = m.shape
    off = jnp.zeros((e, 1), F32)
    parts = []
    for j in range(t // 128):
        blk = m[:, j * 128:(j + 1) * 128]
        parts.append(_dot(blk, su) + off)
        off = off + jnp.sum(blk, axis=1, keepdims=True)
    return jnp.concatenate(parts, axis=1)


ROUTE_TBK = 256
GATHER_W = 96
COMBINE_W = 128
META_ROWS = 2 * N_EXPERTS + 8


def _select_kernel(cap, gw, cw, x_ref, sh_ref, sc_ref, rw_ref, h2_ref, aff_ref, post_ref, pose_ref, meta_ref):
    t = x_ref.shape[1]
    ne = N_EXPERTS
    h2 = _ln(x_ref[0]) * (1.0 + sc_ref[0]) + sh_ref[0]
    mm = lambda p, q: jnp.dot(p, q, preferred_element_type=F32)
    h_hi = h2.astype(BF16)
    h2_ref[0] = h_hi
    h_lo = (h2 - h_hi.astype(F32)).astype(BF16)
    w_hi = rw_ref[...].astype(BF16)
    w_lo = (rw_ref[...] - w_hi.astype(F32)).astype(BF16)
    logits = mm(h_hi, w_hi) + (mm(h_lo, w_hi) + mm(h_hi, w_lo))
    lane = _iota((1, 128), 1)
    logits = jnp.where(lane < ne, logits, -1e30)
    m = jnp.max(logits, axis=-1, keepdims=True)
    ex = jnp.exp(logits - m)
    aff = ex / jnp.sum(ex, axis=-1, keepdims=True)
    aff_ref[0] = aff
    aff_t = aff.T[0:ne, :]

    def bs(_, lohi):
        lo, hi = lohi
        mid = lo + ((hi - lo) >> 1)
        mid_f = pltpu.bitcast(mid, F32)[:, 0:1]
        cnt = jnp.sum(jnp.where(aff_t >= mid_f, 1.0, 0.0), axis=1, keepdims=True)
        ok = cnt >= cap
        return jnp.where(ok, mid, lo), jnp.where(ok, hi, mid)

    lo0 = jnp.zeros((ne, 128), jnp.int32)
    hi0 = jnp.full((ne, 128), 0x3F800001, jnp.int32)
    thr_bits, _ = lax.fori_loop(0, 31, bs, (lo0, hi0))
    thr = pltpu.bitcast(thr_bits, F32)[:, 0:1]
    gt = jnp.where(aff_t > thr, 1.0, 0.0)
    eq = jnp.where(aff_t == thr, 1.0, 0.0)
    need = cap - jnp.sum(gt, axis=1, keepdims=True)
    su = jnp.where(_iota((128, 128), 0) < _iota((128, 128), 1), 1.0, 0.0).astype(BF16)
    sel = gt + eq * jnp.where(_prefix_excl(eq, su) < need, 1.0, 0.0)
    pos = jnp.where(sel > 0.0, _prefix_excl(sel, su), -1.0)
    pose_ref[0, 0:ne, :] = pos
    pose_ref[0, ne:, :] = jnp.full((128 - ne, t), -1.0, F32)
    post_ref[0] = pose_ref[0].T

    lanes = _iota((ne, 128), 1)
    first = jnp.zeros((ne, 1), F32)
    og = jnp.zeros((ne, 128), F32)
    oc = jnp.zeros((ne, 128), F32)
    fits = jnp.ones((ne, 1), F32)
    for j in range(t // ROUTE_TBK):
        n = jnp.sum(sel[:, j * ROUTE_TBK:(j + 1) * ROUTE_TBK], axis=1, keepdims=True)
        a16 = jnp.floor(first * (1.0 / 16.0)) * 16.0
        sg = jnp.minimum(a16, float(cap - gw))
        sc_ = jnp.minimum(a16, float(cap - cw))
        fits = fits * jnp.where(first - sg + n <= gw, 1.0, 0.0) * jnp.where(first - sc_ + n <= cw, 1.0, 0.0)
        og = jnp.where(lanes == j, sg, og)
        oc = jnp.where(lanes == j, sc_, oc)
        first = first + n
    meta_ref[0, 0:ne, :] = og.astype(jnp.int32)
    meta_ref[0, ne:2 * ne, :] = oc.astype(jnp.int32)
    meta_ref[0, 2 * ne:, :] = jnp.broadcast_to(jnp.min(fits, axis=0, keepdims=True), (8, 128)).astype(jnp.int32)


def _select_call(x1, sh2, sc2, router_pad, cap, gw, cw):
    b, t, d = x1.shape
    row = lambda w: pl.BlockSpec((1, t, w), lambda i: (i, 0, 0))
    vec = pl.BlockSpec((1, 1, d), lambda i: (i, 0, 0))
    return pl.pallas_call(
        functools.partial(_select_kernel, cap, gw, cw),
        grid=(b,),
        in_specs=[row(d), vec, vec, pl.BlockSpec((d, 128), lambda i: (0, 0))],
        out_specs=[row(d), row(128), row(128), pl.BlockSpec((1, 128, t), lambda i: (i, 0, 0)),
                   pl.BlockSpec((1, META_ROWS, 128), lambda i: (i, 0, 0))],
        out_shape=[jax.ShapeDtypeStruct((b, t, d), BF16),
                   jax.ShapeDtypeStruct((b, t, 128), F32),
                   jax.ShapeDtypeStruct((b, t, 128), F32),
                   jax.ShapeDtypeStruct((b, 128, t), F32),
                   jax.ShapeDtypeStruct((b, META_ROWS, 128), jnp.int32)],
        compiler_params=_params(("arbitrary",)),
        name="select",
    )(x1, sh2, sc2, router_pad)


def _gather_kernel(cap, gw, nblk, og_ref, ok_ref, h2_ref, pose_ref, xg_ref):
    b = pl.program_id(0)
    t = h2_ref.shape[1]
    ne = N_EXPERTS

    @pl.when(ok_ref[b] != 0)
    def _():
        xg_ref[...] = jnp.zeros(xg_ref.shape, BF16)
        srow = _iota((gw, 1), 0).astype(F32)
        for j in range(nblk):
            tok = slice(j * ROUTE_TBK, (j + 1) * ROUTE_TBK)
            starts = [pl.multiple_of(og_ref[(b * ne + e) * nblk + j], 16) for e in range(ne)]
            oh = jnp.concatenate(
                [jnp.where(pose_ref[0, e:e + 1, tok] == srow + starts[e].astype(F32), 1.0, 0.0).astype(BF16)
                 for e in range(ne)], axis=0)
            res = jnp.dot(oh, h2_ref[0, tok, :], preferred_element_type=F32).astype(BF16)
            for e in range(ne):
                rows = pl.ds(starts[e], gw)
                xg_ref[0, e, rows, :] = xg_ref[0, e, rows, :] + res[e * gw:(e + 1) * gw]

    @pl.when(ok_ref[b] == 0)
    def _():
        slot = _iota((cap, t), 0).astype(F32)
        for e in range(ne):
            onehot = jnp.where(pose_ref[0, e:e + 1, :] == slot, 1.0, 0.0).astype(BF16)
            xg_ref[0, e] = jnp.dot(onehot, h2_ref[0], preferred_element_type=F32).astype(BF16)


def _gather_call(h2, pos_e, og, ok, cap, gw):
    b, t, d = h2.shape
    nblk = t // ROUTE_TBK
    grid_spec = pltpu.PrefetchScalarGridSpec(
        num_scalar_prefetch=2,
        grid=(b,),
        in_specs=[pl.BlockSpec((1, t, d), lambda i, *_: (i, 0, 0)),
                  pl.BlockSpec((1, 128, t), lambda i, *_: (i, 0, 0))],
        out_specs=pl.BlockSpec((1, N_EXPERTS, cap, d), lambda i, *_: (i, 0, 0, 0)),
    )
    return pl.pallas_call(
        functools.partial(_gather_kernel, cap, gw, nblk),
        grid_spec=grid_spec,
        out_shape=jax.ShapeDtypeStruct((b, N_EXPERTS, cap, d), BF16),
        compiler_params=_params(("arbitrary",)),
        name="gather",
    )(og, ok, h2, pos_e)


def _ffn_kernel(nb, x_ref, wg_ref, wu_ref, wd_ref, y_ref, wg_s, wu_s, wd_s):
    p = pl.program_id(0)
    b = pl.program_id(1)
    ne = pl.num_programs(0) - 1
    rg = wg_s.shape[1] // nb
    rd = wd_s.shape[1] // nb
    og = pl.multiple_of(b * rg, rg)
    od = pl.multiple_of(b * rd, rd)

    for half in (0, 1):
        @pl.when((p < ne) & (p % 2 == half))
        def _():
            wg_s[half, pl.ds(og, rg), :] = wg_ref[0].astype(BF16)
            wu_s[half, pl.ds(og, rg), :] = wu_ref[0].astype(BF16)
            wd_s[half, pl.ds(od, rd), :] = wd_ref[0].astype(BF16)

        @pl.when((p > 0) & (p % 2 != half))
        def _():
            x = x_ref[0, 0]
            g = jnp.dot(x, wg_s[half], preferred_element_type=F32)
            u = jnp.dot(x, wu_s[half], preferred_element_type=F32)
            h = (g * _sigmoid(g) * u).astype(BF16)
            y_ref[0, 0] = jnp.dot(h, wd_s[half], preferred_element_type=F32).astype(BF16)

    @pl.when(p == 0)
    def _():
        y_ref[...] = jnp.zeros(y_ref.shape, BF16)


def _ffn_call(xg, wg, wu, wd):
    b, ne, cap, d = xg.shape
    f = wg.shape[2]
    assert d % (16 * b) == 0 and f % (16 * b) == 0, "weight chunks must be whole bf16 sublane tiles"
    wmap = lambda p, i: (jnp.minimum(p, ne - 1), jnp.where(p < ne, i, b - 1), 0)
    xmap = lambda p, i: (jnp.where(p > 0, i, 0), jnp.maximum(p - 1, 0), 0, 0)
    return pl.pallas_call(
        functools.partial(_ffn_kernel, b),
        grid=(ne + 1, b),
        in_specs=[pl.BlockSpec((1, 1, cap, d), xmap),
                  pl.BlockSpec((1, d // b, f), wmap),
                  pl.BlockSpec((1, d // b, f), wmap),
                  pl.BlockSpec((1, f // b, d), wmap)],
        out_specs=pl.BlockSpec((1, 1, cap, d), xmap),
        out_shape=jax.ShapeDtypeStruct((b, ne, cap, d), BF16),
        scratch_shapes=[pltpu.VMEM((2, d, f), BF16), pltpu.VMEM((2, d, f), BF16), pltpu.VMEM((2, f, d), BF16)],
        compiler_params=_params(("arbitrary", "arbitrary")),
        name="ffn",
    )(xg, wg, wu, wd)


def _combine_kernel(cap, cw, nblk, nsub, oc_ref, ok_ref, y_ref, pos_ref, aff_ref, x_ref, g2_ref, l2g_ref,
                    l2b_ref, out_ref):
    b = pl.program_id(0)
    j = pl.program_id(1)
    ne = N_EXPERTS
    pos = pos_ref[0]
    gates = aff_ref[0]

    def finish(acc, rows):
        xs = DN_ALPHA * x_ref[0, rows, :] + g2_ref[0] * acc
        out_ref[0, rows, :] = _ln(xs) * l2g_ref[...] + l2b_ref[...]

    @pl.when(ok_ref[b] != 0)
    def _():
        lane = _iota((1, cw), 1).astype(F32)
        for jj in range(nsub):
            rows = slice(jj * ROUTE_TBK, (jj + 1) * ROUTE_TBK)
            ohs, ys = [], []
            for e in range(ne):
                a = pl.multiple_of(oc_ref[(b * ne + e) * nblk + j * nsub + jj], 16)
                ohs.append(jnp.where(pos[rows, e:e + 1] == lane + a.astype(F32),
                                     gates[rows, e:e + 1], 0.0).astype(BF16))
                ys.append(y_ref[0, e, pl.ds(a, cw), :])
            finish(jnp.dot(jnp.concatenate(ohs, axis=1), jnp.concatenate(ys, axis=0),
                           preferred_element_type=F32), rows)

    @pl.when(ok_ref[b] == 0)
    def _():
        slot = _iota((1, cap), 1).astype(F32)
        acc = None
        for e in range(ne):
            oh = jnp.where(pos[:, e:e + 1] == slot, 1.0, 0.0).astype(BF16)
            part = jnp.dot(oh, y_ref[0, e], preferred_element_type=F32) * gates[:, e:e + 1]
            acc = part if acc is None else acc + part
        finish(acc, slice(None))


def _combine_call(y, pos_t, aff_t, x1, g2, l2g, l2b, oc, ok, cap, cw):
    b, t, d = x1.shape
    tm = min(t, 2 * ROUTE_TBK)
    nblk = t // ROUTE_TBK
    nsub = tm // ROUTE_TBK
    grid_spec = pltpu.PrefetchScalarGridSpec(
        num_scalar_prefetch=2,
        grid=(b, t // tm),
        in_specs=[pl.BlockSpec((1, N_EXPERTS, cap, d), lambda i, j, *_: (i, 0, 0, 0)),
                  pl.BlockSpec((1, tm, 128), lambda i, j, *_: (i, j, 0)),
                  pl.BlockSpec((1, tm, 128), lambda i, j, *_: (i, j, 0)),
                  pl.BlockSpec((1, tm, d), lambda i, j, *_: (i, j, 0)),
                  pl.BlockSpec((1, 1, d), lambda i, j, *_: (i, 0, 0)),
                  pl.BlockSpec((1, d), lambda i, j, *_: (0, 0)),
                  pl.BlockSpec((1, d), lambda i, j, *_: (0, 0))],
        out_specs=pl.BlockSpec((1, tm, d), lambda i, j, *_: (i, j, 0)),
    )
    return pl.pallas_call(
        functools.partial(_combine_kernel, cap, cw, nblk, nsub),
        grid_spec=grid_spec,
        out_shape=jax.ShapeDtypeStruct((b, t, d), F32),
        compiler_params=_params(("arbitrary", "arbitrary")),
        name="combine",
    )(oc, ok, y, pos_t, aff_t, x1, g2, l2g, l2b)


def _pad_cols(parts, total):
    rows = parts[0][0].shape[0]
    out = jnp.zeros((rows, total), parts[0][0].dtype)
    for a, off in parts:
        out = lax.dynamic_update_slice(out, a, (0, off))
    return out


def _block_ones(n, width):
    i = jnp.arange(n) // width
    return (i[:, None] == i[None, :]).astype(BF16)


def _layout_params(w_in, rw_mu, rw_w0, rw_w2, rw_a0, rw_a2, rw_g2, rw_k_k, rw_k_a, rw_r_k,
                   rw_gn_w, rw_gn_b, gla_a2, gla_a_b, gla_norm_w, w_out, ln1_g, ln1_b):
    rw_in = 1760
    segs = [(0, 1536, COL_R), (1536, 1664, COL_LORA), (1664, 1760, COL_GD),
            (rw_in, rw_in + 256, COL_Q), (rw_in + 256, rw_in + 512, COL_KG),
            (rw_in + 512, rw_in + 1024, COL_VG), (rw_in + 1024, rw_in + 1536, COL_GG),
            (rw_in + 1536, rw_in + 1568, COL_GA)]
    w_pad = _pad_cols([(w_in[:, a:b], off) for a, b, off in segs], IN_PAD).astype(BF16)
    mu = _pad_cols([(rw_mu[None, a:b], off) for a, b, off in segs[:3]], RW_PAD)
    w_lora = jnp.zeros((128, 2048), F32)
    for i, m in enumerate((rw_w2[0], rw_w2[1], rw_a2[0], rw_a2[1])):
        w_lora = lax.dynamic_update_slice(w_lora, m, (32 * i, 512 * i))
    g2 = jnp.zeros((128, 512), F32).at[:96].set(rw_g2)
    a2 = jnp.zeros((128, 512), F32).at[0:16, 0:256].set(gla_a2[0]).at[16:32, 256:512].set(gla_a2[1])
    row = lambda a: a.reshape(1, -1)
    return dict(
        w_in=w_pad, mu=mu, k_k=row(rw_k_k), k_a=row(rw_k_a), r_k=row(rw_r_k), w0=rw_w0, a0=rw_a0,
        ab=gla_a_b, w_lora=w_lora.astype(BF16), g2=g2.astype(BF16), a2=a2.astype(BF16),
        seg64=_block_ones(512, 64), seg128=_block_ones(512, 128),
        gn_w=row(rw_gn_w), gn_b=row(rw_gn_b), gla_nw=row(gla_norm_w), w_out=w_out.astype(BF16),
        ln1_g=row(ln1_g), ln1_b=row(ln1_b))


def kernel(x, c, ctx, c_ctx, ada_w, ada_b, w_in, rw_mu, rw_w0, rw_w2, rw_a0, rw_a2, rw_g2, rw_k_k, rw_k_a, rw_r_k, rw_gn_w, rw_gn_b, gla_a2, gla_a_b, gla_norm_w, w_out, ln1_g, ln1_b, router_w, ex_gate, ex_up, ex_down, ln2_g, ln2_b):
    assert ada_w.shape[0] == 1, "single-layer block"
    b, t, d = x.shape
    cap = CAPACITY_FACTOR * t // N_EXPERTS
    pp = _layout_params(w_in[0], rw_mu[0], rw_w0[0], rw_w2[0], rw_a0[0], rw_a2[0], rw_g2[0],
                        rw_k_k[0], rw_k_a[0], rw_r_k[0], rw_gn_w[0], rw_gn_b[0], gla_a2[0],
                        gla_a_b[0], gla_norm_w[0], w_out[0], ln1_g[0], ln1_b[0])

    rows = -(-(b + 1) // 8) * 8
    cs = jnp.zeros((rows, d), F32).at[:b].set(c).at[b].set(c_ctx)
    mod = _mod_call(cs, ada_w[0], ada_b[0][None])
    sh1, sc1, g1, sh2, sc2, g2 = [m[:, None, :] for m in jnp.split(mod[:b], 6, axis=-1)]
    sh1c, sc1c = [jnp.broadcast_to(m[None, None, :], (b, 1, d)) for m in jnp.split(mod[b], 6)[:2]]

    pc = _front_call(ctx, sh1c, sc1c, False, pp)
    h_ctx = _wkv_call(pc, None)
    s_ctx = _gla_call(pc, None)

    pz = _front_call(x, sh1, sc1, True, pp)
    yf, yb = _wkv_call(pz, h_ctx)
    of, ob = _gla_call(pz, s_ctx)
    x1 = _post_call(yf, yb, pz["bonus"], pz["gout"], of, ob, pz["gate"], x, g1, pp)

    router_pad = jnp.zeros((d, 128), F32).at[:, :N_EXPERTS].set(router_w[0])
    gw, cw = min(GATHER_W, cap), min(COMBINE_W, cap)
    nblk = t // ROUTE_TBK
    h2, aff_t, pos_t, pos_e, meta = _select_call(x1, sh2, sc2, router_pad, cap, gw, cw)
    og = meta[:, 0:N_EXPERTS, 0:nblk].reshape(-1)
    oc = meta[:, N_EXPERTS:2 * N_EXPERTS, 0:nblk].reshape(-1)
    ok = meta[:, 2 * N_EXPERTS, 0]
    xg = _gather_call(h2, pos_e, og, ok, cap, gw)
    ye = _ffn_call(xg, ex_gate[0], ex_up[0], ex_down[0])
    return _combine_call(ye, pos_t, aff_t, x1, g2, ln2_g[0][None], ln2_b[0][None], oc, ok, cap, cw)
```

```python
import functools

import jax
import jax.numpy as jnp
from jax import lax
from jax.experimental import pallas as pl
from jax.experimental.pallas import tpu as pltpu

F32 = jnp.float32
BF16 = jnp.bfloat16
HIGHEST = lax.Precision.HIGHEST

D_MODEL = 1024
GRID_W = 64
RW_WIDTH = 512
RW_HEAD = 64
RW_GN_EPS = 64e-5
GLA_KEY = 256
GLA_VAL = 512
GLA_DV = 128
GLA_GATE_NORM = 16.0
GLA_NORM_EPS = 1e-5
N_EXPERTS = 16
CAPACITY_FACTOR = 2
DN_ALPHA = 2.0 ** 0.25
LN_EPS = 1e-5
CHUNK = 64
LOG_DECAY_SCALE = 0.6065306597126334

COL_R, COL_K, COL_V, COL_LORA, COL_GD = 0, 512, 1024, 1536, 1664
RW_PAD = 1792
COL_Q, COL_KG, COL_VG, COL_GG, COL_GA = 1792, 2048, 2304, 2816, 3328
IN_PAD = 3456

VMEM_LIMIT = 56 * 1024 * 1024


def _params(sem):
    return pltpu.CompilerParams(dimension_semantics=sem, vmem_limit_bytes=VMEM_LIMIT)


def _dot(a, b):
    return jnp.dot(a.astype(BF16), b.astype(BF16), preferred_element_type=F32)


def _dot_nt(a, b):
    return lax.dot_general(a.astype(BF16), b.astype(BF16), (((1,), (1,)), ((), ())),
                           preferred_element_type=F32)


def _dot_tn(a, b):
    return lax.dot_general(a.astype(BF16), b.astype(BF16), (((0,), (0,)), ((), ())),
                           preferred_element_type=F32)


def _sigmoid(x):
    return 1.0 / (1.0 + jnp.exp(-x))


def _softplus(x):
    return jnp.maximum(x, 0.0) + jnp.log(1.0 + jnp.exp(-jnp.abs(x)))


def _ln(x):
    mu = jnp.mean(x, axis=-1, keepdims=True)
    xc = x - mu
    var = jnp.mean(xc * xc, axis=-1, keepdims=True)
    return xc * lax.rsqrt(var + LN_EPS)


def _iota(shape, dim):
    return lax.broadcasted_iota(jnp.int32, shape, dim)


def _mod_kernel(c_ref, w_ref, b_ref, o_ref):
    c = c_ref[...]
    s = c * _sigmoid(c)
    o_ref[...] = jnp.dot(s, w_ref[...], precision=HIGHEST, preferred_element_type=F32) + b_ref[...]


def _mod_call(cs, ada_w, ada_b):
    rows, d = cs.shape
    n = ada_w.shape[1]
    bn = 1536
    return pl.pallas_call(
        _mod_kernel,
        grid=(n // bn,),
        in_specs=[pl.BlockSpec((rows, d), lambda j: (0, 0)),
                  pl.BlockSpec((d, bn), lambda j: (0, j)),
                  pl.BlockSpec((1, bn), lambda j: (0, j))],
        out_specs=pl.BlockSpec((rows, bn), lambda j: (0, j)),
        out_shape=jax.ShapeDtypeStruct((rows, n), F32),
        compiler_params=_params(("arbitrary",)),
        name="mod",
    )(cs, ada_w, ada_b)


_FRONT_OUT = (
    ("r", 512, BF16), ("kk", 512, BF16), ("v", 512, BF16),
    ("kh_f", 512, BF16), ("kh_b", 512, BF16), ("kka_f", 512, BF16), ("kka_b", 512, BF16),
    ("lw_f", 512, F32), ("lw_b", 512, F32), ("gout", 512, BF16), ("bonus", 512, BF16),
    ("q", 256, BF16), ("kg", 256, BF16), ("vg", 512, BF16),
    ("la_f", 256, F32), ("la_b", 256, F32), ("gate", 512, BF16),
)
_FRONT_PARAMS = ("w_in", "mu", "k_k", "k_a", "r_k", "w0", "a0", "ab", "w_lora", "g2", "a2", "seg64")


def _front_kernel(grid_shift, t_total, tr, *refs):
    if grid_shift:
        x_ref, xp_ref, xn_ref = refs[:3]
        refs = refs[3:]
    else:
        x_ref = refs[0]
        refs = refs[1:]
    sh_ref, sc_ref = refs[:2]
    (w_ref, mu_ref, kk_p, ka_p, rk_p, w0_ref, a0_ref, ab_ref, wl_ref, g2_ref, a2_ref, seg_ref) = refs[2:14]
    outs = dict(zip([n for n, _, _ in _FRONT_OUT], refs[14:14 + len(_FRONT_OUT)]))
    zs = refs[14 + len(_FRONT_OUT)]

    i = pl.program_id(1)
    modulate = lambda xx: _ln(xx) * (1.0 + sc_ref[0]) + sh_ref[0]
    hc = modulate(x_ref[0]).astype(BF16)
    row = _iota((tr, 1), 0) + i * tr
    if grid_shift:
        keep_p = jnp.where(i > 0, 1.0, 0.0)
        keep_n = jnp.where(i < t_total // tr - 1, 1.0, 0.0)
        hall = jnp.concatenate([(modulate(xp_ref[0]) * keep_p).astype(BF16), hc,
                                (modulate(xn_ref[0]) * keep_n).astype(BF16)], axis=0)
        zall = jnp.dot(hall, w_ref[:, 0:RW_PAD], preferred_element_type=F32)
        zc = zall[GRID_W:GRID_W + tr]
        up = zall[0:tr]
        down = zall[2 * GRID_W:2 * GRID_W + tr]
    else:
        zc = jnp.dot(hc, w_ref[:, 0:RW_PAD], preferred_element_type=F32)
    zg = jnp.dot(hc, w_ref[:, RW_PAD:IN_PAD], preferred_element_type=F32)
    zs[0:8, :] = jnp.zeros((8, RW_PAD), F32)
    zs[8 + tr:16 + tr, :] = jnp.zeros((8, RW_PAD), F32)
    zs[8:8 + tr, :] = zc
    left = zs[7:7 + tr, :]
    right = zs[9:9 + tr, :]
    if grid_shift:
        col = row & (GRID_W - 1)
        left = jnp.where(col > 0, left, 0.0)
        right = jnp.where(col < GRID_W - 1, right, 0.0)
        nb = 0.25 * (up + down + left + right)
    else:
        left = jnp.where(row > 0, left, 0.0)
        right = jnp.where(row < t_total - 1, right, 0.0)
        nb = 0.5 * (left + right)
    zr = zc + (nb - zc) * mu_ref[...]

    r = zr[:, COL_R:COL_R + 512]
    k = zr[:, COL_K:COL_K + 512]
    v = zr[:, COL_V:COL_V + 512]
    lo = zr[:, COL_LORA:COL_LORA + 128]
    gd = zr[:, COL_GD:COL_GD + 128]
    seg = seg_ref[...]

    lane = _iota((1, 128), 1)
    lor = _dot(jnp.where(lane < 64, jnp.tanh(lo), lo), wl_ref[...])
    kkr = k * kk_p[...]
    kk = kkr * lax.rsqrt(_dot(kkr * kkr, seg) + 1e-12)
    outs["r"][0] = r.astype(BF16)
    outs["kk"][0] = kk.astype(BF16)
    outs["v"][0] = v.astype(BF16)
    outs["gout"][0] = _dot(_sigmoid(gd), g2_ref[...]).astype(BF16)
    kh_sum = None
    for d, sfx in ((0, "_f"), (1, "_b")):
        u = lor[:, d * 512:(d + 1) * 512] + w0_ref[d:d + 1, :]
        outs["lw" + sfx][0] = (-LOG_DECAY_SCALE) * _sigmoid(u)
        a = _sigmoid(a0_ref[d:d + 1, :] + lor[:, 1024 + d * 512:1536 + d * 512])
        kh = k * (1.0 + (a - 1.0) * ka_p[...])
        outs["kh" + sfx][0] = kh.astype(BF16)
        outs["kka" + sfx][0] = (kk * a).astype(BF16)
        kh_sum = kh if kh_sum is None else kh_sum + kh
    outs["bonus"][0] = (_dot(r * rk_p[...] * kh_sum, seg) * v).astype(BF16)

    g0 = RW_PAD
    outs["q"][0] = (zg[:, COL_Q - g0:COL_Q - g0 + 256] * (RW_HEAD ** -0.5)).astype(BF16)
    outs["kg"][0] = zg[:, COL_KG - g0:COL_KG - g0 + 256].astype(BF16)
    outs["vg"][0] = zg[:, COL_VG - g0:COL_VG - g0 + 512].astype(BF16)
    gg = zg[:, COL_GG - g0:COL_GG - g0 + 512]
    outs["gate"][0] = (gg * _sigmoid(gg)).astype(BF16)
    gl = _dot(zg[:, COL_GA - g0:COL_GA - g0 + 128], a2_ref[...])
    for d, sfx in ((0, "_f"), (1, "_b")):
        xg = gl[:, d * 256:(d + 1) * 256] + ab_ref[d:d + 1, :]
        outs["la" + sfx][0] = -_softplus(-xg) * (1.0 / GLA_GATE_NORM)


def _front_call(x, sh, sc, grid_shift, pp):
    b, t, d = x.shape
    tr = min(t, 512)
    nt = t // tr
    hb = tr // GRID_W
    nh = t // GRID_W
    full = lambda shape: pl.BlockSpec(shape, lambda i, j: (0,) * len(shape))
    vec = pl.BlockSpec((1, 1, d), lambda i, j: (i, 0, 0))
    in_specs = [pl.BlockSpec((1, tr, d), lambda i, j: (i, j, 0))]
    args = [x]
    if grid_shift:
        in_specs += [
            pl.BlockSpec((1, GRID_W, d), lambda i, j: (i, jnp.maximum(j * hb - 1, 0), 0)),
            pl.BlockSpec((1, GRID_W, d), lambda i, j: (i, jnp.minimum((j + 1) * hb, nh - 1), 0)),
        ]
        args += [x, x]
    in_specs += [vec, vec]
    args += [sh, sc]
    for nm in _FRONT_PARAMS:
        in_specs.append(full(pp[nm].shape))
        args.append(pp[nm])
    out_specs = [pl.BlockSpec((1, tr, w), lambda i, j: (i, j, 0)) for _, w, _ in _FRONT_OUT]
    out_shape = [jax.ShapeDtypeStruct((b, t, w), dt) for _, w, dt in _FRONT_OUT]
    res = pl.pallas_call(
        functools.partial(_front_kernel, grid_shift, t, tr),
        grid=(b, nt),
        in_specs=in_specs,
        out_specs=out_specs,
        out_shape=out_shape,
        scratch_shapes=[pltpu.VMEM((tr + 16, RW_PAD), F32)],
        compiler_params=_params(("arbitrary", "arbitrary")),
        name="front_grid" if grid_shift else "front_seq",
    )(*args)
    return dict(zip([nm for nm, _, _ in _FRONT_OUT], res))


def _bd(x, blk):
    reps = blk.shape[0] // x.shape[0]
    return jnp.where(blk, jnp.concatenate([x] * reps, axis=0), jnp.zeros((), x.dtype))


def _scan_masks(rev, width):
    c = CHUNK
    ri = _iota((c, width), 0)
    jj = _iota((c, width), 1) & (c - 1)
    strict = (jj > ri) if rev else (jj < ri)
    incl = (jj >= ri) if rev else (jj <= ri)
    eye = jj == ri
    a = _iota((c, c), 0)
    bq = _iota((c, c), 1)
    mcum = jnp.where((bq >= a) if rev else (bq <= a), 1.0, 0.0).astype(BF16)
    levels = [strict & ((jj >> (l + 1)) == (ri >> (l + 1))) & ((jj >> l) != (ri >> l)) for l in range(6)]
    return strict, incl, eye, mcum, levels


def _each(f, *lists):
    return [f(*a) for a in zip(*lists)]


def _cumsum(x, mcum):
    hi = x.astype(BF16)
    lo = (x - hi.astype(F32)).astype(BF16)
    return (jnp.dot(mcum, hi, preferred_element_type=F32) + jnp.dot(mcum, lo, preferred_element_type=F32))


def _wkv_masks(rev):
    strict, incl, eye, mcum, levels = _scan_masks(rev, 256)
    one = lambda m: jnp.where(m, 1.0, 0.0).astype(BF16)
    both = jnp.concatenate([one(strict), one(incl)], axis=0)
    return both, one(incl), jnp.where(eye, 1.0, 0.0), mcum, [one(m) for m in levels]


def _wkv_chains(need_y, ins, hts, masks, blk):
    c = CHUNK
    blk_f, blk_b = blk
    bd = lambda x: jnp.concatenate([x] * 4, axis=0) * blk_b
    revs = [i[0] for i in ins]
    mk = [masks[int(rv)] for rv in revs]
    g = _each(lambda i, m: _cumsum(i[6], m[3]), ins, mk)
    gl = [gg[0:1] if rv else gg[c - 1:c] for gg, rv in zip(g, revs)]
    x2 = _each(lambda i, gg: jnp.concatenate([(i[2] * jnp.exp(gg - i[6])).astype(BF16),
                                              (i[1] * jnp.exp(gg)).astype(BF16)], axis=0), ins, g)
    at = _each(lambda i, gg: (i[3] * jnp.exp(-gg)).astype(BF16), ins, g)
    kt = _each(lambda i, gg: (i[4] * jnp.exp(-gg)).astype(BF16), ins, g)
    sa = _each(lambda x, a: _dot_nt(x, bd(a)).astype(BF16), x2, at)
    sk = _each(lambda x, k: _dot_nt(x, bd(k)).astype(BF16), x2, kt)
    tm = _each(lambda s, m: m[2] - (s[:c] * m[4][0]).astype(F32), sa, mk)
    for l in range(1, 6):
        x = _each(lambda t, s, m: _dot(t, bd(s[:c] * m[4][l])), tm, sa, mk)
        tm = _each(lambda t, xx: t - _dot(xx, bd(t.astype(BF16))), tm, x)
    xh = _each(_dot, x2, hts)
    vb = [i[5].astype(BF16) for i in ins]
    vbd = [bd(v) for v in vb]
    if need_y:
        lky = _each(lambda s, m, vd: _dot(s * m[0], vd), sk, mk, vbd)
        lkv = [x[:c] for x in lky]
    else:
        lkv = _each(lambda s, m, vd: _dot(s[:c] * m[0][:c], vd), sk, mk, vbd)
    ub = _each(lambda t, h, lv: (-_dot(t, bd((h[:c] + lv).astype(BF16)))).astype(BF16), tm, xh, lkv)
    upd = _each(lambda u, v, i, gg, gle: _dot_tn(
        jnp.concatenate([(i[3] * jnp.exp(gle - gg)).astype(BF16), (i[4] * jnp.exp(gle - gg)).astype(BF16)], axis=0),
        jnp.concatenate([u, v], axis=0)),
        ub, vb, ins, g, gl)
    ht_new = _each(lambda h, gle, up: h * _column(jnp.exp(gle)) + up * blk_f, hts, gl, upd)
    ys = [None] * len(ins)
    if need_y:
        ya = _each(lambda s, m, u: _dot(s[c:] * m[1], bd(u)), sa, mk, ub)
        ys = _each(lambda h, a, k: h[c:] + a + k[c:], xh, ya, lky)
    return ys, ht_new


SCAN_TB = 512
WKV_NB = 2
GLA_NB = 4


def _wkv_kernel(need_y, nb, tb, *refs):
    fwd, bwd, rest = refs[:6], refs[6:12], refs[12:]
    if need_y:
        h0_ref, yf_ref, yb_ref, hs = rest
    else:
        hout_ref, hs = rest
    c = CHUNK
    nc = tb // c
    w = 256
    j = pl.program_id(1)
    blk_f = jnp.where((_iota((w, w), 0) >> 6) == (_iota((w, w), 1) >> 6), 1.0, 0.0)
    blk = (blk_f, blk_f.astype(BF16))
    masks = (_wkv_masks(False), _wkv_masks(True))

    @pl.when(j == 0)
    def _():
        hs[...] = h0_ref[...] if need_y else jnp.zeros(hs.shape, F32)

    def body(ci, carry):
        where, ins, hts = [], [], []
        for ib in range(nb):
            for g in range(RW_WIDTH // w):
                lanes = slice(g * w, (g + 1) * w)
                for d, views in enumerate((fwd, bwd)):
                    cc = ci if d == 0 else nc - 1 - ci
                    rows = pl.ds(pl.multiple_of(cc * c, c), c)
                    r, kk, v, kh, kka = [ref[ib, rows, lanes].astype(F32) for ref in views[:5]]
                    ins.append((d == 1, r, kk, kka, kh, v, views[5][ib, rows, lanes]))
                    hts.append(hs[ib, g, d])
                    where.append((ib, g, d, rows, lanes))
        ys, hts = _wkv_chains(need_y, ins, hts, masks, blk)
        for (ib, g, d, rows, lanes), y, ht in zip(where, ys, hts):
            hs[ib, g, d] = ht
            if need_y:
                y_ref = yb_ref if d else yf_ref
                y_ref[ib, rows, lanes] = y.astype(BF16)
        return carry

    lax.fori_loop(0, nc, body, 0)

    if not need_y:
        @pl.when(j == pl.num_programs(1) - 1)
        def _():
            hout_ref[...] = hs[...]


def _wkv_call(p, h0):
    need_y = h0 is not None
    b, t, _ = p["r"].shape
    nb = min(WKV_NB, b)
    tb = min(SCAN_TB, t)
    nblk = t // tb
    ng = RW_WIDTH // 256
    fwd = pl.BlockSpec((nb, tb, RW_WIDTH), lambda i, j: (i, j, 0))
    bwd = pl.BlockSpec((nb, tb, RW_WIDTH), lambda i, j: (i, nblk - 1 - j, 0))
    st = pl.BlockSpec((nb, ng, 2, 256, 256), lambda i, j: (i, 0, 0, 0, 0))
    args = [p["r"], p["kk"], p["v"], p["kh_f"], p["kka_f"], p["lw_f"],
            p["r"], p["kk"], p["v"], p["kh_b"], p["kka_b"], p["lw_b"]]
    if need_y:
        in_specs, args = [fwd] * 6 + [bwd] * 6 + [st], args + [h0]
        out_specs = [fwd, bwd]
        out_shape = [jax.ShapeDtypeStruct((b, t, RW_WIDTH), BF16)] * 2
    else:
        in_specs = [fwd] * 6 + [bwd] * 6
        out_specs = st
        out_shape = jax.ShapeDtypeStruct((b, ng, 2, 256, 256), F32)
    return pl.pallas_call(
        functools.partial(_wkv_kernel, need_y, nb, tb),
        grid=(b // nb, nblk),
        in_specs=in_specs,
        out_specs=out_specs,
        out_shape=out_shape,
        scratch_shapes=[pltpu.VMEM((nb, ng, 2, 256, 256), F32)],
        compiler_params=_params(("arbitrary", "arbitrary")),
        name="wkv_y" if need_y else "wkv_state",
    )(*args)


def _column(row):
    return jnp.transpose(jnp.broadcast_to(row, (8, row.shape[1])))[:, 0:1]


def _gla_chains(need_o, ins, sts, masks, blk_k, blk_v, blk_s):
    c = CHUNK
    revs = [i[0] for i in ins]
    mk = [masks[int(rv)] for rv in revs]
    b = _each(lambda i, m: _cumsum(i[4], m[3]), ins, mk)
    bl = [bb[0:1] if rv else bb[c - 1:c] for bb, rv in zip(b, revs)]
    vb = [i[3].astype(BF16) for i in ins]
    upd = _each(lambda v, i, bb, ble: _dot_tn(v, (i[2] * jnp.exp(ble - bb)).astype(BF16)), vb, ins, b, bl)
    st_new = _each(lambda s, ble, up: s * jnp.exp(ble) + jnp.where(blk_s, up, 0.0), sts, bl, upd)
    os_ = [None] * len(ins)
    if need_o:
        q_in = _each(lambda i, bb: (i[1] * jnp.exp(bb)).astype(BF16), ins, b)
        sc = _each(lambda q, i, bb, m: jnp.where(
            m[1], _dot_nt(q, _bd((i[2] * jnp.exp(-bb)).astype(BF16), blk_k)), 0.0), q_in, ins, b, mk)
        oi = _each(lambda s, v: _dot(s, _bd(v, blk_v)), sc, vb)
        ox = _each(_dot_nt, q_in, sts)
        os_ = _each(lambda a, x: a + x, oi, ox)
    return os_, st_new


def _gla_kernel(need_o, nb, tb, *refs):
    fwd, bwd, rest = refs[:4], refs[4:8], refs[8:]
    if need_o:
        s0_ref, of_ref, ob_ref, ss = rest
    else:
        sout_ref, ss = rest
    c = CHUNK
    nc = tb // c
    j = pl.program_id(1)
    blk_k = (_iota((256, 256), 0) >> 6) == (_iota((256, 256), 1) >> 6)
    blk_v = (_iota((256, 512), 0) >> 6) == (_iota((256, 512), 1) >> 7)
    blk_s = (_iota((512, 256), 0) >> 7) == (_iota((512, 256), 1) >> 6)
    masks = (_scan_masks(False, 256), _scan_masks(True, 256))

    @pl.when(j == 0)
    def _():
        ss[...] = s0_ref[...] if need_o else jnp.zeros(ss.shape, F32)

    def body(ci, carry):
        where, ins, sts = [], [], []
        for ib in range(nb):
            for d, views in enumerate((fwd, bwd)):
                cc = ci if d == 0 else nc - 1 - ci
                rows = pl.ds(pl.multiple_of(cc * c, c), c)
                q, k, v = [ref[ib, rows, :].astype(F32) for ref in views[:3]]
                ins.append((d == 1, q, k, v, views[3][ib, rows, :]))
                sts.append(ss[ib, d])
                where.append((ib, d, rows))
        os_, sts = _gla_chains(need_o, ins, sts, masks, blk_k, blk_v, blk_s)
        for (ib, d, rows), o, st in zip(where, os_, sts):
            ss[ib, d] = st
            if need_o:
                o_ref = ob_ref if d else of_ref
                o_ref[ib, rows, :] = o.astype(BF16)
        return carry

    lax.fori_loop(0, nc, body, 0)

    if not need_o:
        @pl.when(j == pl.num_programs(1) - 1)
        def _():
            sout_ref[...] = ss[...]


def _gla_call(p, s0):
    need_o = s0 is not None
    b, t, _ = p["q"].shape
    nb = min(GLA_NB, b)
    tb = min(SCAN_TB, t)
    nblk = t // tb
    fwd = lambda w: pl.BlockSpec((nb, tb, w), lambda i, j: (i, j, 0))
    bwd = lambda w: pl.BlockSpec((nb, tb, w), lambda i, j: (i, nblk - 1 - j, 0))
    st = pl.BlockSpec((nb, 2, GLA_VAL, GLA_KEY), lambda i, j: (i, 0, 0, 0))
    in_specs = [fwd(256), fwd(256), fwd(512), fwd(256), bwd(256), bwd(256), bwd(512), bwd(256)]
    args = [p["q"], p["kg"], p["vg"], p["la_f"], p["q"], p["kg"], p["vg"], p["la_b"]]
    if need_o:
        in_specs, args = in_specs + [st], args + [s0]
        out_specs = [fwd(GLA_VAL), bwd(GLA_VAL)]
        out_shape = [jax.ShapeDtypeStruct((b, t, GLA_VAL), BF16)] * 2
    else:
        out_specs = st
        out_shape = jax.ShapeDtypeStruct((b, 2, GLA_VAL, GLA_KEY), F32)
    return pl.pallas_call(
        functools.partial(_gla_kernel, need_o, nb, tb),
        grid=(b // nb, nblk),
        in_specs=in_specs,
        out_specs=out_specs,
        out_shape=out_shape,
        scratch_shapes=[pltpu.VMEM((nb, 2, GLA_VAL, GLA_KEY), F32)],
        compiler_params=_params(("arbitrary", "arbitrary")),
        name="gla_o" if need_o else "gla_state",
    )(*args)


def _post_kernel(yf_ref, yb_ref, bon_ref, gout_ref, of_ref, ob_ref, gate_ref, x_ref, g1_ref, gnw_ref,
                 gnb_ref, nw_ref, wout_ref, l1g_ref, l1b_ref, seg64_ref, seg128_ref, out_ref):
    y = yf_ref[0].astype(F32) + yb_ref[0].astype(F32)
    seg64 = seg64_ref[...]
    mu = _dot(y, seg64) * (1.0 / RW_HEAD)
    yc = y - mu
    var = _dot(yc * yc, seg64) * (1.0 / RW_HEAD)
    y_n = yc * lax.rsqrt(var + RW_GN_EPS) * gnw_ref[...] + gnb_ref[...]
    rw_out = (y_n + bon_ref[0].astype(F32)) * gout_ref[0].astype(F32)
    o = of_ref[0].astype(F32) + ob_ref[0].astype(F32)
    ms = _dot(o * o, seg128_ref[...]) * (1.0 / GLA_DV)
    gla_out = o * lax.rsqrt(ms + GLA_NORM_EPS) * nw_ref[...] * gate_ref[0].astype(F32)
    mix = jnp.concatenate([rw_out.astype(BF16), gla_out.astype(BF16)], axis=-1)
    proj = jnp.dot(mix, wout_ref[...], preferred_element_type=F32)
    xs = DN_ALPHA * x_ref[0] + g1_ref[0] * proj
    out_ref[0] = _ln(xs) * l1g_ref[...] + l1b_ref[...]


def _post_call(yf, yb, bonus, gout, of, ob, gate, x, g1, pp):
    b, t, d = x.shape
    tm = min(t, 512)
    tile = lambda w: pl.BlockSpec((1, tm, w), lambda i, j: (i, j, 0))
    full = lambda a: pl.BlockSpec(a.shape, lambda i, j: (0,) * a.ndim)
    consts = [pp["gn_w"], pp["gn_b"], pp["gla_nw"], pp["w_out"], pp["ln1_g"], pp["ln1_b"],
              pp["seg64"], pp["seg128"]]
    return pl.pallas_call(
        _post_kernel,
        grid=(b, t // tm),
        in_specs=[tile(512)] * 7 + [tile(d), pl.BlockSpec((1, 1, d), lambda i, j: (i, 0, 0))]
        + [full(a) for a in consts],
        out_specs=tile(d),
        out_shape=jax.ShapeDtypeStruct((b, t, d), F32),
        compiler_params=_params(("arbitrary", "arbitrary")),
        name="post",
    )(yf, yb, bonus, gout, of, ob, gate, x, g1, *consts)


def _prefix_excl(m, su):
    e, t = m.shape
    off = jnp.zeros((e, 1), F32)
    parts = []
    for j in range(t // 128):
        blk = m[:, j * 128:(j + 1) * 128]
        parts.append(_dot(blk, su) + off)
        off = off + jnp.sum(blk, axis=1, keepdims=True)
    return jnp.concatenate(parts, axis=1)


ROUTE_TBK = 256
GATHER_W = 96
COMBINE_W = 128
META_ROWS = 2 * N_EXPERTS + 8


def _select_kernel(cap, gw, cw, x_ref, sh_ref, sc_ref, rw_ref, h2_ref, aff_ref, post_ref, pose_ref, meta_ref):
    t = x_ref.shape[1]
    ne = N_EXPERTS
    h2 = _ln(x_ref[0]) * (1.0 + sc_ref[0]) + sh_ref[0]
    mm = lambda p, q: jnp.dot(p, q, preferred_element_type=F32)
    h_hi = h2.astype(BF16)
    h2_ref[0] = h_hi
    h_lo = (h2 - h_hi.astype(F32)).astype(BF16)
    w_hi = rw_ref[...].astype(BF16)
    w_lo = (rw_ref[...] - w_hi.astype(F32)).astype(BF16)
    logits = mm(h_hi, w_hi) + (mm(h_lo, w_hi) + mm(h_hi, w_lo))
    lane = _iota((1, 128), 1)
    logits = jnp.where(lane < ne, logits, -1e30)
    m = jnp.max(logits, axis=-1, keepdims=True)
    ex = jnp.exp(logits - m)
    aff = ex / jnp.sum(ex, axis=-1, keepdims=True)
    aff_ref[0] = aff
    aff_t = aff.T[0:ne, :]

    def bs(_, lohi):
        lo, hi = lohi
        mid = lo + ((hi - lo) >> 1)
        mid_f = pltpu.bitcast(mid, F32)[:, 0:1]
        cnt = jnp.sum(jnp.where(aff_t >= mid_f, 1.0, 0.0), axis=1, keepdims=True)
        ok = cnt >= cap
        return jnp.where(ok, mid, lo), jnp.where(ok, hi, mid)

    lo0 = jnp.zeros((ne, 128), jnp.int32)
    hi0 = jnp.full((ne, 128), 0x3F800001, jnp.int32)
    thr_bits, _ = lax.fori_loop(0, 31, bs, (lo0, hi0))
    thr = pltpu.bitcast(thr_bits, F32)[:, 0:1]
    gt = jnp.where(aff_t > thr, 1.0, 0.0)
    eq = jnp.where(aff_t == thr, 1.0, 0.0)
    need = cap - jnp.sum(gt, axis=1, keepdims=True)
    su = jnp.where(_iota((128, 128), 0) < _iota((128, 128), 1), 1.0, 0.0).astype(BF16)
    sel = gt + eq * jnp.where(_prefix_excl(eq, su) < need, 1.0, 0.0)
    pos = jnp.where(sel > 0.0, _prefix_excl(sel, su), -1.0)
    pose_ref[0, 0:ne, :] = pos
    pose_ref[0, ne:, :] = jnp.full((128 - ne, t), -1.0, F32)
    post_ref[0] = pose_ref[0].T

    lanes = _iota((ne, 128), 1)
    first = jnp.zeros((ne, 1), F32)
    og = jnp.zeros((ne, 128), F32)
    oc = jnp.zeros((ne, 128), F32)
    fits = jnp.ones((ne, 1), F32)
    for j in range(t // ROUTE_TBK):
        n = jnp.sum(sel[:, j * ROUTE_TBK:(j + 1) * ROUTE_TBK], axis=1, keepdims=True)
        a16 = jnp.floor(first * (1.0 / 16.0)) * 16.0
        sg = jnp.minimum(a16, float(cap - gw))
        sc_ = jnp.minimum(a16, float(cap - cw))
        fits = fits * jnp.where(first - sg + n <= gw, 1.0, 0.0) * jnp.where(first - sc_ + n <= cw, 1.0, 0.0)
        og = jnp.where(lanes == j, sg, og)
        oc = jnp.where(lanes == j, sc_, oc)
        first = first + n
    meta_ref[0, 0:ne, :] = og.astype(jnp.int32)
    meta_ref[0, ne:2 * ne, :] = oc.astype(jnp.int32)
    meta_ref[0, 2 * ne:, :] = jnp.broadcast_to(jnp.min(fits, axis=0, keepdims=True), (8, 128)).astype(jnp.int32)


def _select_call(x1, sh2, sc2, router_pad, cap, gw, cw):
    b, t, d = x1.shape
    row = lambda w: pl.BlockSpec((1, t, w), lambda i: (i, 0, 0))
    vec = pl.BlockSpec((1, 1, d), lambda i: (i, 0, 0))
    return pl.pallas_call(
        functools.partial(_select_kernel, cap, gw, cw),
        grid=(b,),
        in_specs=[row(d), vec, vec, pl.BlockSpec((d, 128), lambda i: (0, 0))],
        out_specs=[row(d), row(128), row(128), pl.BlockSpec((1, 128, t), lambda i: (i, 0, 0)),
                   pl.BlockSpec((1, META_ROWS, 128), lambda i: (i, 0, 0))],
        out_shape=[jax.ShapeDtypeStruct((b, t, d), BF16),
                   jax.ShapeDtypeStruct((b, t, 128), F32),
                   jax.ShapeDtypeStruct((b, t, 128), F32),
                   jax.ShapeDtypeStruct((b, 128, t), F32),
                   jax.ShapeDtypeStruct((b, META_ROWS, 128), jnp.int32)],
        compiler_params=_params(("arbitrary",)),
        name="select",
    )(x1, sh2, sc2, router_pad)


def _gather_kernel(cap, gw, nblk, og_ref, ok_ref, h2_ref, pose_ref, xg_ref):
    b = pl.program_id(0)
    t = h2_ref.shape[1]
    ne = N_EXPERTS

    @pl.when(ok_ref[b] != 0)
    def _():
        xg_ref[...] = jnp.zeros(xg_ref.shape, BF16)
        srow = _iota((gw, 1), 0).astype(F32)
        for j in range(nblk):
            tok = slice(j * ROUTE_TBK, (j + 1) * ROUTE_TBK)
            starts = [pl.multiple_of(og_ref[(b * ne + e) * nblk + j], 16) for e in range(ne)]
            oh = jnp.concatenate(
                [jnp.where(pose_ref[0, e:e + 1, tok] == srow + starts[e].astype(F32), 1.0, 0.0).astype(BF16)
                 for e in range(ne)], axis=0)
            res = jnp.dot(oh, h2_ref[0, tok, :], preferred_element_type=F32).astype(BF16)
            for e in range(ne):
                rows = pl.ds(starts[e], gw)
                xg_ref[0, e, rows, :] = xg_ref[0, e, rows, :] + res[e * gw:(e + 1) * gw]

    @pl.when(ok_ref[b] == 0)
    def _():
        slot = _iota((cap, t), 0).astype(F32)
        for e in range(ne):
            onehot = jnp.where(pose_ref[0, e:e + 1, :] == slot, 1.0, 0.0).astype(BF16)
            xg_ref[0, e] = jnp.dot(onehot, h2_ref[0], preferred_element_type=F32).astype(BF16)


def _gather_call(h2, pos_e, og, ok, cap, gw):
    b, t, d = h2.shape
    nblk = t // ROUTE_TBK
    grid_spec = pltpu.PrefetchScalarGridSpec(
        num_scalar_prefetch=2,
        grid=(b,),
        in_specs=[pl.BlockSpec((1, t, d), lambda i, *_: (i, 0, 0)),
                  pl.BlockSpec((1, 128, t), lambda i, *_: (i, 0, 0))],
        out_specs=pl.BlockSpec((1, N_EXPERTS, cap, d), lambda i, *_: (i, 0, 0, 0)),
    )
    return pl.pallas_call(
        functools.partial(_gather_kernel, cap, gw, nblk),
        grid_spec=grid_spec,
        out_shape=jax.ShapeDtypeStruct((b, N_EXPERTS, cap, d), BF16),
        compiler_params=_params(("arbitrary",)),
        name="gather",
    )(og, ok, h2, pos_e)


FFN_NS = 2


def _ffn_kernel(nb, x_ref, wg_ref, wu_ref, wd_ref, y_ref, wg_s, wu_s, wd_s):
    p = pl.program_id(0)
    b = pl.program_id(1)
    ne = pl.num_programs(0) - 1
    rg = wg_s.shape[1] // nb
    rd = wd_s.shape[1] // nb
    og = pl.multiple_of(b * rg, rg)
    od = pl.multiple_of(b * rd, rd)

    for half in (0, 1):
        @pl.when((p < ne) & (p % 2 == half))
        def _():
            wg_s[half, pl.ds(og, rg), :] = wg_ref[0].astype(BF16)
            wu_s[half, pl.ds(og, rg), :] = wu_ref[0].astype(BF16)
            wd_s[half, pl.ds(od, rd), :] = wd_ref[0].astype(BF16)

        @pl.when((p > 0) & (p % 2 != half))
        def _():
            for s in range(x_ref.shape[0]):
                x = x_ref[s, 0]
                g = jnp.dot(x, wg_s[half], preferred_element_type=F32)
                u = jnp.dot(x, wu_s[half], preferred_element_type=F32)
                h = (g * _sigmoid(g) * u).astype(BF16)
                y_ref[s, 0] = jnp.dot(h, wd_s[half], preferred_element_type=F32).astype(BF16)

    @pl.when(p == 0)
    def _():
        y_ref[...] = jnp.zeros(y_ref.shape, BF16)


def _ffn_call(xg, wg, wu, wd):
    b, ne, cap, d = xg.shape
    f = wg.shape[2]
    ns = min(FFN_NS, b)
    n = b // ns
    assert d % (16 * n) == 0 and f % (16 * n) == 0, "weight chunks must be whole bf16 sublane tiles"
    wmap = lambda p, i: (jnp.minimum(p, ne - 1), jnp.where(p < ne, i, n - 1), 0)
    xmap = lambda p, i: (jnp.where(p > 0, i, 0), jnp.maximum(p - 1, 0), 0, 0)
    return pl.pallas_call(
        functools.partial(_ffn_kernel, n),
        grid=(ne + 1, n),
        in_specs=[pl.BlockSpec((ns, 1, cap, d), xmap),
                  pl.BlockSpec((1, d // n, f), wmap),
                  pl.BlockSpec((1, d // n, f), wmap),
                  pl.BlockSpec((1, f // n, d), wmap)],
        out_specs=pl.BlockSpec((ns, 1, cap, d), xmap),
        out_shape=jax.ShapeDtypeStruct((b, ne, cap, d), BF16),
        scratch_shapes=[pltpu.VMEM((2, d, f), BF16), pltpu.VMEM((2, d, f), BF16), pltpu.VMEM((2, f, d), BF16)],
        compiler_params=_params(("arbitrary", "arbitrary")),
        name="ffn",
    )(xg, wg, wu, wd)


def _combine_kernel(cap, cw, nblk, nsub, oc_ref, ok_ref, y_ref, pos_ref, aff_ref, x_ref, g2_ref, l2g_ref,
                    l2b_ref, out_ref):
    b = pl.program_id(0)
    j = pl.program_id(1)
    ne = N_EXPERTS
    pos = pos_ref[0]
    gates = aff_ref[0]

    def finish(acc, rows):
        xs = DN_ALPHA * x_ref[0, rows, :] + g2_ref[0] * acc
        out_ref[0, rows, :] = _ln(xs) * l2g_ref[...] + l2b_ref[...]

    @pl.when(ok_ref[b] != 0)
    def _():
        lane = _iota((1, cw), 1).astype(F32)
        for jj in range(nsub):
            rows = slice(jj * ROUTE_TBK, (jj + 1) * ROUTE_TBK)
            ohs, ys = [], []
            for e in range(ne):
                a = pl.multiple_of(oc_ref[(b * ne + e) * nblk + j * nsub + jj], 16)
                ohs.append(jnp.where(pos[rows, e:e + 1] == lane + a.astype(F32),
                                     gates[rows, e:e + 1], 0.0).astype(BF16))
                ys.append(y_ref[0, e, pl.ds(a, cw), :])
            finish(jnp.dot(jnp.concatenate(ohs, axis=1), jnp.concatenate(ys, axis=0),
                           preferred_element_type=F32), rows)

    @pl.when(ok_ref[b] == 0)
    def _():
        slot = _iota((1, cap), 1).astype(F32)
        acc = None
        for e in range(ne):
            oh = jnp.where(pos[:, e:e + 1] == slot, 1.0, 0.0).astype(BF16)
            part = jnp.dot(oh, y_ref[0, e], preferred_element_type=F32) * gates[:, e:e + 1]
            acc = part if acc is None else acc + part
        finish(acc, slice(None))


def _combine_call(y, pos_t, aff_t, x1, g2, l2g, l2b, oc, ok, cap, cw):
    b, t, d = x1.shape
    tm = min(t, 2 * ROUTE_TBK)
    nblk = t // ROUTE_TBK
    nsub = tm // ROUTE_TBK
    grid_spec = pltpu.PrefetchScalarGridSpec(
        num_scalar_prefetch=2,
        grid=(b, t // tm),
        in_specs=[pl.BlockSpec((1, N_EXPERTS, cap, d), lambda i, j, *_: (i, 0, 0, 0)),
                  pl.BlockSpec((1, tm, 128), lambda i, j, *_: (i, j, 0)),
                  pl.BlockSpec((1, tm, 128), lambda i, j, *_: (i, j, 0)),
                  pl.BlockSpec((1, tm, d), lambda i, j, *_: (i, j, 0)),
                  pl.BlockSpec((1, 1, d), lambda i, j, *_: (i, 0, 0)),
                  pl.BlockSpec((1, d), lambda i, j, *_: (0, 0)),
                  pl.BlockSpec((1, d), lambda i, j, *_: (0, 0))],
        out_specs=pl.BlockSpec((1, tm, d), lambda i, j, *_: (i, j, 0)),
    )
    return pl.pallas_call(
        functools.partial(_combine_kernel, cap, cw, nblk, nsub),
        grid_spec=grid_spec,
        out_shape=jax.ShapeDtypeStruct((b, t, d), F32),
        compiler_params=_params(("arbitrary", "arbitrary")),
        name="combine",
    )(oc, ok, y, pos_t, aff_t, x1, g2, l2g, l2b)


def _pad_cols(parts, total):
    rows = parts[0][0].shape[0]
    out = jnp.zeros((rows, total), parts[0][0].dtype)
    for a, off in parts:
        out = lax.dynamic_update_slice(out, a, (0, off))
    return out


def _block_ones(n, width):
    i = jnp.arange(n) // width
    return (i[:, None] == i[None, :]).astype(BF16)


def _layout_params(w_in, rw_mu, rw_w0, rw_w2, rw_a0, rw_a2, rw_g2, rw_k_k, rw_k_a, rw_r_k,
                   rw_gn_w, rw_gn_b, gla_a2, gla_a_b, gla_norm_w, w_out, ln1_g, ln1_b):
    rw_in = 1760
    segs = [(0, 1536, COL_R), (1536, 1664, COL_LORA), (1664, 1760, COL_GD),
            (rw_in, rw_in + 256, COL_Q), (rw_in + 256, rw_in + 512, COL_KG),
            (rw_in + 512, rw_in + 1024, COL_VG), (rw_in + 1024, rw_in + 1536, COL_GG),
            (rw_in + 1536, rw_in + 1568, COL_GA)]
    w_pad = _pad_cols([(w_in[:, a:b], off) for a, b, off in segs], IN_PAD).astype(BF16)
    mu = _pad_cols([(rw_mu[None, a:b], off) for a, b, off in segs[:3]], RW_PAD)
    w_lora = jnp.zeros((128, 2048), F32)
    for i, m in enumerate((rw_w2[0], rw_w2[1], rw_a2[0], rw_a2[1])):
        w_lora = lax.dynamic_update_slice(w_lora, m, (32 * i, 512 * i))
    g2 = jnp.zeros((128, 512), F32).at[:96].set(rw_g2)
    a2 = jnp.zeros((128, 512), F32).at[0:16, 0:256].set(gla_a2[0]).at[16:32, 256:512].set(gla_a2[1])
    row = lambda a: a.reshape(1, -1)
    return dict(
        w_in=w_pad, mu=mu, k_k=row(rw_k_k), k_a=row(rw_k_a), r_k=row(rw_r_k), w0=rw_w0, a0=rw_a0,
        ab=gla_a_b, w_lora=w_lora.astype(BF16), g2=g2.astype(BF16), a2=a2.astype(BF16),
        seg64=_block_ones(512, 64), seg128=_block_ones(512, 128),
        gn_w=row(rw_gn_w), gn_b=row(rw_gn_b), gla_nw=row(gla_norm_w), w_out=w_out.astype(BF16),
        ln1_g=row(ln1_g), ln1_b=row(ln1_b))


def kernel(x, c, ctx, c_ctx, ada_w, ada_b, w_in, rw_mu, rw_w0, rw_w2, rw_a0, rw_a2, rw_g2, rw_k_k, rw_k_a, rw_r_k, rw_gn_w, rw_gn_b, gla_a2, gla_a_b, gla_norm_w, w_out, ln1_g, ln1_b, router_w, ex_gate, ex_up, ex_down, ln2_g, ln2_b):
    assert ada_w.shape[0] == 1, "single-layer block"
    b, t, d = x.shape
    cap = CAPACITY_FACTOR * t // N_EXPERTS
    pp = _layout_params(w_in[0], rw_mu[0], rw_w0[0], rw_w2[0], rw_a0[0], rw_a2[0], rw_g2[0],
                        rw_k_k[0], rw_k_a[0], rw_r_k[0], rw_gn_w[0], rw_gn_b[0], gla_a2[0],
                        gla_a_b[0], gla_norm_w[0], w_out[0], ln1_g[0], ln1_b[0])

    rows = -(-(b + 1) // 8) * 8
    cs = jnp.zeros((rows, d), F32).at[:b].set(c).at[b].set(c_ctx)
    mod = _mod_call(cs, ada_w[0], ada_b[0][None])
    sh1, sc1, g1, sh2, sc2, g2 = [m[:, None, :] for m in jnp.split(mod[:b], 6, axis=-1)]
    sh1c, sc1c = [jnp.broadcast_to(m[None, None, :], (b, 1, d)) for m in jnp.split(mod[b], 6)[:2]]

    pc = _front_call(ctx, sh1c, sc1c, False, pp)
    h_ctx = _wkv_call(pc, None)
    s_ctx = _gla_call(pc, None)

    pz = _front_call(x, sh1, sc1, True, pp)
    yf, yb = _wkv_call(pz, h_ctx)
    of, ob = _gla_call(pz, s_ctx)
    x1 = _post_call(yf, yb, pz["bonus"], pz["gout"], of, ob, pz["gate"], x, g1, pp)

    router_pad = jnp.zeros((d, 128), F32).at[:, :N_EXPERTS].set(router_w[0])
    gw, cw = min(GATHER_W, cap), min(COMBINE_W, cap)
    nblk = t // ROUTE_TBK
    h2, aff_t, pos_t, pos_e, meta = _select_call(x1, sh2, sc2, router_pad, cap, gw, cw)
    og = meta[:, 0:N_EXPERTS, 0:nblk].reshape(-1)
    oc = meta[:, N_EXPERTS:2 * N_EXPERTS, 0:nblk].reshape(-1)
    ok = meta[:, 2 * N_EXPERTS, 0]
    xg = _gather_call(h2, pos_e, og, ok, cap, gw)
    ye = _ffn_call(xg, ex_gate[0], ex_up[0], ex_down[0])
    return _combine_call(ye, pos_t, aff_t, x1, g2, ln2_g[0][None], ln2_b[0][None], oc, ok, cap, cw)
```
